```python
import jax, jax.numpy as jnp
from jax import lax
import numpy as np

D_MODEL = 1024
BATCH = 32
SEQ = 256
DEPTH = 4
DEC_BATCH = 8
DEC_SEQ = 4096
PAST_LEN = 512

GRID_W = 64
HEAD_DIM = 64
N_BRANCH = 4
BRANCH_W = 256
NA_HEADS = 4
NA_KR_MAX = 8
NA_KC = 16
ML_HEADS = 4
ML_CHUNK = 128
SW_HEADS = 4
SW_KV_HEADS = 2
SW_GROUP = SW_HEADS // SW_KV_HEADS
SW_WINDOW = 128
SW_BLOCK = 128
MLA_HEADS = 4
MLA_Q_RANK = 192
MLA_KV_RANK = 128
MLA_NOPE = 64
MLA_ROPE = 32
MLA_V = 64
MLA_SCALE = (MLA_NOPE + MLA_ROPE) ** -0.5
PEER_HEADS = 8
PEER_KEY_DIM = 64
PEER_N_KEYS = 128
PEER_TOPK = 16
PEER_N_EXPERTS = PEER_N_KEYS * PEER_N_KEYS
PEER_BLOCK = 128
Q_BLOCK = 128
ROPE_BASE = 10000.0
RMS_EPS = 1e-6
NEG_INF = -1e30
FORGET_BIAS = 3.0
IN_SPLITS = (NA_HEADS * HEAD_DIM, NA_HEADS * HEAD_DIM, NA_HEADS * HEAD_DIM,
             ML_HEADS * HEAD_DIM, ML_HEADS * HEAD_DIM, ML_HEADS * HEAD_DIM, ML_HEADS * HEAD_DIM,
             2 * ML_HEADS, 2 * ML_HEADS,
             SW_HEADS * HEAD_DIM, SW_KV_HEADS * HEAD_DIM, SW_KV_HEADS * HEAD_DIM,
             MLA_Q_RANK, MLA_KV_RANK, MLA_ROPE,
             N_BRANCH * D_MODEL)
IN_WIDTH = sum(IN_SPLITS)

kernel_name = 'hybrid_diffusion_trunk_step'


def rmsnorm(x, g):
    xf = x.astype(jnp.float32)
    y = xf * lax.rsqrt(jnp.mean(xf * xf, axis=-1, keepdims=True) + RMS_EPS)
    return (y * g.astype(jnp.float32)).astype(x.dtype)


def rms_unit(x):
    xf = x.astype(jnp.float32)
    return (xf * lax.rsqrt(jnp.mean(xf * xf, axis=-1, keepdims=True) + RMS_EPS)).astype(x.dtype)


def modulate(x, g, shift, scale):
    return rmsnorm(x, g) * (1 + scale) + shift


def adaln(cond, w_mod, b_mod):
    return jnp.split(jax.nn.silu(cond) @ w_mod + b_mod, 6, axis=-1)


def rope_1d(x, pos):
    half = x.shape[-1] // 2
    freqs = ROPE_BASE ** (-jnp.arange(half, dtype=jnp.float32) / half)
    ang = pos.astype(jnp.float32)[:, None] * freqs[None, :]
    ang = ang.reshape((ang.shape[0],) + (1,) * (x.ndim - 3) + (half,))
    cos, sin = jnp.cos(ang), jnp.sin(ang)
    x1 = x[..., :half].astype(jnp.float32)
    x2 = x[..., half:].astype(jnp.float32)
    return jnp.concatenate([x1 * cos - x2 * sin, x1 * sin + x2 * cos], axis=-1).astype(x.dtype)


def rope2d(x, rows, cols):
    d = x.shape[-1]
    return jnp.concatenate([rope_1d(x[..., :d // 2], rows), rope_1d(x[..., d // 2:], cols)], axis=-1)


def split_proj(p):
    idx = [int(v) for v in np.cumsum(IN_SPLITS)[:-1]]
    return jnp.split(p, idx, axis=-1)


def dense_attention(q, k, v, scale, sink=None):
    B, Tq, G, R, d = q.shape
    Tk = k.shape[1]
    nblk = Tq // Q_BLOCK
    qb = jnp.moveaxis(q.reshape(B, nblk, Q_BLOCK, G, R, d), 1, 0)

    def block(qi):
        s = jnp.einsum('bqgrd,bkgd->bgrqk', qi, k).astype(jnp.float32) * scale
        if sink is not None:
            s_sink = jnp.broadcast_to(sink.astype(jnp.float32)[None, :, :, None, None], s.shape[:-1] + (1,))
            s = jnp.concatenate([s, s_sink], axis=-1)
        pr = jax.nn.softmax(s, axis=-1)[..., :Tk].astype(v.dtype)
        return jnp.einsum('bgrqk,bkgd->bqgrd', pr, v)

    out = lax.map(block, qb)
    return jnp.moveaxis(out, 0, 1).reshape(B, Tq, -1)


def window_attention(q, k, v, k_ctx, v_ctx, sink):
    B, T, G, R, d = q.shape
    nb = T // SW_BLOCK
    scale = d ** -0.5
    pad = ((0, 0), (SW_BLOCK, SW_BLOCK), (0, 0), (0, 0))
    kp = jnp.pad(k, pad).reshape(B, nb + 2, SW_BLOCK, G, d)
    vp = jnp.pad(v, pad).reshape(B, nb + 2, SW_BLOCK, G, d)
    kb = jnp.concatenate([kp[:, :-2], kp[:, 1:-1], kp[:, 2:]], axis=2)
    vb = jnp.concatenate([vp[:, :-2], vp[:, 1:-1], vp[:, 2:]], axis=2)
    qb = q.reshape(B, nb, SW_BLOCK, G, R, d)
    s_loc = jnp.einsum('bnqgrd,bnkgd->bngrqk', qb, kb).astype(jnp.float32) * scale
    blk = jnp.arange(nb)
    qpos = blk[:, None] * SW_BLOCK + jnp.arange(SW_BLOCK)[None, :]
    kpos = (blk[:, None] - 1) * SW_BLOCK + jnp.arange(3 * SW_BLOCK)[None, :]
    valid = ((jnp.abs(qpos[:, :, None] - kpos[:, None, :]) <= SW_WINDOW)
             & (kpos[:, None, :] >= 0) & (kpos[:, None, :] < T))
    s_loc = jnp.where(valid[None, :, None, None], s_loc, NEG_INF)
    s_ctx = jnp.einsum('bnqgrd,blgd->bngrql', qb, k_ctx).astype(jnp.float32) * scale
    s_sink = jnp.broadcast_to(sink.astype(jnp.float32)[None, None, :, :, None, None], s_loc.shape[:-1] + (1,))
    pr = jax.nn.softmax(jnp.concatenate([s_loc, s_ctx, s_sink], axis=-1), axis=-1).astype(v.dtype)
    nk = 3 * SW_BLOCK
    lc = k_ctx.shape[1]
    out = (jnp.einsum('bngrqk,bnkgd->bnqgrd', pr[..., :nk], vb)
           + jnp.einsum('bngrql,blgd->bnqgrd', pr[..., nk:nk + lc], v_ctx))
    return out.reshape(B, T, G * R * d)


def neighbourhood_attention(q, k, v, k_ctx, v_ctx, rpb):
    B, T, H, d = q.shape
    rows_n = T // GRID_W
    kr = min(NA_KR_MAX, rows_n)
    scale = d ** -0.5
    r = jnp.arange(rows_n)
    row_idx = jnp.clip(r - kr // 2, 0, rows_n - kr)[:, None] + jnp.arange(kr)[None, :]
    c = jnp.arange(GRID_W)
    col_start = jnp.clip(c - NA_KC // 2, 0, GRID_W - NA_KC)
    col_ok = (c[None, :] >= col_start[:, None]) & (c[None, :] < col_start[:, None] + NA_KC)
    dr = row_idx - r[:, None] + (NA_KR_MAX - 1)
    dc = jnp.clip(c[None, :] - c[:, None] + (NA_KC - 1), 0, 2 * NA_KC - 2)
    bias = rpb.astype(jnp.float32)[:, dr[:, :, None, None], dc[None, None, :, :]]
    bias = jnp.where(col_ok[None, None, None], bias, NEG_INF).transpose(0, 1, 3, 2, 4)
    qg = q.reshape(B, rows_n, GRID_W, H, d)
    kg = k.reshape(B, rows_n, GRID_W, H, d)[:, row_idx]
    vg = v.reshape(B, rows_n, GRID_W, H, d)[:, row_idx]
    s_nb = jnp.einsum('brchd,brkwhd->bhrckw', qg, kg).astype(jnp.float32) * scale + bias[None]
    s_nb = s_nb.reshape(B, H, rows_n, GRID_W, kr * GRID_W)
    s_ctx = jnp.einsum('brchd,blhd->bhrcl', qg, k_ctx).astype(jnp.float32) * scale
    pr = jax.nn.softmax(jnp.concatenate([s_nb, s_ctx], axis=-1), axis=-1).astype(v.dtype)
    p_nb = pr[..., :kr * GRID_W].reshape(B, H, rows_n, GRID_W, kr, GRID_W)
    p_ctx = pr[..., kr * GRID_W:]
    out = (jnp.einsum('bhrckw,brkwhd->brchd', p_nb, vg)
           + jnp.einsum('bhrcl,blhd->brchd', p_ctx, v_ctx))
    return out.reshape(B, T, H * d)


def mlstm_scan(q, k, v, i_pre, f_pre, C0, n0, m0):
    B, T, H, d = q.shape
    nc = T // ML_CHUNK

    def chunks(a):
        a = a.astype(jnp.float32).reshape((B, nc, ML_CHUNK, H) + a.shape[3:])
        return jnp.moveaxis(a, (1, 3), (0, 2))

    qc, kc, vc = chunks(q), chunks(k) * (d ** -0.5), chunks(v)
    ic, fc = chunks(i_pre), chunks(f_pre)
    causal = jnp.tril(jnp.ones((ML_CHUNK, ML_CHUNK), dtype=bool))

    def step(carry, xs):
        C, n, m = carry
        qx, kx, vx, ix, fx = xs
        b = jnp.cumsum(jax.nn.log_sigmoid(fx), axis=-1)
        dmat = jnp.where(causal, b[..., :, None] - b[..., None, :] + ix[..., None, :], -jnp.inf)
        inter = b + m[..., None]
        m_t = jnp.maximum(inter, dmat.max(axis=-1))
        w_intra = jnp.exp(dmat - m_t[..., None])
        w_inter = jnp.exp(inter - m_t)
        s = jnp.einsum('bhtd,bhsd->bhts', qx, kx) * w_intra
        num = (jnp.einsum('bhts,bhsv->bhtv', s, vx)
               + w_inter[..., None] * jnp.einsum('bhtd,bhdv->bhtv', qx, C))
        den = s.sum(axis=-1) + w_inter * jnp.einsum('bhtd,bhd->bht', qx, n)
        h = num / jnp.maximum(jnp.abs(den), jnp.exp(-m_t))[..., None]
        b_last = b[..., -1]
        g = b_last[..., None] - b + ix
        m_new = jnp.maximum(b_last + m, g.max(axis=-1))
        w = jnp.exp(g - m_new[..., None])
        decay = jnp.exp(b_last + m - m_new)
        C_new = decay[..., None, None] * C + jnp.einsum('bhs,bhsd,bhsv->bhdv', w, kx, vx)
        n_new = decay[..., None] * n + jnp.einsum('bhs,bhsd->bhd', w, kx)
        return (C_new, n_new, m_new), h

    init = (C0.astype(jnp.float32), n0.astype(jnp.float32), m0.astype(jnp.float32))
    (C, n, m), h = lax.scan(step, init, (qc, kc, vc, ic, fc))
    h = jnp.moveaxis(h, (0, 2), (1, 3)).reshape(B, T, H, -1)
    return h, C, n, m


def mlstm_bidir(q, k, v, i_pre, f_pre, C0, n0, m0):
    hf, Cf, nf, mf = mlstm_scan(q, k, v, i_pre[:, :, 0], f_pre[:, :, 0], C0[:, 0], n0[:, 0], m0[:, 0])
    fl = lambda a: jnp.flip(a, axis=1)
    hb, Cb, nb, mb = mlstm_scan(fl(q), fl(k), fl(v), fl(i_pre[:, :, 1]), fl(f_pre[:, :, 1]),
                                C0[:, 1], n0[:, 1], m0[:, 1])
    h = hf + fl(hb)
    return h, jnp.stack([Cf, Cb], axis=1), jnp.stack([nf, nb], axis=1), jnp.stack([mf, mb], axis=1)


def mlstm_branch(ml_q, ml_k, ml_v, ml_o, ml_i, ml_f, C0, n0, m0):
    B, T, _ = ml_q.shape
    hd = lambda a: a.reshape(B, T, ML_HEADS, HEAD_DIM)
    h, C, n, m = mlstm_bidir(hd(ml_q), hd(ml_k), hd(ml_v), ml_i.reshape(B, T, 2, ML_HEADS),
                             ml_f.reshape(B, T, 2, ML_HEADS), C0, n0, m0)
    out = rms_unit(h) * jax.nn.sigmoid(hd(ml_o).astype(jnp.float32))
    return out.reshape(B, T, -1).astype(ml_q.dtype), C, n, m


def mla_kv(ckv_n, krope, w_uk, w_uv):
    B, L, _ = ckv_n.shape
    k_nope = (ckv_n @ w_uk).reshape(B, L, MLA_HEADS, MLA_NOPE)
    v = (ckv_n @ w_uv).reshape(B, L, MLA_HEADS, MLA_V)
    k = jnp.concatenate([k_nope, jnp.broadcast_to(krope[:, :, None, :], (B, L, MLA_HEADS, MLA_ROPE)).astype(k_nope.dtype)], axis=-1)
    return k, v


def merge_branches(outs, gate_pre, w_branch, w_out):
    acc = jax.nn.sigmoid(gate_pre[..., :D_MODEL]) * (outs[0] @ w_branch[0])
    for n in range(1, N_BRANCH):
        g = jax.nn.sigmoid(gate_pre[..., n * D_MODEL:(n + 1) * D_MODEL])
        acc = acc + g * (outs[n] @ w_branch[n])
    return acc @ w_out


def peer_ffn(h, wq, keys, u, v):
    B, T, D = h.shape
    nblk = (B * T) // PEER_BLOCK

    def block(xb):
        q = (xb @ wq).reshape(PEER_BLOCK, PEER_HEADS, 2, PEER_KEY_DIM)
        s = jnp.einsum('phxd,hxnd->phxn', q, keys).astype(jnp.float32)
        s1, i1 = lax.top_k(s[:, :, 0], PEER_TOPK)
        s2, i2 = lax.top_k(s[:, :, 1], PEER_TOPK)
        cand = (s1[..., :, None] + s2[..., None, :]).reshape(PEER_BLOCK, PEER_HEADS, PEER_TOPK * PEER_TOPK)
        cidx = (i1[..., :, None] * PEER_N_KEYS + i2[..., None, :]).reshape(PEER_BLOCK, PEER_HEADS, PEER_TOPK * PEER_TOPK)
        top, pos = lax.top_k(cand, PEER_TOPK)
        idx = jnp.take_along_axis(cidx, pos, axis=-1)
        g = jax.nn.softmax(top, axis=-1)
        ue, ve = u[idx], v[idx]
        a = jax.nn.gelu(jnp.einsum('pd,phkd->phk', xb, ue).astype(jnp.float32))
        return jnp.einsum('phk,phkd->pd', (g * a).astype(xb.dtype), ve)

    out = lax.map(block, h.reshape(nblk, PEER_BLOCK, D))
    return out.reshape(B, T, D)


def context_mixers(h, p):
    B, L, _ = h.shape
    (na_q, na_k, na_v, ml_q, ml_k, ml_v, ml_o, ml_i, ml_f,
     sw_q, sw_k, sw_v, cq, ckv, krope, gate_pre) = split_proj(h @ p['w_in'] + p['b_in'])
    heads = lambda a, n: a.reshape(B, L, n, HEAD_DIM)
    hd = HEAD_DIM
    na_k, na_v = heads(na_k, NA_HEADS), heads(na_v, NA_HEADS)
    o_na = dense_attention(na_q.reshape(B, L, NA_HEADS, 1, hd), na_k, na_v, hd ** -0.5)
    C0 = jnp.zeros((B, 2, ML_HEADS, hd, hd), jnp.float32)
    n0 = jnp.zeros((B, 2, ML_HEADS, hd), jnp.float32)
    m0 = jnp.zeros((B, 2, ML_HEADS), jnp.float32)
    o_ml, C, n, m = mlstm_branch(ml_q, ml_k, ml_v, ml_o, ml_i, ml_f, C0, n0, m0)
    sw_k, sw_v = heads(sw_k, SW_KV_HEADS), heads(sw_v, SW_KV_HEADS)
    o_sw = dense_attention(sw_q.reshape(B, L, SW_KV_HEADS, SW_GROUP, hd), sw_k, sw_v, hd ** -0.5,
                           sink=p['sw_sink'].reshape(SW_KV_HEADS, SW_GROUP))
    q_mla = (rmsnorm(cq, p['mla_q_norm']) @ p['w_uq']).reshape(B, L, MLA_HEADS, MLA_NOPE + MLA_ROPE)
    ckv_n = rmsnorm(ckv, p['mla_kv_norm'])
    k_mla, v_mla = mla_kv(ckv_n, krope, p['w_uk'], p['w_uv'])
    o_mla = dense_attention(q_mla[:, :, :, None, :], k_mla, v_mla, MLA_SCALE)
    out = merge_branches([o_na, o_ml, o_sw, o_mla], gate_pre, p['w_branch'], p['w_out'])
    return out, (na_k, na_v, C, n, m, sw_k, sw_v, ckv_n, krope)


def latent_mixers(h, p, cache):
    na_k_c, na_v_c, ml_C, ml_n, ml_m, sw_k_c, sw_v_c, ckv_c, krope_c = cache
    B, T, _ = h.shape
    pos = jnp.arange(T)
    rows, cols = pos // GRID_W, pos % GRID_W
    (na_q, na_k, na_v, ml_q, ml_k, ml_v, ml_o, ml_i, ml_f,
     sw_q, sw_k, sw_v, cq, ckv, krope, gate_pre) = split_proj(h @ p['w_in'] + p['b_in'])
    heads = lambda a, n: a.reshape(B, T, n, HEAD_DIM)
    o_na = neighbourhood_attention(heads(na_q, NA_HEADS), heads(na_k, NA_HEADS), heads(na_v, NA_HEADS),
                                   na_k_c, na_v_c, p['na_rpb'])
    o_ml, _, _, _ = mlstm_branch(ml_q, ml_k, ml_v, ml_o, ml_i, ml_f, ml_C, ml_n, ml_m)
    q_sw = rope2d(sw_q.reshape(B, T, SW_KV_HEADS, SW_GROUP, HEAD_DIM), rows, cols)
    k_sw = rope2d(heads(sw_k, SW_KV_HEADS), rows, cols)
    o_sw = window_attention(q_sw, k_sw, heads(sw_v, SW_KV_HEADS), sw_k_c, sw_v_c,
                            p['sw_sink'].reshape(SW_KV_HEADS, SW_GROUP))
    q_mla = (rmsnorm(cq, p['mla_q_norm']) @ p['w_uq']).reshape(B, T, MLA_HEADS, MLA_NOPE + MLA_ROPE)
    q_mla = jnp.concatenate([q_mla[..., :MLA_NOPE], rope2d(q_mla[..., MLA_NOPE:], rows, cols)], axis=-1)
    k_lat, v_lat = mla_kv(rmsnorm(ckv, p['mla_kv_norm']), rope2d(krope, rows, cols), p['w_uk'], p['w_uv'])
    k_ctx, v_ctx = mla_kv(ckv_c, krope_c, p['w_uk'], p['w_uv'])
    o_mla = dense_attention(q_mla[:, :, :, None, :], jnp.concatenate([k_lat, k_ctx.astype(k_lat.dtype)], axis=1),
                            jnp.concatenate([v_lat, v_ctx.astype(v_lat.dtype)], axis=1), MLA_SCALE)
    return merge_branches([o_na, o_ml, o_sw, o_mla], gate_pre, p['w_branch'], p['w_out'])


def setup_inputs(seed: int = 0) -> dict:
    key = jax.random.key(seed)
    ks = iter(jax.random.split(key, 64))
    nrm = lambda shape, s: s * jax.random.normal(next(ks), shape, jnp.float32)
    hd = HEAD_DIM
    f_off = int(np.cumsum(IN_SPLITS)[7])
    b_in = nrm((DEPTH, IN_WIDTH), 0.02).at[:, f_off:f_off + 2 * ML_HEADS].add(FORGET_BIAS)
    return {
        'x_prompt': nrm((BATCH, SEQ, D_MODEL), 1.0),
        'x_sample': nrm((DEC_BATCH, DEC_SEQ, D_MODEL), 1.0),
        'cache_na_k': nrm((DEC_BATCH, DEPTH, PAST_LEN, NA_HEADS, hd), 1.0),
        'cache_na_v': nrm((DEC_BATCH, DEPTH, PAST_LEN, NA_HEADS, hd), 1.0),
        'state_mlstm_C': nrm((DEC_BATCH, DEPTH, 2, ML_HEADS, hd, hd), 0.1),
        'state_mlstm_n': nrm((DEC_BATCH, DEPTH, 2, ML_HEADS, hd), 0.3),
        'state_mlstm_m': nrm((DEC_BATCH, DEPTH, 2, ML_HEADS), 1.0),
        'cache_swa_k': nrm((DEC_BATCH, DEPTH, PAST_LEN, SW_KV_HEADS, hd), 1.0),
        'cache_swa_v': nrm((DEC_BATCH, DEPTH, PAST_LEN, SW_KV_HEADS, hd), 1.0),
        'cache_mla_ckv': nrm((DEC_BATCH, DEPTH, PAST_LEN, MLA_KV_RANK), 1.0),
        'cache_mla_krope': nrm((DEC_BATCH, DEPTH, PAST_LEN, MLA_ROPE), 1.0),
        'c': nrm((DEC_BATCH, D_MODEL), 1.0),
        'c_ctx': nrm((D_MODEL,), 1.0),
        'w_mod': nrm((DEPTH, D_MODEL, 6 * D_MODEL), 0.5 * D_MODEL ** -0.5),
        'b_mod': nrm((DEPTH, 6 * D_MODEL), 0.02),
        'norm1_g': 1.0 + nrm((DEPTH, D_MODEL), 0.02),
        'norm2_g': 1.0 + nrm((DEPTH, D_MODEL), 0.02),
        'w_in': nrm((DEPTH, D_MODEL, IN_WIDTH), D_MODEL ** -0.5),
        'b_in': b_in,
        'na_rpb': nrm((DEPTH, NA_HEADS, 2 * NA_KR_MAX - 1, 2 * NA_KC - 1), 0.1),
        'sw_sink': nrm((DEPTH, SW_HEADS), 0.5),
        'mla_q_norm': 1.0 + nrm((DEPTH, MLA_Q_RANK), 0.02),
        'w_uq': nrm((DEPTH, MLA_Q_RANK, MLA_HEADS * (MLA_NOPE + MLA_ROPE)), MLA_Q_RANK ** -0.5),
        'mla_kv_norm': 1.0 + nrm((DEPTH, MLA_KV_RANK), 0.02),
        'w_uk': nrm((DEPTH, MLA_KV_RANK, MLA_HEADS * MLA_NOPE), MLA_KV_RANK ** -0.5),
        'w_uv': nrm((DEPTH, MLA_KV_RANK, MLA_HEADS * MLA_V), MLA_KV_RANK ** -0.5),
        'w_branch': nrm((DEPTH, N_BRANCH, BRANCH_W, D_MODEL), BRANCH_W ** -0.5),
        'w_out': nrm((DEPTH, D_MODEL, D_MODEL), D_MODEL ** -0.5),
        'peer_wq': nrm((DEPTH, D_MODEL, PEER_HEADS * 2 * PEER_KEY_DIM), D_MODEL ** -0.5),
        'peer_keys': nrm((DEPTH, PEER_HEADS, 2, PEER_N_KEYS, PEER_KEY_DIM), PEER_KEY_DIM ** -0.5),
        'peer_u': nrm((DEPTH, PEER_N_EXPERTS, D_MODEL), D_MODEL ** -0.5),
        'peer_v': nrm((DEPTH, PEER_N_EXPERTS, D_MODEL), 0.1),
        'final_norm_g': 1.0 + nrm((D_MODEL,), 0.02),
    }


def reference(x_prompt, x_sample, cache_na_k, cache_na_v, state_mlstm_C, state_mlstm_n, state_mlstm_m,
              cache_swa_k, cache_swa_v, cache_mla_ckv, cache_mla_krope, c, c_ctx,
              w_mod, b_mod, norm1_g, norm2_g, w_in, b_in, na_rpb, sw_sink, mla_q_norm, w_uq,
              mla_kv_norm, w_uk, w_uv, w_branch, w_out, peer_wq, peer_keys, peer_u, peer_v, final_norm_g):
    xp, xs = x_prompt, x_sample
    per_layer = tuple([] for _ in range(9))
    for l in range(DEPTH):
        p = {'w_in': w_in[l], 'b_in': b_in[l], 'na_rpb': na_rpb[l], 'sw_sink': sw_sink[l],
             'mla_q_norm': mla_q_norm[l], 'w_uq': w_uq[l], 'mla_kv_norm': mla_kv_norm[l],
             'w_uk': w_uk[l], 'w_uv': w_uv[l], 'w_branch': w_branch[l], 'w_out': w_out[l]}
        sh1, sc1, g1, sh2, sc2, g2 = adaln(c_ctx[None, None, :], w_mod[l], b_mod[l])
        mix, ctx_tensors = context_mixers(modulate(xp, norm1_g[l], sh1, sc1), p)
        xp = xp + g1 * mix
        xp = xp + g2 * peer_ffn(modulate(xp, norm2_g[l], sh2, sc2), peer_wq[l], peer_keys[l], peer_u[l], peer_v[l])
        for i, t in enumerate(ctx_tensors):
            per_layer[i].append(t)
        sh1, sc1, g1, sh2, sc2, g2 = adaln(c[:, None, :], w_mod[l], b_mod[l])
        cache_l = (cache_na_k[:, l], cache_na_v[:, l], state_mlstm_C[:, l], state_mlstm_n[:, l], state_mlstm_m[:, l],
                   cache_swa_k[:, l], cache_swa_v[:, l], cache_mla_ckv[:, l], cache_mla_krope[:, l])
        xs = xs + g1 * latent_mixers(modulate(xs, norm1_g[l], sh1, sc1), p, cache_l)
        xs = xs + g2 * peer_ffn(modulate(xs, norm2_g[l], sh2, sc2), peer_wq[l], peer_keys[l], peer_u[l], peer_v[l])
    y_prompt = rmsnorm(xp, final_norm_g)
    y_sample = rmsnorm(xs, final_norm_g)
    (new_na_k, new_na_v, new_mlstm_C, new_mlstm_n, new_mlstm_m,
     new_swa_k, new_swa_v, new_mla_ckv, new_mla_krope) = [jnp.stack(s, axis=1) for s in per_layer]
    return (y_prompt, y_sample, new_na_k, new_na_v, new_mlstm_C, new_mlstm_n, new_mlstm_m,
            new_swa_k, new_swa_v, new_mla_ckv, new_mla_krope)
```

```python
import functools

import numpy as np
import jax
import jax.numpy as jnp
from jax import lax
from jax.experimental import pallas as pl
from jax.experimental.pallas import tpu as pltpu

F32 = jnp.float32
BF16 = jnp.bfloat16

D_MODEL = 1024
DEPTH = 4
GRID_W = 64
HEAD_DIM = 64
N_BRANCH = 4
BRANCH_W = 256
NA_HEADS = 4
NA_KR_MAX = 8
NA_KC = 16
ML_HEADS = 4
ML_CHUNK = 128
SW_HEADS = 4
SW_KV_HEADS = 2
SW_GROUP = SW_HEADS // SW_KV_HEADS
SW_WINDOW = 128
MLA_HEADS = 4
MLA_Q_RANK = 192
MLA_KV_RANK = 128
MLA_NOPE = 64
MLA_ROPE = 32
MLA_V = 64
MLA_SCALE = (MLA_NOPE + MLA_ROPE) ** -0.5
PEER_HEADS = 8
PEER_KEY_DIM = 64
PEER_N_KEYS = 128
PEER_TOPK = 16
ROPE_BASE = 10000.0
RMS_EPS = 1e-6
NEG = -1e30
IN_SPLITS = (256, 256, 256, 256, 256, 256, 256, 8, 8, 256, 128, 128, 192, 128, 32, 4096)

LANE = 128

SEGS_AB = (("na_q", 256), ("na_k", 256), ("na_v", 256), ("ml_q", 256), ("ml_k", 256), ("ml_v", 256),
           ("ml_o", 256), ("ml_if", 128), ("sw_q", 256), ("sw_k", 128), ("sw_v", 128), ("cq", 256),
           ("ckv", 128), ("krope", 128))


def _cp(sem, vmem_mb=None):
    kw = dict(dimension_semantics=sem)
    if vmem_mb is not None:
        kw["vmem_limit_bytes"] = vmem_mb * 1024 * 1024
    return pltpu.CompilerParams(**kw)


def _dot(a, b):
    return jnp.dot(a, b, preferred_element_type=F32)


def _dot_nt(a, b):
    return lax.dot_general(a, b, (((1,), (1,)), ((), ())), preferred_element_type=F32)


def _rms(x, inv_n):
    return x * lax.rsqrt(jnp.sum(x * x, axis=-1, keepdims=True) * inv_n + RMS_EPS)


def _sigmoid(x):
    return 1.0 / (1.0 + jnp.exp(-x))


def _gelu(x):
    return 0.5 * x * (1.0 + jnp.tanh(0.7978845608028654 * (x + 0.044715 * (x * x * x))))


def _mod_kernel(c_ref, w_ref, b_ref, o_ref):
    c = c_ref[...]
    s = c * _sigmoid(c)
    o_ref[...] = _dot(s.astype(BF16), w_ref[...].astype(BF16)) + b_ref[...]


def _modulation(cond, w_mod, b_mod):
    nl, d, n = w_mod.shape
    r = cond.shape[0]
    tn = 1024
    return pl.pallas_call(
        _mod_kernel,
        grid=(nl, n // tn),
        in_specs=[pl.BlockSpec((r, d), lambda l, j: (0, 0)),
                  pl.BlockSpec((None, d, tn), lambda l, j: (l, 0, j)),
                  pl.BlockSpec((None, 1, tn), lambda l, j: (l, 0, j))],
        out_specs=pl.BlockSpec((None, r, tn), lambda l, j: (l, 0, j)),
        out_shape=jax.ShapeDtypeStruct((nl, r, n), F32),
        compiler_params=_cp(("parallel", "parallel")),
        name="adaln_modulation",
    )(cond, w_mod, b_mod.reshape(nl, 1, n))


def _linear_kernel(*refs, segs, has_norm, has_mod, has_bias, emit_normed, inv_k):
    it = iter(refs)
    x_ref = next(it)
    g_ref = next(it) if has_norm else None
    sc_ref = next(it) if has_mod else None
    sh_ref = next(it) if has_mod else None
    w_ref = next(it)
    b_ref = next(it) if has_bias else None
    outs = list(it)
    x = x_ref[...]
    if has_norm:
        x = _rms(x, inv_k) * g_ref[...]
    if has_mod:
        x = x * (1.0 + sc_ref[0]) + sh_ref[0]
    if emit_normed:
        outs[0][...] = x
        outs = outs[1:]
    xb = x.astype(BF16)
    for (s, wd), o in zip(segs, outs):
        y = _dot(xb, w_ref[:, s:s + wd])
        if has_bias:
            y = y + b_ref[:, s:s + wd]
        o[...] = y


def _linear(x, w, widths, *, bias=None, norm_g=None, k_valid=None, mod=None, rows_per_mod=None,
            emit_normed=False, tm=256, vmem_mb=None, name="linear"):
    n, k = x.shape
    tm = min(tm, n)
    segs, s = [], 0
    for wd in widths:
        segs.append((s, wd))
        s += wd
    assert s == w.shape[1] and n % tm == 0
    args = [x]
    in_specs = [pl.BlockSpec((tm, k), lambda i: (i, 0))]
    if norm_g is not None:
        args.append(norm_g.reshape(1, k))
        in_specs.append(pl.BlockSpec((1, k), lambda i: (0, 0)))
    if mod is not None:
        assert rows_per_mod % tm == 0
        for m in mod:
            args.append(m)
            in_specs.append(pl.BlockSpec((1, 1, k), lambda i: ((i * tm) // rows_per_mod, 0, 0)))
    args.append(w)
    in_specs.append(pl.BlockSpec(w.shape, lambda i: (0, 0)))
    if bias is not None:
        args.append(bias.reshape(1, -1))
        in_specs.append(pl.BlockSpec((1, w.shape[1]), lambda i: (0, 0)))
    out_shapes, out_specs = [], []
    if emit_normed:
        out_shapes.append(jax.ShapeDtypeStruct((n, k), F32))
        out_specs.append(pl.BlockSpec((tm, k), lambda i: (i, 0)))
    for wd in widths:
        out_shapes.append(jax.ShapeDtypeStruct((n, wd), F32))
        out_specs.append(pl.BlockSpec((tm, wd), lambda i: (i, 0)))
    kern = functools.partial(_linear_kernel, segs=tuple(segs), has_norm=norm_g is not None,
                             has_mod=mod is not None, has_bias=bias is not None, emit_normed=emit_normed,
                             inv_k=1.0 / (k_valid or k))
    return pl.pallas_call(kern, grid=(n // tm,), in_specs=in_specs, out_specs=out_specs,
                          out_shape=out_shapes, compiler_params=_cp(("parallel",), vmem_mb), name=name)(*args)


def _merge_kernel(ona, hf, hb, mlo, osw, omla, gate, wb, wo, x, g1, bd, o):
    h = hf[...] + hb[...]
    hsq = h * h
    hi = hsq.astype(BF16)
    lo = (hsq - hi.astype(F32)).astype(BF16)
    ms = (_dot(hi, bd[...]) + _dot(lo, bd[...])) * (1.0 / HEAD_DIM)
    oml = h * lax.rsqrt(ms + RMS_EPS) * _sigmoid(mlo[...])
    branches = (ona[...], oml, osw[...], omla[...])
    acc = None
    for n in range(N_BRANCH):
        y = _sigmoid(gate[:, n * D_MODEL:(n + 1) * D_MODEL]) * _dot(branches[n].astype(BF16), wb[n])
        acc = y if acc is None else acc + y
    o[...] = x[...] + g1[0] * _dot(acc.astype(BF16), wo[...])


def _merge(ona, hf, hb, mlo, osw, omla, gate, wb, wo, x, g1, rows_per_mod, tm=256):
    n = x.shape[0]
    tm = min(tm, n)
    bd = jnp.asarray(np.kron(np.eye(BRANCH_W // HEAD_DIM), np.ones((HEAD_DIM, HEAD_DIM))), BF16)
    row = lambda wd: pl.BlockSpec((tm, wd), lambda i: (i, 0))
    return pl.pallas_call(
        _merge_kernel,
        grid=(n // tm,),
        in_specs=[row(BRANCH_W)] * 6 + [row(N_BRANCH * D_MODEL),
                  pl.BlockSpec(wb.shape, lambda i: (0, 0, 0)),
                  pl.BlockSpec(wo.shape, lambda i: (0, 0)),
                  row(D_MODEL),
                  pl.BlockSpec((1, 1, D_MODEL), lambda i: ((i * tm) // rows_per_mod, 0, 0)),
                  pl.BlockSpec(bd.shape, lambda i: (0, 0))],
        out_specs=row(D_MODEL),
        out_shape=jax.ShapeDtypeStruct((n, D_MODEL), F32),
        compiler_params=_cp(("parallel",), 48),
        name="branch_merge",
    )(ona, hf, hb, mlo, osw, omla, gate, wb, wo, x, g1, bd)


def _peer_kernel(x_ref, ng_ref, sc_ref, sh_ref, g2_ref, wq_ref, keys_ref, u_ref, vt_ref, o_ref,
                 h2t_ref, acc_ref, t1_ref, e1_ref, s2_ref, e2_ref, *, ec, n_chunks):
    c = pl.program_id(1)
    n_i1 = ec // PEER_N_KEYS

    @pl.when(c == 0)
    def _():
        x = x_ref[...]
        h2 = _rms(x, 1.0 / D_MODEL) * ng_ref[...] * (1.0 + sc_ref[0]) + sh_ref[0]
        h2t_ref[...] = h2.T.astype(BF16)
        q = _dot(h2.astype(BF16), wq_ref[...])
        qt = q.T.astype(BF16)
        rows8 = lax.broadcasted_iota(jnp.int32, (8, 1), 0)
        for h in range(PEER_HEADS):
            scores, tops = [], []
            for sd in range(2):
                r0 = (h * 2 + sd) * PEER_KEY_DIM
                s = _dot(keys_ref[h, sd], qt[r0:r0 + PEER_KEY_DIM, :])
                scores.append(s)
                cur, rows = s, []
                for _ in range(PEER_TOPK):
                    mx = jnp.max(cur, axis=0, keepdims=True)
                    rows.append(mx)
                    cur = jnp.where(cur == mx, NEG, cur)
                tops.append(rows)
            a1, a2 = tops
            a1_16 = jnp.concatenate(a1, axis=0)
            a1_8 = a1_16[:8]
            cands = [a1_16 + a2[0], a1_8 + a2[1]]
            for k2 in range(2, 8):
                cands.append(jnp.where(rows8 < PEER_TOPK // (k2 + 1), a1_8 + a2[k2], NEG))
            cands.append(a1[0] + jnp.concatenate(a2[8:], axis=0))
            cand = jnp.concatenate(cands, axis=0)
            cur, top = cand, []
            for _ in range(PEER_TOPK + 1):
                mx = jnp.max(cur, axis=0, keepdims=True)
                top.append(mx)
                cur = jnp.where(cur == mx, NEG, cur)
            tau = 0.5 * (top[PEER_TOPK - 1] + top[PEER_TOPK])
            z = jnp.sum(jnp.where(cand >= tau, jnp.exp(cand - top[0]), 0.0), axis=0, keepdims=True)
            t1_ref[h] = tau - scores[0]
            e1_ref[h] = jnp.exp(scores[0] - a1[0])
            s2_ref[h] = scores[1]
            e2_ref[h] = jnp.exp(scores[1] - a2[0]) / z
        acc_ref[...] = jnp.zeros_like(acc_ref)

    a = _gelu(_dot(u_ref[...], h2t_ref[...]))
    zs = []
    for j in range(n_i1):
        i1 = c * n_i1 + j
        w = None
        for h in range(PEER_HEADS):
            th = t1_ref[h, pl.ds(i1, 1), :]
            e1r = e1_ref[h, pl.ds(i1, 1), :]
            term = jnp.where(s2_ref[h] >= th, e2_ref[h], 0.0) * e1r
            w = term if w is None else w + term
        zs.append((w * a[j * PEER_N_KEYS:(j + 1) * PEER_N_KEYS, :]).astype(BF16))
    acc_ref[...] += _dot(vt_ref[...], jnp.concatenate(zs, axis=0))

    @pl.when(c == n_chunks - 1)
    def _():
        o_ref[...] = x_ref[...] + g2_ref[0] * acc_ref[...].T


def _peer(x, ng, sc, sh, g2, wq, keys, u, vt, rows_per_mod, tm=512, ec=512):
    n = x.shape[0]
    tm = min(tm, n)
    n_exp = u.shape[0]
    n_chunks = n_exp // ec
    modspec = pl.BlockSpec((1, 1, D_MODEL), lambda i, c: ((i * tm) // rows_per_mod, 0, 0))
    kern = functools.partial(_peer_kernel, ec=ec, n_chunks=n_chunks)
    head_buf = pltpu.VMEM((PEER_HEADS, PEER_N_KEYS, tm), F32)
    return pl.pallas_call(
        kern,
        grid=(n // tm, n_chunks),
        in_specs=[pl.BlockSpec((tm, D_MODEL), lambda i, c: (i, 0)),
                  pl.BlockSpec((1, D_MODEL), lambda i, c: (0, 0)),
                  modspec, modspec, modspec,
                  pl.BlockSpec(wq.shape, lambda i, c: (0, 0)),
                  pl.BlockSpec(keys.shape, lambda i, c: (0, 0, 0, 0)),
                  pl.BlockSpec((ec, D_MODEL), lambda i, c: (c, 0)),
                  pl.BlockSpec((D_MODEL, ec), lambda i, c: (0, c))],
        out_specs=pl.BlockSpec((tm, D_MODEL), lambda i, c: (i, 0)),
        out_shape=jax.ShapeDtypeStruct((n, D_MODEL), F32),
        scratch_shapes=[pltpu.VMEM((D_MODEL, tm), BF16), pltpu.VMEM((D_MODEL, tm), F32),
                        head_buf, head_buf, head_buf, head_buf],
        compiler_params=_cp(("parallel", "arbitrary"), 56),
        name="peer_ffn",
    )(x, ng.reshape(1, D_MODEL), sc, sh, g2, wq, keys, u, vt)


def _final_norm_kernel(x_ref, g_ref, o_ref):
    o_ref[...] = _rms(x_ref[...], 1.0 / D_MODEL) * g_ref[...]


def _final_norm(x, g, tm=512):
    n = x.shape[0]
    tm = min(tm, n)
    return pl.pallas_call(
        _final_norm_kernel, grid=(n // tm,),
        in_specs=[pl.BlockSpec((tm, D_MODEL), lambda i: (i, 0)), pl.BlockSpec((1, D_MODEL), lambda i: (0, 0))],
        out_specs=pl.BlockSpec((tm, D_MODEL), lambda i: (i, 0)),
        out_shape=jax.ShapeDtypeStruct((n, D_MODEL), F32),
        compiler_params=_cp(("parallel",)), name="final_norm")(x, g.reshape(1, D_MODEL))


def _rope_tables(t, n_heads, dh):
    pos = jnp.arange(t)
    sec, half = dh // 2, dh // 4
    freqs = ROPE_BASE ** (-jnp.arange(half, dtype=F32) / half)
    p = np.arange(dh)
    first = jnp.asarray((p % sec) < half)
    ang_r = (pos // GRID_W).astype(F32)[:, None] * freqs[None, :]
    ang_c = (pos % GRID_W).astype(F32)[:, None] * freqs[None, :]
    ang = jnp.concatenate([ang_r, ang_r, ang_c, ang_c], axis=1)
    cos, sin = jnp.cos(ang), jnp.sin(ang)
    sa = jnp.where(first[None, :], -sin, 0.0)
    sb = jnp.where(first[None, :], 0.0, sin)
    tile = lambda a: jnp.tile(a, (1, n_heads))
    return tile(cos), tile(sa), tile(sb)


def _rope(x, cos, sa, sb, half):
    w = x.shape[-1]
    return x * cos + pltpu.roll(x, w - half, 1) * sa + pltpu.roll(x, half, 1) * sb


def _softmax_pv(blocks, scale, sink=None):
    m = None
    for s, _ in blocks:
        bm = jnp.max(s, axis=1, keepdims=True)
        m = bm if m is None else jnp.maximum(m, bm)
    if sink is not None:
        m = jnp.maximum(m, sink)
    l, acc = None, None
    for s, v in blocks:
        p = jnp.exp(s - m)
        bl = jnp.sum(p, axis=1, keepdims=True)
        pv = _dot(p.astype(BF16), v)
        l = bl if l is None else l + bl
        acc = pv if acc is None else acc + pv
    if sink is not None:
        l = l + jnp.exp(sink - m)
    return acc / l


def _ctx_attn_kernel(naq, nak, nav, swq, swk, swv, qm, kn, kr, vm, sink_ref, o_na, o_sw, o_mla):
    hd = HEAD_DIM
    scale = hd ** -0.5
    q, k, v = naq[...].astype(BF16), nak[...].astype(BF16), nav[...].astype(BF16)
    for h in range(NA_HEADS):
        hs = slice(h * hd, (h + 1) * hd)
        o_na[:, hs] = _softmax_pv([(_dot_nt(q[:, hs], k[:, hs]) * scale, v[:, hs])], scale)
    q, k, v = swq[...].astype(BF16), swk[...].astype(BF16), swv[...].astype(BF16)
    for h in range(SW_HEADS):
        g = h // SW_GROUP
        hs, gs = slice(h * hd, (h + 1) * hd), slice(g * hd, (g + 1) * hd)
        o_sw[:, hs] = _softmax_pv([(_dot_nt(q[:, hs], k[:, gs]) * scale, v[:, gs])], scale, sink=sink_ref[h])
    q, k, v = qm[...].astype(BF16), kn[...].astype(BF16), vm[...].astype(BF16)
    krb = kr[:, :MLA_ROPE].astype(BF16)
    nope_w = MLA_HEADS * MLA_NOPE
    for h in range(MLA_HEADS):
        ns = slice(h * MLA_NOPE, (h + 1) * MLA_NOPE)
        rs = slice(nope_w + h * MLA_ROPE, nope_w + (h + 1) * MLA_ROPE)
        s = (_dot_nt(q[:, ns], k[:, ns]) + _dot_nt(q[:, rs], krb)) * MLA_SCALE
        o_mla[:, h * MLA_V:(h + 1) * MLA_V] = _softmax_pv([(s, v[:, h * MLA_V:(h + 1) * MLA_V])], MLA_SCALE)


def _ctx_attention(p, qm, kn, vm, sink, nb, t):
    row = lambda wd: pl.BlockSpec((t, wd), lambda b: (b, 0))
    outs = pl.pallas_call(
        _ctx_attn_kernel, grid=(nb,),
        in_specs=[row(256), row(256), row(256), row(256), row(128), row(128), row(384), row(256), row(128),
                  row(256), pl.BlockSpec(memory_space=pltpu.SMEM)],
        out_specs=[row(256)] * 3,
        out_shape=[jax.ShapeDtypeStruct((nb * t, BRANCH_W), F32)] * 3,
        compiler_params=_cp(("parallel",)), name="ctx_attention",
    )(p["na_q"], p["na_k"], p["na_v"], p["sw_q"], p["sw_k"], p["sw_v"], qm, kn, p["krope"], vm, sink)
    return outs


NA_TQ = 256


def _na_bias_tables(rpb, rows_n):
    tabs = []
    for j in (0, 1, rows_n // 4 - 1):
        a = np.arange(4)[:, None, None, None]
        c = np.arange(GRID_W)[None, :, None, None]
        e = np.arange(12)[None, None, :, None]
        w = np.arange(GRID_W)[None, None, None, :]
        r = 4 * j + a
        krow = 4 * (j - 1) + e
        r0 = np.clip(r - NA_KR_MAX // 2, 0, rows_n - NA_KR_MAX)
        ok_r = (krow >= r0) & (krow < r0 + NA_KR_MAX)
        cs = np.clip(c - NA_KC // 2, 0, GRID_W - NA_KC)
        ok_c = (w >= cs) & (w < cs + NA_KC)
        ok = np.broadcast_to(ok_r & ok_c, (4, GRID_W, 12, GRID_W))
        dr = np.broadcast_to(np.clip(krow - r + NA_KR_MAX - 1, 0, 2 * NA_KR_MAX - 2), ok.shape)
        dc = np.broadcast_to(np.clip(w - c + NA_KC - 1, 0, 2 * NA_KC - 2), ok.shape)
        b = rpb[:, dr.reshape(-1), dc.reshape(-1)].reshape((NA_HEADS,) + ok.shape)
        b = jnp.where(jnp.asarray(ok)[None], b, NEG)
        tabs.append(b.reshape(NA_HEADS, 4 * GRID_W, 12 * GRID_W))
    return jnp.stack(tabs)


def _na_kernel(q, k0, k1, k2, v0, v1, v2, kc, vc, bias, o):
    hd = HEAD_DIM
    scale = hd ** -0.5
    qb = q[...].astype(BF16)
    kk = jnp.concatenate([k0[...], k1[...], k2[...]], axis=0).astype(BF16)
    vv = jnp.concatenate([v0[...], v1[...], v2[...]], axis=0).astype(BF16)
    kcb, vcb = kc[...].astype(BF16), vc[...].astype(BF16)
    for h in range(NA_HEADS):
        hs = slice(h * hd, (h + 1) * hd)
        s_nb = _dot_nt(qb[:, hs], kk[:, hs]) * scale + bias[h]
        s_cx = _dot_nt(qb[:, hs], kcb[:, hs]) * scale
        o[:, hs] = _softmax_pv([(s_nb, vv[:, hs]), (s_cx, vcb[:, hs])], scale)


def _na_attention(q, k, v, kc, vc, bias, nb, t):
    nj = t // NA_TQ
    lc = kc.shape[1]
    blk = lambda f: pl.BlockSpec((NA_TQ, BRANCH_W), lambda b, j: (b * nj + f(j), 0))
    prev = lambda j: jnp.maximum(j - 1, 0)
    nxt = lambda j: jnp.minimum(j + 1, nj - 1)
    cur = lambda j: j
    ctx = pl.BlockSpec((None, lc, BRANCH_W), lambda b, j: (b, 0, 0))
    tsel = lambda b, j: (jnp.where(j == 0, 0, jnp.where(j == nj - 1, 2, 1)), 0, 0, 0)
    return pl.pallas_call(
        _na_kernel, grid=(nb, nj),
        in_specs=[blk(cur), blk(prev), blk(cur), blk(nxt), blk(prev), blk(cur), blk(nxt), ctx, ctx,
                  pl.BlockSpec((None,) + bias.shape[1:], tsel)],
        out_specs=blk(cur),
        out_shape=jax.ShapeDtypeStruct((nb * t, BRANCH_W), F32),
        compiler_params=_cp(("parallel", "arbitrary"), 48), name="na_attention",
    )(q, k, k, k, v, v, v, kc, vc, bias)


SW_TQ = 256


def _sw_kernel(q, k0, k1, k2, v0, v1, v2, kc, vc, cq, saq, sbq, ck0, sak0, sbk0, ck1, sak1, sbk1,
               ck2, sak2, sbk2, sink_ref, o, *, t):
    hd = HEAD_DIM
    scale = hd ** -0.5
    half = hd // 4
    j = pl.program_id(1)
    qr = _rope(q[...], cq[...], saq[...], sbq[...], half).astype(BF16)
    kk = jnp.concatenate([_rope(k0[...], ck0[...], sak0[...], sbk0[...], half),
                          _rope(k1[...], ck1[...], sak1[...], sbk1[...], half),
                          _rope(k2[...], ck2[...], sak2[...], sbk2[...], half)], axis=0).astype(BF16)
    vv = jnp.concatenate([v0[...], v1[...], v2[...]], axis=0).astype(BF16)
    kcb, vcb = kc[...].astype(BF16), vc[...].astype(BF16)
    qpos = j * SW_TQ + lax.broadcasted_iota(jnp.int32, (SW_TQ, 3 * SW_TQ), 0)
    kpos = (j - 1) * SW_TQ + lax.broadcasted_iota(jnp.int32, (SW_TQ, 3 * SW_TQ), 1)
    valid = (jnp.abs(qpos - kpos) <= SW_WINDOW) & (kpos >= 0) & (kpos < t)
    for h in range(SW_HEADS):
        g = h // SW_GROUP
        hs, gs = slice(h * hd, (h + 1) * hd), slice(g * hd, (g + 1) * hd)
        s_loc = jnp.where(valid, _dot_nt(qr[:, hs], kk[:, gs]) * scale, NEG)
        s_cx = _dot_nt(qr[:, hs], kcb[:, gs]) * scale
        o[:, hs] = _softmax_pv([(s_loc, vv[:, gs]), (s_cx, vcb[:, gs])], scale, sink=sink_ref[h])


def _sw_attention(q, k, v, kc, vc, sink, tabs_q, tabs_k, nb, t):
    nj = t // SW_TQ
    lc = kc.shape[1]
    kvw = SW_KV_HEADS * HEAD_DIM
    prev = lambda j: jnp.maximum(j - 1, 0)
    nxt = lambda j: jnp.minimum(j + 1, nj - 1)
    cur = lambda j: j
    qblk = pl.BlockSpec((SW_TQ, BRANCH_W), lambda b, j: (b * nj + j, 0))
    kblk = lambda f: pl.BlockSpec((SW_TQ, kvw), lambda b, j: (b * nj + f(j), 0))
    ctx = pl.BlockSpec((None, lc, kvw), lambda b, j: (b, 0, 0))
    tq = pl.BlockSpec((SW_TQ, BRANCH_W), lambda b, j: (j, 0))
    tk = lambda f: pl.BlockSpec((SW_TQ, kvw), lambda b, j: (f(j), 0))
    in_specs = [qblk, kblk(prev), kblk(cur), kblk(nxt), kblk(prev), kblk(cur), kblk(nxt), ctx, ctx, tq, tq, tq]
    args = [q, k, k, k, v, v, v, kc, vc, *tabs_q]
    for f in (prev, cur, nxt):
        in_specs += [tk(f)] * 3
        args += list(tabs_k)
    in_specs.append(pl.BlockSpec(memory_space=pltpu.SMEM))
    args.append(sink)
    return pl.pallas_call(
        functools.partial(_sw_kernel, t=t), grid=(nb, nj), in_specs=in_specs, out_specs=qblk,
        out_shape=jax.ShapeDtypeStruct((nb * t, BRANCH_W), F32),
        compiler_params=_cp(("parallel", "arbitrary"), 48), name="sw_attention",
    )(*args)


MLA_TQ = 512
MLA_TK = 512


def _mla_kernel(qm, kn, kr, v, knc, krc, vc, cq, saq, sbq, ck, sak, sbk, o, qr_ref, m_ref, l_ref, acc_ref, *, n_lat):
    kj = pl.program_id(2)
    half = MLA_ROPE // 4
    nope_w = MLA_HEADS * MLA_NOPE

    @pl.when(kj == 0)
    def _():
        qr_ref[...] = _rope(qm[:, nope_w:], cq[...], saq[...], sbq[...], half).astype(BF16)
        m_ref[...] = jnp.full(m_ref.shape, NEG, F32)
        l_ref[...] = jnp.zeros(l_ref.shape, F32)
        acc_ref[...] = jnp.zeros(acc_ref.shape, F32)

    def step(kn_b, kr_b, v_b):
        qn = qm[:, :nope_w].astype(BF16)
        for h in range(MLA_HEADS):
            ns = slice(h * MLA_NOPE, (h + 1) * MLA_NOPE)
            vs = slice(h * MLA_V, (h + 1) * MLA_V)
            s = (_dot_nt(qn[:, ns], kn_b[:, ns])
                 + _dot_nt(qr_ref[:, h * MLA_ROPE:(h + 1) * MLA_ROPE], kr_b)) * MLA_SCALE
            m_old = m_ref[h]
            m_new = jnp.maximum(m_old, jnp.max(s, axis=1, keepdims=True))
            alpha = jnp.exp(m_old - m_new)
            p = jnp.exp(s - m_new)
            l_ref[h] = alpha * l_ref[h] + jnp.sum(p, axis=1, keepdims=True)
            acc_ref[:, vs] = alpha * acc_ref[:, vs] + _dot(p.astype(BF16), v_b[:, vs])
            m_ref[h] = m_new

    @pl.when(kj < n_lat)
    def _():
        krr = _rope(kr[...], ck[...], sak[...], sbk[...], half)[:, :MLA_ROPE].astype(BF16)
        step(kn[...].astype(BF16), krr, v[...].astype(BF16))

    @pl.when(kj == n_lat)
    def _():
        step(knc[...].astype(BF16), krc[...].astype(BF16), vc[...].astype(BF16))
        for h in range(MLA_HEADS):
            vs = slice(h * MLA_V, (h + 1) * MLA_V)
            o[:, vs] = acc_ref[:, vs] / l_ref[h]


def _mla_attention(qm, kn, kr, v, knc, krc, vc, tabs, nb, t):
    nq, n_lat = t // MLA_TQ, t // MLA_TK
    lc = krc.shape[1]
    assert lc == MLA_TK
    kidx = lambda b, i, kj: (b * n_lat + jnp.minimum(kj, n_lat - 1), 0)
    kblk = lambda wd: pl.BlockSpec((MLA_TK, wd), kidx)
    tkb = pl.BlockSpec((MLA_TK, LANE), lambda b, i, kj: (jnp.minimum(kj, n_lat - 1), 0))
    tqb = pl.BlockSpec((MLA_TQ, LANE), lambda b, i, kj: (i, 0))
    cblk = lambda wd: pl.BlockSpec((lc, wd), lambda b, i, kj: (b, 0))
    return pl.pallas_call(
        functools.partial(_mla_kernel, n_lat=n_lat), grid=(nb, nq, n_lat + 1),
        in_specs=[pl.BlockSpec((MLA_TQ, 384), lambda b, i, kj: (b * nq + i, 0)),
                  kblk(256), kblk(LANE), kblk(256), cblk(256),
                  pl.BlockSpec((None, lc, MLA_ROPE), lambda b, i, kj: (b, 0, 0)), cblk(256),
                  tqb, tqb, tqb, tkb, tkb, tkb],
        out_specs=pl.BlockSpec((MLA_TQ, BRANCH_W), lambda b, i, kj: (b * nq + i, 0)),
        out_shape=jax.ShapeDtypeStruct((nb * t, BRANCH_W), F32),
        scratch_shapes=[pltpu.VMEM((MLA_TQ, LANE), BF16), pltpu.VMEM((MLA_HEADS, MLA_TQ, 1), F32),
                        pltpu.VMEM((MLA_HEADS, MLA_TQ, 1), F32), pltpu.VMEM((MLA_TQ, BRANCH_W), F32)],
        compiler_params=_cp(("parallel", "parallel", "arbitrary"), 48), name="mla_attention",
    )(qm, kn, kr, v, knc, krc, vc, *tabs, *tabs)


def _log_sigmoid(x):
    return jnp.minimum(x, 0.0) - jnp.log(1.0 + jnp.exp(-jnp.abs(x)))


def _mlstm_dir(d, q_ref, k_ref, v_ref, if_ref, o_ref, c_s, n_s, m_s):
    ln = ML_CHUNK
    hd = HEAD_DIM
    scale = hd ** -0.5
    t_in = if_ref[...]
    logf = _log_sigmoid(t_in)
    ri = lax.broadcasted_iota(jnp.int32, (ln, ln), 0)
    ci = lax.broadcasted_iota(jnp.int32, (ln, ln), 1)
    mask = (ci <= ri) if d == 0 else (ci >= ri)
    tri = jnp.where(mask, 1.0, 0.0).astype(BF16)
    hi = logf.astype(BF16)
    r1 = logf - hi.astype(F32)
    mid = r1.astype(BF16)
    lo = (r1 - mid.astype(F32)).astype(BF16)
    bcol = _dot(tri, hi) + _dot(tri, mid) + _dot(tri, lo)
    brow = bcol.T
    trow = t_in.T
    q, k, v = q_ref[...], k_ref[...], v_ref[...]
    for h in range(ML_HEADS):
        ch_i = d * ML_HEADS + h
        ch_f = 2 * ML_HEADS + ch_i
        hs = slice(h * hd, (h + 1) * hd)
        bc = bcol[:, ch_f:ch_f + 1]
        br = brow[ch_f:ch_f + 1, :]
        ir = trow[ch_i:ch_i + 1, :]
        m_prev = m_s[ch_i]
        dm = jnp.where(mask, bc - br + ir, NEG)
        inter = bc + m_prev
        m_t = jnp.maximum(inter, jnp.max(dm, axis=1, keepdims=True))
        w_intra = jnp.exp(dm - m_t)
        w_inter = jnp.exp(inter - m_t)
        qh, kh = q[:, hs], k[:, hs] * scale
        qb, kb, vb = qh.astype(BF16), kh.astype(BF16), v[:, hs].astype(BF16)
        s = _dot_nt(qb, kb) * w_intra
        c_prev = c_s[ch_i]
        n_prev = n_s[ch_i]
        num = _dot(s.astype(BF16), vb) + w_inter * _dot(qb, c_prev.astype(BF16))
        den = jnp.sum(s, axis=1, keepdims=True) + w_inter * jnp.sum(qh * n_prev, axis=1, keepdims=True)
        o_ref[:, hs] = num / jnp.maximum(jnp.abs(den), jnp.exp(-m_t))
        b_last = bc[ln - 1:ln, :] if d == 0 else bc[0:1, :]
        g = b_last - br + ir
        m_new = jnp.maximum(b_last + m_prev, jnp.max(g, axis=1, keepdims=True))
        w = jnp.exp(g - m_new)
        decay = jnp.exp(b_last + m_prev - m_new)
        c_s[ch_i] = decay * c_prev + _dot((kh.T * w).astype(BF16), vb)
        n_s[ch_i] = decay * n_prev + _dot(jnp.broadcast_to(w, (8, ln)).astype(BF16), kb)[0:1]
        m_s[ch_i] = m_new


def _mlstm_kernel(qf, kf, vf, iff, qb, kb, vb, ifb, c0, n0, m0, hf, hb, c_out, n_out, m_out, c_s, n_s, m_s, *, nc):
    c = pl.program_id(1)

    @pl.when(c == 0)
    def _():
        c_s[...] = c0[...]
        n_s[...] = n0[...]
        m_s[...] = m0[...]

    _mlstm_dir(0, qf, kf, vf, iff, hf, c_s, n_s, m_s)
    _mlstm_dir(1, qb, kb, vb, ifb, hb, c_s, n_s, m_s)

    @pl.when(c == nc - 1)
    def _():
        c_out[...] = c_s[...]
        n_out[...] = n_s[...]
        m_out[...] = m_s[...]


def _mlstm(q, k, v, gif, c0, n0, m0, nb, t):
    nc = t // ML_CHUNK
    nst = 2 * ML_HEADS
    fw = lambda wd: pl.BlockSpec((ML_CHUNK, wd), lambda b, c: (b * nc + c, 0))
    bw = lambda wd: pl.BlockSpec((ML_CHUNK, wd), lambda b, c: (b * nc + nc - 1 - c, 0))
    st = lambda shp: pl.BlockSpec((None,) + shp, lambda b, c: (b,) + (0,) * len(shp))
    shapes = ((nst, HEAD_DIM, HEAD_DIM), (nst, 1, HEAD_DIM), (nst, 1, 1))
    return pl.pallas_call(
        functools.partial(_mlstm_kernel, nc=nc), grid=(nb, nc),
        in_specs=[fw(256), fw(256), fw(256), fw(LANE), bw(256), bw(256), bw(256), bw(LANE)]
                 + [st(s) for s in shapes],
        out_specs=[fw(256), bw(256)] + [st(s) for s in shapes],
        out_shape=[jax.ShapeDtypeStruct((nb * t, BRANCH_W), F32)] * 2
                  + [jax.ShapeDtypeStruct((nb,) + s, F32) for s in shapes],
        scratch_shapes=[pltpu.VMEM(s, F32) for s in shapes],
        compiler_params=_cp(("parallel", "arbitrary")), name="mlstm_scan",
    )(q, k, v, gif, q, k, v, gif, c0, n0, m0)


def _prep_weights(w_in, b_in, mla_q_norm, w_uq, w_uk, w_uv, w_branch, w_out, peer_wq, peer_keys, peer_u, peer_v):
    offs = np.cumsum((0,) + IN_SPLITS)
    seg = lambda a, i: a[..., offs[i]:offs[i + 1]]
    pad = lambda a, wd: jnp.pad(a, [(0, 0)] * (a.ndim - 1) + [(0, wd - a.shape[-1])])

    def layout(a):
        ml_if = pad(jnp.concatenate([seg(a, 7), seg(a, 8)], axis=-1), LANE)
        parts = [seg(a, i) for i in range(7)] + [ml_if, seg(a, 9), seg(a, 10), seg(a, 11),
                                                  pad(seg(a, 12), 256), seg(a, 13), pad(seg(a, 14), LANE)]
        return jnp.concatenate(parts, axis=-1)

    nl = w_in.shape[0]
    wq3 = w_uq.reshape(nl, MLA_Q_RANK, MLA_HEADS, MLA_NOPE + MLA_ROPE)
    w_uq_p = jnp.concatenate([wq3[..., :MLA_NOPE].reshape(nl, MLA_Q_RANK, -1),
                              wq3[..., MLA_NOPE:].reshape(nl, MLA_Q_RANK, -1)], axis=-1)
    w_uq_p = jnp.pad(w_uq_p, ((0, 0), (0, 256 - MLA_Q_RANK), (0, 0)))
    return dict(
        w_ab=layout(w_in).astype(BF16), b_ab=layout(b_in),
        w_c=seg(w_in, 15).astype(BF16), b_c=seg(b_in, 15),
        q_norm=pad(mla_q_norm, 256), w_uq=w_uq_p.astype(BF16),
        w_ukv=jnp.concatenate([w_uk, w_uv], axis=-1).astype(BF16),
        w_branch=w_branch.astype(BF16), w_out=w_out.astype(BF16), peer_wq=peer_wq.astype(BF16),
        peer_keys=peer_keys.astype(BF16), peer_u=peer_u.astype(BF16),
        peer_vt=jnp.swapaxes(peer_v, 1, 2).astype(BF16),
    )


def _project(x, wl, l, norm_g, sc, sh, rows_per_mod):
    outs = _linear(x, wl["w_ab"][l], [wd for _, wd in SEGS_AB], bias=wl["b_ab"][l], norm_g=norm_g,
                   mod=(sc, sh), rows_per_mod=rows_per_mod, tm=256, vmem_mb=48, name="in_proj_ab")
    p = {name: o for (name, _), o in zip(SEGS_AB, outs)}
    (p["gate"],) = _linear(x, wl["w_c"][l], [N_BRANCH * D_MODEL], bias=wl["b_c"][l], norm_g=norm_g,
                           mod=(sc, sh), rows_per_mod=rows_per_mod, tm=256, vmem_mb=48, name="in_proj_gate")
    (qm,) = _linear(p["cq"], wl["w_uq"][l], [384], norm_g=wl["q_norm"][l], k_valid=MLA_Q_RANK, tm=512,
                    name="mla_q_up")
    return p, qm


def kernel(x_prompt, x_sample, cache_na_k, cache_na_v, state_mlstm_C, state_mlstm_n, state_mlstm_m,
           cache_swa_k, cache_swa_v, cache_mla_ckv, cache_mla_krope, c, c_ctx, w_mod, b_mod, norm1_g, norm2_g,
           w_in, b_in, na_rpb, sw_sink, mla_q_norm, w_uq, mla_kv_norm, w_uk, w_uv, w_branch, w_out, peer_wq,
           peer_keys, peer_u, peer_v, final_norm_g):
    nbp, tp, d = x_prompt.shape
    nbs, ts, _ = x_sample.shape
    nl = w_in.shape[0]
    lc = cache_na_k.shape[2]
    nst = 2 * ML_HEADS
    wl = _prep_weights(w_in, b_in, mla_q_norm, w_uq, w_uk, w_uv, w_branch, w_out, peer_wq, peer_keys,
                       peer_u, peer_v)

    n_cond = 1 + nbs
    cond = jnp.zeros((16, d), F32).at[0].set(c_ctx).at[1:n_cond].set(c)
    mods = _modulation(cond, w_mod, b_mod).reshape(nl, 16, 6, 1, d)

    tabs_swq = _rope_tables(ts, SW_HEADS, HEAD_DIM)
    tabs_swk = _rope_tables(ts, SW_KV_HEADS, HEAD_DIM)
    tabs_mla = _rope_tables(ts, MLA_HEADS, MLA_ROPE)

    xp = x_prompt.reshape(nbp * tp, d)
    xs = x_sample.reshape(nbs * ts, d)
    zeros_c = jnp.zeros((nbp, nst, HEAD_DIM, HEAD_DIM), F32)
    zeros_n = jnp.zeros((nbp, nst, 1, HEAD_DIM), F32)
    zeros_m = jnp.zeros((nbp, nst, 1, 1), F32)
    per_layer = tuple([] for _ in range(9))

    for l in range(nl):
        mod_p = [mods[l, 0:1, i] for i in range(6)]
        mod_s = [mods[l, 1:n_cond, i] for i in range(6)]
        sink = sw_sink[l]

        p, qm = _project(xp, wl, l, norm1_g[l], mod_p[1], mod_p[0], nbp * tp)
        ckv_n, kn, vm = _linear(p["ckv"], wl["w_ukv"][l], [256, 256], norm_g=mla_kv_norm[l], emit_normed=True,
                                tm=512, name="mla_kv_up")
        o_na, o_sw, o_mla = _ctx_attention(p, qm, kn, vm, sink, nbp, tp)
        hf, hb, c_new, n_new, m_new = _mlstm(p["ml_q"], p["ml_k"], p["ml_v"], p["ml_if"], zeros_c, zeros_n,
                                             zeros_m, nbp, tp)
        xp = _merge(o_na, hf, hb, p["ml_o"], o_sw, o_mla, p["gate"], wl["w_branch"][l], wl["w_out"][l], xp,
                    mod_p[2], nbp * tp)
        xp = _peer(xp, norm2_g[l], mod_p[4], mod_p[3], mod_p[5], wl["peer_wq"][l], wl["peer_keys"][l],
                   wl["peer_u"][l], wl["peer_vt"][l], nbp * tp)
        ctx_t = (p["na_k"].reshape(nbp, tp, NA_HEADS, HEAD_DIM), p["na_v"].reshape(nbp, tp, NA_HEADS, HEAD_DIM),
                 c_new.reshape(nbp, 2, ML_HEADS, HEAD_DIM, HEAD_DIM), n_new.reshape(nbp, 2, ML_HEADS, HEAD_DIM),
                 m_new.reshape(nbp, 2, ML_HEADS),
                 p["sw_k"].reshape(nbp, tp, SW_KV_HEADS, HEAD_DIM), p["sw_v"].reshape(nbp, tp, SW_KV_HEADS, HEAD_DIM),
                 ckv_n.reshape(nbp, tp, MLA_KV_RANK), p["krope"][:, :MLA_ROPE].reshape(nbp, tp, MLA_ROPE))
        for i, a in enumerate(ctx_t):
            per_layer[i].append(a)

        p, qm = _project(xs, wl, l, norm1_g[l], mod_s[1], mod_s[0], ts)
        _, kn, vm = _linear(p["ckv"], wl["w_ukv"][l], [256, 256], norm_g=mla_kv_norm[l], emit_normed=True,
                            tm=512, name="mla_kv_up")
        knc, vc = _linear(cache_mla_ckv[:, l].reshape(nbs * lc, MLA_KV_RANK), wl["w_ukv"][l], [256, 256], tm=512,
                          name="mla_kv_up_cache")
        o_na = _na_attention(p["na_q"], p["na_k"], p["na_v"], cache_na_k[:, l].reshape(nbs, lc, BRANCH_W),
                             cache_na_v[:, l].reshape(nbs, lc, BRANCH_W), _na_bias_tables(na_rpb[l], ts // GRID_W), nbs, ts)
        o_sw = _sw_attention(p["sw_q"], p["sw_k"], p["sw_v"],
                             cache_swa_k[:, l].reshape(nbs, lc, SW_KV_HEADS * HEAD_DIM),
                             cache_swa_v[:, l].reshape(nbs, lc, SW_KV_HEADS * HEAD_DIM), sink, tabs_swq, tabs_swk,
                             nbs, ts)
        o_mla = _mla_attention(qm, kn, p["krope"], vm, knc, cache_mla_krope[:, l], vc, tabs_mla, nbs, ts)
        hf, hb, _, _, _ = _mlstm(p["ml_q"], p["ml_k"], p["ml_v"], p["ml_if"],
                                 state_mlstm_C[:, l].reshape(nbs, nst, HEAD_DIM, HEAD_DIM),
                                 state_mlstm_n[:, l].reshape(nbs, nst, 1, HEAD_DIM),
                                 state_mlstm_m[:, l].reshape(nbs, nst, 1, 1), nbs, ts)
        xs = _merge(o_na, hf, hb, p["ml_o"], o_sw, o_mla, p["gate"], wl["w_branch"][l], wl["w_out"][l], xs,
                    mod_s[2], ts)
        xs = _peer(xs, norm2_g[l], mod_s[4], mod_s[3], mod_s[5], wl["peer_wq"][l], wl["peer_keys"][l],
                   wl["peer_u"][l], wl["peer_vt"][l], ts)

    y_prompt = _final_norm(xp, final_norm_g).reshape(nbp, tp, d)
    y_sample = _final_norm(xs, final_norm_g).reshape(nbs, ts, d)
    return (y_prompt, y_sample) + tuple(jnp.stack(s, axis=1) for s in per_layer)
```

```python
import functools

import numpy as np
import jax
import jax.numpy as jnp
from jax import lax
from jax.experimental import pallas as pl
from jax.experimental.pallas import tpu as pltpu

F32 = jnp.float32
BF16 = jnp.bfloat16

D_MODEL = 1024
GRID_W = 64
HEAD_DIM = 64
N_BRANCH = 4
BRANCH_W = 256
NA_HEADS = 4
NA_KR_MAX = 8
NA_KC = 16
ML_HEADS = 4
ML_CHUNK = 128
SW_HEADS = 4
SW_KV_HEADS = 2
SW_GROUP = SW_HEADS // SW_KV_HEADS
SW_WINDOW = 128
MLA_HEADS = 4
MLA_Q_RANK = 192
MLA_KV_RANK = 128
MLA_NOPE = 64
MLA_ROPE = 32
MLA_V = 64
MLA_SCALE = (MLA_NOPE + MLA_ROPE) ** -0.5
PEER_HEADS = 8
PEER_KEY_DIM = 64
PEER_N_KEYS = 128
PEER_TOPK = 16
ROPE_BASE = 10000.0
RMS_EPS = 1e-6
NEG = -1e30
IN_SPLITS = (256, 256, 256, 256, 256, 256, 256, 8, 8, 256, 128, 128, 192, 128, 32, 4096)

LANE = 128
MLA_QK_W = MLA_HEADS * LANE

SEGS_AB = (("na_q", 256), ("na_k", 256), ("na_v", 256), ("ml_q", 256), ("ml_k", 256), ("ml_v", 256),
           ("ml_o", 256), ("ml_if", 128), ("sw_q", 256), ("sw_k", 128), ("sw_v", 128), ("cq", 256),
           ("ckv", 128), ("krope", 128))
KROPE_SLICE = slice(MLA_NOPE, MLA_NOPE + MLA_ROPE)


def _cp(sem, vmem_mb=None):
    kw = dict(dimension_semantics=sem)
    if vmem_mb is not None:
        kw["vmem_limit_bytes"] = vmem_mb * 1024 * 1024
    return pltpu.CompilerParams(**kw)


def _dot(a, b):
    return jnp.dot(a, b, preferred_element_type=F32)


def _dot_nt(a, b):
    return lax.dot_general(a, b, (((1,), (1,)), ((), ())), preferred_element_type=F32)


def _rms(x, inv_n):
    return x * lax.rsqrt(jnp.sum(x * x, axis=-1, keepdims=True) * inv_n + RMS_EPS)


def _sigmoid(x):
    return 1.0 / (1.0 + jnp.exp(-x))


def _mod_kernel(c_ref, w_ref, b_ref, o_ref):
    c = c_ref[...]
    s = c * _sigmoid(c)
    o_ref[...] = _dot(s.astype(BF16), w_ref[...].astype(BF16)) + b_ref[...]


def _modulation(cond, w_mod, b_mod):
    nl, d, n = w_mod.shape
    r = cond.shape[0]
    tn = 1024
    return pl.pallas_call(
        _mod_kernel,
        grid=(nl, n // tn),
        in_specs=[pl.BlockSpec((r, d), lambda l, j: (0, 0)),
                  pl.BlockSpec((None, d, tn), lambda l, j: (l, 0, j)),
                  pl.BlockSpec((None, 1, tn), lambda l, j: (l, 0, j))],
        out_specs=pl.BlockSpec((None, r, tn), lambda l, j: (l, 0, j)),
        out_shape=jax.ShapeDtypeStruct((nl, r, n), F32),
        compiler_params=_cp(("parallel", "parallel")),
        name="adaln_modulation",
    )(cond, w_mod, b_mod.reshape(nl, 1, n))


def _linear_kernel(*refs, segs, has_norm, has_mod, has_bias, emit_normed, transpose_last, inv_k):
    it = iter(refs)
    x_ref = next(it)
    g_ref = next(it) if has_norm else None
    sc_ref = next(it) if has_mod else None
    sh_ref = next(it) if has_mod else None
    w_ref = next(it)
    b_ref = next(it) if has_bias else None
    outs = list(it)
    x = x_ref[...]
    if has_norm:
        x = _rms(x, inv_k) * g_ref[...]
    if has_mod:
        x = x * (1.0 + sc_ref[0]) + sh_ref[0]
    if emit_normed:
        outs[0][...] = x
        outs = outs[1:]
    xb = x.astype(BF16)
    for idx, ((s, wd), o) in enumerate(zip(segs, outs)):
        y = _dot(xb, w_ref[:, s:s + wd])
        if has_bias:
            y = y + b_ref[:, s:s + wd]
        o[...] = y.T if (transpose_last and idx == len(segs) - 1) else y


def _linear(x, w, widths, *, bias=None, norm_g=None, k_valid=None, mod=None, rows_per_mod=None,
            emit_normed=False, transpose_last=False, tm=256, vmem_mb=None, name="linear"):
    n, k = x.shape
    tm = min(tm, n)
    segs, s = [], 0
    for wd in widths:
        segs.append((s, wd))
        s += wd
    assert s == w.shape[1] and n % tm == 0
    args = [x]
    in_specs = [pl.BlockSpec((tm, k), lambda i: (i, 0))]
    if norm_g is not None:
        args.append(norm_g.reshape(1, k))
        in_specs.append(pl.BlockSpec((1, k), lambda i: (0, 0)))
    if mod is not None:
        assert rows_per_mod % tm == 0
        for m in mod:
            args.append(m)
            in_specs.append(pl.BlockSpec((1, 1, k), lambda i: ((i * tm) // rows_per_mod, 0, 0)))
    args.append(w)
    in_specs.append(pl.BlockSpec(w.shape, lambda i: (0, 0)))
    if bias is not None:
        args.append(bias.reshape(1, -1))
        in_specs.append(pl.BlockSpec((1, w.shape[1]), lambda i: (0, 0)))
    out_shapes, out_specs = [], []
    if emit_normed:
        out_shapes.append(jax.ShapeDtypeStruct((n, k), F32))
        out_specs.append(pl.BlockSpec((tm, k), lambda i: (i, 0)))
    for idx, wd in enumerate(widths):
        if transpose_last and idx == len(widths) - 1:
            out_shapes.append(jax.ShapeDtypeStruct((wd, n), F32))
            out_specs.append(pl.BlockSpec((wd, tm), lambda i: (0, i)))
        else:
            out_shapes.append(jax.ShapeDtypeStruct((n, wd), F32))
            out_specs.append(pl.BlockSpec((tm, wd), lambda i: (i, 0)))
    kern = functools.partial(_linear_kernel, segs=tuple(segs), has_norm=norm_g is not None,
                             has_mod=mod is not None, has_bias=bias is not None, emit_normed=emit_normed,
                             transpose_last=transpose_last, inv_k=1.0 / (k_valid or k))
    return pl.pallas_call(kern, grid=(n // tm,), in_specs=in_specs, out_specs=out_specs,
                          out_shape=out_shapes, compiler_params=_cp(("parallel",), vmem_mb), name=name)(*args)


def _merge_kernel(ona, hf, hb, mlo, osw, omla, gate, wb, wo, x, g1, bd, o):
    h = hf[...] + hb[...]
    hsq = h * h
    hi = hsq.astype(BF16)
    lo = (hsq - hi.astype(F32)).astype(BF16)
    ms = (_dot(hi, bd[...]) + _dot(lo, bd[...])) * (1.0 / HEAD_DIM)
    oml = h * lax.rsqrt(ms + RMS_EPS) * _sigmoid(mlo[...])
    branches = (ona[...], oml, osw[...], omla[...])
    acc = None
    for n in range(N_BRANCH):
        y = _sigmoid(gate[:, n * D_MODEL:(n + 1) * D_MODEL]) * _dot(branches[n].astype(BF16), wb[n])
        acc = y if acc is None else acc + y
    o[...] = x[...] + g1[0] * _dot(acc.astype(BF16), wo[...])


def _merge(ona, hf, hb, mlo, osw, omla, gate, wb, wo, x, g1, rows_per_mod, tm=256):
    n = x.shape[0]
    tm = min(tm, n)
    bd = jnp.asarray(np.kron(np.eye(BRANCH_W // HEAD_DIM), np.ones((HEAD_DIM, HEAD_DIM))), BF16)
    row = lambda wd: pl.BlockSpec((tm, wd), lambda i: (i, 0))
    return pl.pallas_call(
        _merge_kernel,
        grid=(n // tm,),
        in_specs=[row(BRANCH_W)] * 6 + [row(N_BRANCH * D_MODEL),
                  pl.BlockSpec(wb.shape, lambda i: (0, 0, 0)),
                  pl.BlockSpec(wo.shape, lambda i: (0, 0)),
                  row(D_MODEL),
                  pl.BlockSpec((1, 1, D_MODEL), lambda i: ((i * tm) // rows_per_mod, 0, 0)),
                  pl.BlockSpec(bd.shape, lambda i: (0, 0))],
        out_specs=row(D_MODEL),
        out_shape=jax.ShapeDtypeStruct((n, D_MODEL), F32),
        compiler_params=_cp(("parallel",), 48),
        name="branch_merge",
    )(ona, hf, hb, mlo, osw, omla, gate, wb, wo, x, g1, bd)


def _top_rows(s, k):
    cur, rows = s, []
    for _ in range(k):
        mx = jnp.max(cur, axis=0, keepdims=True)
        rows.append(mx)
        cur = jnp.where(cur == mx, NEG, cur)
    return rows


def _peer_select(s1, s2):
    rows8 = lax.broadcasted_iota(jnp.int32, (8, 1), 0)
    a1, a2 = _top_rows(s1, PEER_TOPK), _top_rows(s2, PEER_TOPK)
    a1_16 = jnp.concatenate(a1, axis=0)
    a1_8 = a1_16[:8]
    cands = [a1_16 + a2[0], a1_8 + a2[1]]
    for k2 in range(2, 8):
        cands.append(jnp.where(rows8 < PEER_TOPK // (k2 + 1), a1_8 + a2[k2], NEG))
    cands.append(a1[0] + jnp.concatenate(a2[8:], axis=0))
    cand = jnp.concatenate(cands, axis=0)
    top = _top_rows(cand, PEER_TOPK + 1)
    tau = 0.5 * (top[PEER_TOPK - 1] + top[PEER_TOPK])
    z = jnp.sum(jnp.where(cand >= tau, jnp.exp(cand - top[0]), 0.0), axis=0, keepdims=True)
    return tau - s1, jnp.exp(s1 - a1[0]), jnp.exp(s2 - a2[0]) * (0.5 / z)


def _peer_kernel(x_ref, ng_ref, sc_ref, sh_ref, g2_ref, wq_ref, keys_ref, u_ref, vt_ref, o_ref,
                 h2t_ref, acc_ref, t1_ref, e1_ref, s2_ref, e2_ref, ht0_ref, ht1_ref, z0_ref, z1_ref, *,
                 ec, n_chunks, tm):
    s = pl.program_id(1)
    n_i1 = ec // PEER_N_KEYS
    n_tc = tm // LANE

    @pl.when(s == 0)
    def _():
        x = x_ref[...]
        h2 = _rms(x, 1.0 / D_MODEL) * ng_ref[...] * (1.0 + sc_ref[0]) + sh_ref[0]
        h2t_ref[...] = h2.T.astype(BF16)
        q = _dot(h2.astype(BF16), wq_ref[...])
        qt = q.T.astype(BF16)
        for h in range(PEER_HEADS):
            r0 = h * 2 * PEER_KEY_DIM
            s1 = _dot(keys_ref[h, 0], qt[r0:r0 + PEER_KEY_DIM, :])
            s2 = _dot(keys_ref[h, 1], qt[r0 + PEER_KEY_DIM:r0 + 2 * PEER_KEY_DIM, :])
            for tc in range(n_tc):
                cols = slice(tc * LANE, (tc + 1) * LANE)
                t1, e1, e2 = _peer_select(s1[:, cols], s2[:, cols])
                t1_ref[h, :, cols] = t1
                e1_ref[h, :, cols] = e1
                s2_ref[h, tc] = s2[:, cols]
                e2_ref[h, tc] = e2
        for r in (ht0_ref, ht1_ref, z0_ref, z1_ref, acc_ref):
            r[...] = jnp.zeros_like(r)

    cb = jnp.clip(s - 1, 0, n_chunks - 1)

    def stages(ht_w, ht_r, z_w, z_r):
        orows = D_MODEL // n_i1
        zfull = jnp.concatenate([z_r[tc] for tc in range(n_tc)], axis=1)

        def matmul_pieces(j):
            rows = slice(j * PEER_N_KEYS, (j + 1) * PEER_N_KEYS)
            hj = _dot(u_ref[rows, :], h2t_ref[...])
            for tc in range(n_tc):
                ht_w[tc, rows, :] = hj[:, tc * LANE:(tc + 1) * LANE]
            osl = slice(j * orows, (j + 1) * orows)
            acc_ref[osl, :] += _dot(vt_ref[osl, :], zfull)

        for j in range(n_i1):
            i1 = cb * n_i1 + j
            rows = slice(j * PEER_N_KEYS, (j + 1) * PEER_N_KEYS)
            matmul_pieces(j)
            th_rows = [t1_ref[h, pl.ds(i1, 1), :] for h in range(PEER_HEADS)]
            e1_rows = [e1_ref[h, pl.ds(i1, 1), :] for h in range(PEER_HEADS)]
            for tc in range(n_tc):
                cols = slice(tc * LANE, (tc + 1) * LANE)
                w = None
                for h in range(PEER_HEADS):
                    th = th_rows[h][:, cols]
                    e1r = e1_rows[h][:, cols]
                    term = jnp.where(s2_ref[h, tc] >= th, e2_ref[h, tc], 0.0) * e1r
                    w = term if w is None else w + term
                xh = ht_r[tc, rows, :]
                t = jnp.tanh(xh * (0.7978845608028654 + 0.035677408136300125 * (xh * xh)))
                z_w[tc, rows, :] = (w * (xh + xh * t)).astype(BF16)

    @pl.when(s % 2 == 0)
    def _():
        stages(ht0_ref, ht1_ref, z1_ref, z0_ref)

    @pl.when(s % 2 == 1)
    def _():
        stages(ht1_ref, ht0_ref, z0_ref, z1_ref)

    @pl.when(s == n_chunks + 1)
    def _():
        o_ref[...] = x_ref[...] + g2_ref[0] * acc_ref[...].T


def _peer(x, ng, sc, sh, g2, wq, keys, u, vt, rows_per_mod, tm=512, ec=512):
    n = x.shape[0]
    tm = min(tm, n)
    n_exp = u.shape[0]
    n_chunks = n_exp // ec
    last = n_chunks - 1
    modspec = pl.BlockSpec((1, 1, D_MODEL), lambda i, s: ((i * tm) // rows_per_mod, 0, 0))
    kern = functools.partial(_peer_kernel, ec=ec, n_chunks=n_chunks, tm=tm)
    head_buf = pltpu.VMEM((PEER_HEADS, PEER_N_KEYS, tm), F32)
    tile_buf = pltpu.VMEM((PEER_HEADS, tm // LANE, PEER_N_KEYS, LANE), F32)
    return pl.pallas_call(
        kern,
        grid=(n // tm, n_chunks + 2),
        in_specs=[pl.BlockSpec((tm, D_MODEL), lambda i, s: (i, 0)),
                  pl.BlockSpec((1, D_MODEL), lambda i, s: (0, 0)),
                  modspec, modspec, modspec,
                  pl.BlockSpec(wq.shape, lambda i, s: (0, 0)),
                  pl.BlockSpec(keys.shape, lambda i, s: (0, 0, 0, 0)),
                  pl.BlockSpec((ec, D_MODEL), lambda i, s: (jnp.minimum(s, last), 0)),
                  pl.BlockSpec((D_MODEL, ec), lambda i, s: (0, jnp.clip(s - 2, 0, last)))],
        out_specs=pl.BlockSpec((tm, D_MODEL), lambda i, s: (i, 0)),
        out_shape=jax.ShapeDtypeStruct((n, D_MODEL), F32),
        scratch_shapes=[pltpu.VMEM((D_MODEL, tm), BF16), pltpu.VMEM((D_MODEL, tm), F32),
                        head_buf, head_buf, tile_buf, tile_buf,
                        pltpu.VMEM((tm // LANE, ec, LANE), F32), pltpu.VMEM((tm // LANE, ec, LANE), F32),
                        pltpu.VMEM((tm // LANE, ec, LANE), BF16), pltpu.VMEM((tm // LANE, ec, LANE), BF16)],
        compiler_params=_cp(("parallel", "arbitrary"), 56),
        name="peer_ffn",
    )(x, ng.reshape(1, D_MODEL), sc, sh, g2, wq, keys, u, vt)


def _final_norm_kernel(x_ref, g_ref, o_ref):
    o_ref[...] = _rms(x_ref[...], 1.0 / D_MODEL) * g_ref[...]


def _final_norm(x, g, tm=512):
    n = x.shape[0]
    tm = min(tm, n)
    return pl.pallas_call(
        _final_norm_kernel, grid=(n // tm,),
        in_specs=[pl.BlockSpec((tm, D_MODEL), lambda i: (i, 0)), pl.BlockSpec((1, D_MODEL), lambda i: (0, 0))],
        out_specs=pl.BlockSpec((tm, D_MODEL), lambda i: (i, 0)),
        out_shape=jax.ShapeDtypeStruct((n, D_MODEL), F32),
        compiler_params=_cp(("parallel",)), name="final_norm")(x, g.reshape(1, D_MODEL))


def _rope_tables(t, n_heads, dh, lead=0, width=None):
    width = width or dh
    pos = jnp.arange(t)
    sec, half = dh // 2, dh // 4
    freqs = ROPE_BASE ** (-jnp.arange(half, dtype=F32) / half)
    first = jnp.asarray((np.arange(dh) % sec) < half)
    ang_r = (pos // GRID_W).astype(F32)[:, None] * freqs[None, :]
    ang_c = (pos % GRID_W).astype(F32)[:, None] * freqs[None, :]
    ang = jnp.concatenate([ang_r, ang_r, ang_c, ang_c], axis=1)
    cos, sin = jnp.cos(ang), jnp.sin(ang)
    sa = jnp.where(first[None, :], -sin, 0.0)
    sb = jnp.where(first[None, :], 0.0, sin)
    padw = ((0, 0), (lead, width - dh - lead))
    place = lambda a, fill: jnp.tile(jnp.pad(a, padw, constant_values=fill), (1, n_heads))
    return place(cos, 1.0), place(sa, 0.0), place(sb, 0.0)


def _rope(x, cos, sa, sb, half):
    w = x.shape[-1]
    return x * cos + pltpu.roll(x, w - half, 1) * sa + pltpu.roll(x, half, 1) * sb


def _softmax_pv(blocks, sink=None):
    m = None
    for s, _ in blocks:
        bm = jnp.max(s, axis=1, keepdims=True)
        m = bm if m is None else jnp.maximum(m, bm)
    if sink is not None:
        m = jnp.maximum(m, sink)
    l, acc = None, None
    for s, v in blocks:
        p = jnp.exp(s - m)
        bl = jnp.sum(p, axis=1, keepdims=True)
        pv = _dot(p.astype(BF16), v)
        l = bl if l is None else l + bl
        acc = pv if acc is None else acc + pv
    if sink is not None:
        l = l + jnp.exp(sink - m)
    return acc / l


def _ctx_attn_kernel(naq, nak, nav, swq, swk, swv, qm, kn, kr, vm, sink_ref, o_na, o_sw, o_mla):
    hd = HEAD_DIM
    scale = hd ** -0.5
    q, k, v = naq[...].astype(BF16), nak[...].astype(BF16), nav[...].astype(BF16)
    for h in range(NA_HEADS):
        hs = slice(h * hd, (h + 1) * hd)
        o_na[:, hs] = _softmax_pv([(_dot_nt(q[:, hs], k[:, hs]) * scale, v[:, hs])])
    q, k, v = swq[...].astype(BF16), swk[...].astype(BF16), swv[...].astype(BF16)
    for h in range(SW_HEADS):
        g = h // SW_GROUP
        hs, gs = slice(h * hd, (h + 1) * hd), slice(g * hd, (g + 1) * hd)
        o_sw[:, hs] = _softmax_pv([(_dot_nt(q[:, hs], k[:, gs]) * scale, v[:, gs])], sink=sink_ref[h])
    q, v = qm[...].astype(BF16), vm[...].astype(BF16)
    krv = kr[...]
    for h in range(MLA_HEADS):
        ks = slice(h * LANE, (h + 1) * LANE)
        kcat = (kn[:, ks] + krv).astype(BF16)
        s = _dot_nt(q[:, ks], kcat) * MLA_SCALE
        o_mla[:, h * MLA_V:(h + 1) * MLA_V] = _softmax_pv([(s, v[:, h * MLA_V:(h + 1) * MLA_V])])


def _ctx_attention(p, qm, kn, vm, sink, nb, t):
    row = lambda wd: pl.BlockSpec((t, wd), lambda b: (b, 0))
    outs = pl.pallas_call(
        _ctx_attn_kernel, grid=(nb,),
        in_specs=[row(256), row(256), row(256), row(256), row(128), row(128), row(MLA_QK_W), row(MLA_QK_W),
                  row(LANE), row(256), pl.BlockSpec(memory_space=pltpu.SMEM)],
        out_specs=[row(256)] * 3,
        out_shape=[jax.ShapeDtypeStruct((nb * t, BRANCH_W), F32)] * 3,
        compiler_params=_cp(("parallel",)), name="ctx_attention",
    )(p["na_q"], p["na_k"], p["na_v"], p["sw_q"], p["sw_k"], p["sw_v"], qm, kn, p["krope"], vm, sink)
    return outs


NA_TQ = 256


def _na_bias_tables(rpb, rows_n):
    nr, nc = 2 * NA_KR_MAX - 1, 2 * NA_KC - 1
    p_sel = np.zeros((3, 4, 12, nr), np.float32)
    ok_r = np.zeros((3, 4, 12), bool)
    for ti, j in enumerate((0, 1, rows_n // 4 - 1)):
        for a in range(4):
            r = 4 * j + a
            r0 = min(max(r - NA_KR_MAX // 2, 0), rows_n - NA_KR_MAX)
            for e in range(12):
                krow = 4 * (j - 1) + e
                if r0 <= krow < r0 + NA_KR_MAX:
                    ok_r[ti, a, e] = True
                    p_sel[ti, a, e, krow - r + NA_KR_MAX - 1] = 1.0
    c = np.arange(GRID_W)[:, None]
    w = np.arange(GRID_W)[None, :]
    cs = np.clip(c - NA_KC // 2, 0, GRID_W - NA_KC)
    ok_c = (w >= cs) & (w < cs + NA_KC)
    q_sel = np.eye(nc, dtype=np.float32)[np.clip(w - c + NA_KC - 1, 0, nc - 1)]
    hp = lax.Precision.HIGHEST
    m1 = jnp.einsum("taei,lhij->lthaej", jnp.asarray(p_sel), rpb, precision=hp)
    b = jnp.einsum("lthaej,cwj->lthacew", m1, jnp.asarray(q_sel), precision=hp)
    ok = ok_r[:, :, None, :, None] & ok_c[None, None, :, None, :]
    b = jnp.where(jnp.asarray(ok)[None, :, None], b, NEG)
    return b.reshape(rpb.shape[0], 3, NA_HEADS, 4 * GRID_W, 12 * GRID_W)


def _na_kernel(q, k0, k1, k2, v0, v1, v2, kc, vc, bias, o):
    hd = HEAD_DIM
    scale = hd ** -0.5
    qb = q[...].astype(BF16)
    kk = jnp.concatenate([k0[...], k1[...], k2[...]], axis=0).astype(BF16)
    vv = jnp.concatenate([v0[...], v1[...], v2[...]], axis=0).astype(BF16)
    kcb, vcb = kc[...].astype(BF16), vc[...].astype(BF16)
    for h in range(NA_HEADS):
        hs = slice(h * hd, (h + 1) * hd)
        s_nb = _dot_nt(qb[:, hs], kk[:, hs]) * scale + bias[h]
        s_cx = _dot_nt(qb[:, hs], kcb[:, hs]) * scale
        o[:, hs] = _softmax_pv([(s_nb, vv[:, hs]), (s_cx, vcb[:, hs])])


def _na_attention(q, k, v, kc, vc, bias, nb, t):
    nj = t // NA_TQ
    lc = kc.shape[1]
    blk = lambda f: pl.BlockSpec((NA_TQ, BRANCH_W), lambda b, j: (b * nj + f(j), 0))
    prev = lambda j: jnp.maximum(j - 1, 0)
    nxt = lambda j: jnp.minimum(j + 1, nj - 1)
    cur = lambda j: j
    ctx = pl.BlockSpec((None, lc, BRANCH_W), lambda b, j: (b, 0, 0))
    tsel = lambda b, j: (jnp.where(j == 0, 0, jnp.where(j == nj - 1, 2, 1)), 0, 0, 0)
    return pl.pallas_call(
        _na_kernel, grid=(nb, nj),
        in_specs=[blk(cur), blk(prev), blk(cur), blk(nxt), blk(prev), blk(cur), blk(nxt), ctx, ctx,
                  pl.BlockSpec((None,) + bias.shape[1:], tsel)],
        out_specs=blk(cur),
        out_shape=jax.ShapeDtypeStruct((nb * t, BRANCH_W), F32),
        compiler_params=_cp(("parallel", "arbitrary"), 48), name="na_attention",
    )(q, k, k, k, v, v, v, kc, vc, bias)


SW_TQ = 256


def _sw_kernel(q, k0, k1, k2, v0, v1, v2, kc, vc, cq, saq, sbq, ck0, sak0, sbk0, ck1, sak1, sbk1,
               ck2, sak2, sbk2, sink_ref, o, *, t):
    hd = HEAD_DIM
    scale = hd ** -0.5
    half = hd // 4
    j = pl.program_id(1)
    qr = _rope(q[...], cq[...], saq[...], sbq[...], half).astype(BF16)
    kk = jnp.concatenate([_rope(k0[...], ck0[...], sak0[...], sbk0[...], half),
                          _rope(k1[...], ck1[...], sak1[...], sbk1[...], half),
                          _rope(k2[...], ck2[...], sak2[...], sbk2[...], half)], axis=0).astype(BF16)
    vv = jnp.concatenate([v0[...], v1[...], v2[...]], axis=0).astype(BF16)
    kcb, vcb = kc[...].astype(BF16), vc[...].astype(BF16)
    qpos = j * SW_TQ + lax.broadcasted_iota(jnp.int32, (SW_TQ, 3 * SW_TQ), 0)
    kpos = (j - 1) * SW_TQ + lax.broadcasted_iota(jnp.int32, (SW_TQ, 3 * SW_TQ), 1)
    valid = (jnp.abs(qpos - kpos) <= SW_WINDOW) & (kpos >= 0) & (kpos < t)
    for h in range(SW_HEADS):
        g = h // SW_GROUP
        hs, gs = slice(h * hd, (h + 1) * hd), slice(g * hd, (g + 1) * hd)
        s_loc = jnp.where(valid, _dot_nt(qr[:, hs], kk[:, gs]) * scale, NEG)
        s_cx = _dot_nt(qr[:, hs], kcb[:, gs]) * scale
        o[:, hs] = _softmax_pv([(s_loc, vv[:, gs]), (s_cx, vcb[:, gs])], sink=sink_ref[h])


def _sw_attention(q, k, v, kc, vc, sink, tabs_q, tabs_k, nb, t):
    nj = t // SW_TQ
    lc = kc.shape[1]
    kvw = SW_KV_HEADS * HEAD_DIM
    prev = lambda j: jnp.maximum(j - 1, 0)
    nxt = lambda j: jnp.minimum(j + 1, nj - 1)
    cur = lambda j: j
    qblk = pl.BlockSpec((SW_TQ, BRANCH_W), lambda b, j: (b * nj + j, 0))
    kblk = lambda f: pl.BlockSpec((SW_TQ, kvw), lambda b, j: (b * nj + f(j), 0))
    ctx = pl.BlockSpec((None, lc, kvw), lambda b, j: (b, 0, 0))
    tq = pl.BlockSpec((SW_TQ, BRANCH_W), lambda b, j: (j, 0))
    tk = lambda f: pl.BlockSpec((SW_TQ, kvw), lambda b, j: (f(j), 0))
    in_specs = [qblk, kblk(prev), kblk(cur), kblk(nxt), kblk(prev), kblk(cur), kblk(nxt), ctx, ctx, tq, tq, tq]
    args = [q, k, k, k, v, v, v, kc, vc, *tabs_q]
    for f in (prev, cur, nxt):
        in_specs += [tk(f)] * 3
        args += list(tabs_k)
    in_specs.append(pl.BlockSpec(memory_space=pltpu.SMEM))
    args.append(sink)
    return pl.pallas_call(
        functools.partial(_sw_kernel, t=t), grid=(nb, nj), in_specs=in_specs, out_specs=qblk,
        out_shape=jax.ShapeDtypeStruct((nb * t, BRANCH_W), F32),
        compiler_params=_cp(("parallel", "arbitrary"), 48), name="sw_attention",
    )(*args)


MLA_TQ = 512
MLA_TK = 512


def _mla_kernel(qm, kn, kr, vt, knc, krc, vtc, cq, saq, sbq, ck, sak, sbk, o, qt_ref, m_ref, l_ref, acc_ref, *,
                n_lat):
    kj = pl.program_id(2)
    half = MLA_ROPE // 4

    @pl.when(kj == 0)
    def _():
        qt_ref[...] = _rope(qm[...], cq[...], saq[...], sbq[...], half).T.astype(BF16)
        m_ref[...] = jnp.full(m_ref.shape, NEG, F32)
        l_ref[...] = jnp.zeros(l_ref.shape, F32)
        acc_ref[...] = jnp.zeros(acc_ref.shape, F32)

    def step(kcat, vt_b):
        for h in range(MLA_HEADS):
            ks = slice(h * LANE, (h + 1) * LANE)
            vs = slice(h * MLA_V, (h + 1) * MLA_V)
            s = _dot(kcat[:, ks], qt_ref[ks, :]) * MLA_SCALE
            m_old = m_ref[h]
            m_new = jnp.maximum(m_old, jnp.max(s, axis=0, keepdims=True))
            alpha = jnp.exp(m_old - m_new)
            p = jnp.exp(s - m_new)
            l_ref[h] = alpha * l_ref[h] + jnp.sum(p, axis=0, keepdims=True)
            acc_ref[vs, :] = alpha * acc_ref[vs, :] + _dot(vt_b[vs, :], p.astype(BF16))
            m_ref[h] = m_new

    @pl.when(kj < n_lat)
    def _():
        krr = _rope(kr[...], ck[...], sak[...], sbk[...], half)
        step((kn[...] + jnp.concatenate([krr] * MLA_HEADS, axis=1)).astype(BF16), vt[...].astype(BF16))

    @pl.when(kj == n_lat)
    def _():
        step((knc[...] + jnp.concatenate([krc[...]] * MLA_HEADS, axis=1)).astype(BF16), vtc[...].astype(BF16))
        for h in range(MLA_HEADS):
            vs = slice(h * MLA_V, (h + 1) * MLA_V)
            acc_ref[vs, :] = acc_ref[vs, :] / l_ref[h]
        o[...] = acc_ref[...].T


def _mla_attention(qm, kn, kr, vt, knc, krc, vtc, tabs_q, tabs_k, nb, t):
    nq, n_lat = t // MLA_TQ, t // MLA_TK
    lc = krc.shape[1]
    assert lc == MLA_TK
    kb = lambda b, i, kj: b * n_lat + jnp.minimum(kj, n_lat - 1)
    kblk = lambda wd: pl.BlockSpec((MLA_TK, wd), lambda b, i, kj: (kb(b, i, kj), 0))
    tkb = pl.BlockSpec((MLA_TK, LANE), lambda b, i, kj: (jnp.minimum(kj, n_lat - 1), 0))
    tqb = pl.BlockSpec((MLA_TQ, MLA_QK_W), lambda b, i, kj: (i, 0))
    return pl.pallas_call(
        functools.partial(_mla_kernel, n_lat=n_lat), grid=(nb, nq, n_lat + 1),
        in_specs=[pl.BlockSpec((MLA_TQ, MLA_QK_W), lambda b, i, kj: (b * nq + i, 0)),
                  kblk(MLA_QK_W), kblk(LANE),
                  pl.BlockSpec((BRANCH_W, MLA_TK), lambda b, i, kj: (0, kb(b, i, kj))),
                  pl.BlockSpec((lc, MLA_QK_W), lambda b, i, kj: (b, 0)),
                  pl.BlockSpec((None, lc, LANE), lambda b, i, kj: (b, 0, 0)),
                  pl.BlockSpec((BRANCH_W, lc), lambda b, i, kj: (0, b)),
                  tqb, tqb, tqb, tkb, tkb, tkb],
        out_specs=pl.BlockSpec((MLA_TQ, BRANCH_W), lambda b, i, kj: (b * nq + i, 0)),
        out_shape=jax.ShapeDtypeStruct((nb * t, BRANCH_W), F32),
        scratch_shapes=[pltpu.VMEM((MLA_QK_W, MLA_TQ), BF16), pltpu.VMEM((MLA_HEADS, 1, MLA_TQ), F32),
                        pltpu.VMEM((MLA_HEADS, 1, MLA_TQ), F32), pltpu.VMEM((BRANCH_W, MLA_TQ), F32)],
        compiler_params=_cp(("parallel", "parallel", "arbitrary"), 48), name="mla_attention",
    )(qm, kn, kr, vt, knc, krc, vtc, *tabs_q, *tabs_k)


def _log_sigmoid(x):
    return jnp.minimum(x, 0.0) - jnp.log(1.0 + jnp.exp(-jnp.abs(x)))


def _mlstm_dir(d, q_ref, k_ref, v_ref, if_ref, o_ref, state):
    ln = ML_CHUNK
    hd = HEAD_DIM
    scale = hd ** -0.5
    t_in = if_ref[...]
    logf = _log_sigmoid(t_in)
    ri = lax.broadcasted_iota(jnp.int32, (ln, ln), 0)
    ci = lax.broadcasted_iota(jnp.int32, (ln, ln), 1)
    mask = (ci <= ri) if d == 0 else (ci >= ri)
    tri = jnp.where(mask, 1.0, 0.0).astype(BF16)
    hi = logf.astype(BF16)
    r1 = logf - hi.astype(F32)
    mid = r1.astype(BF16)
    lo = (r1 - mid.astype(F32)).astype(BF16)
    bcol = _dot(tri, hi) + _dot(tri, mid) + _dot(tri, lo)
    brow = bcol.T
    trow = t_in.T
    q, k, v = q_ref[...], k_ref[...], v_ref[...]
    new_state = []
    for h in range(ML_HEADS):
        ch_i = d * ML_HEADS + h
        ch_f = 2 * ML_HEADS + ch_i
        hs = slice(h * hd, (h + 1) * hd)
        c_prev, n_prev, m_prev = state[ch_i]
        bc = bcol[:, ch_f:ch_f + 1]
        br = brow[ch_f:ch_f + 1, :]
        ir = trow[ch_i:ch_i + 1, :]
        dm = jnp.where(mask, bc - br + ir, NEG)
        inter = bc + m_prev
        m_t = jnp.maximum(inter, jnp.max(dm, axis=1, keepdims=True))
        w_intra = jnp.exp(dm - m_t)
        w_inter = jnp.exp(inter - m_t)
        qh, kh = q[:, hs], k[:, hs] * scale
        qb, kb, vb = qh.astype(BF16), kh.astype(BF16), v[:, hs].astype(BF16)
        s = _dot_nt(qb, kb) * w_intra
        num = _dot(s.astype(BF16), vb) + w_inter * _dot(qb, c_prev.astype(BF16))
        den = jnp.sum(s, axis=1, keepdims=True) + w_inter * jnp.sum(qh * n_prev, axis=1, keepdims=True)
        o_ref[:, hs] = num / jnp.maximum(jnp.abs(den), jnp.exp(-m_t))
        b_last = bc[ln - 1:ln, :] if d == 0 else bc[0:1, :]
        g = b_last - br + ir
        m_new = jnp.maximum(b_last + m_prev, jnp.max(g, axis=1, keepdims=True))
        w = jnp.exp(g - m_new)
        decay = jnp.exp(b_last + m_prev - m_new)
        c_new = decay * c_prev + _dot((kh.T * w).astype(BF16), vb)
        n_new = decay * n_prev + _dot(jnp.broadcast_to(w, (8, ln)).astype(BF16), kb)[0:1]
        new_state.append((c_new, n_new, m_new))
    return new_state


def _mlstm_kernel(qf, kf, vf, iff, qb, kb, vb, ifb, c0, n0, m0, hf, hb, c_out, n_out, m_out, c_s, n_s, m_s, *, nc):
    c = pl.program_id(1)
    nst = 2 * ML_HEADS

    @pl.when(c == 0)
    def _():
        c_s[...] = c0[...]
        n_s[...] = n0[...]
        m_s[...] = m0[...]

    state = [(c_s[i], n_s[i], m_s[i]) for i in range(nst)]
    new_state = (_mlstm_dir(0, qf, kf, vf, iff, hf, state) + _mlstm_dir(1, qb, kb, vb, ifb, hb, state))
    for i, (cn, nn, mn) in enumerate(new_state):
        c_s[i] = cn
        n_s[i] = nn
        m_s[i] = mn

    @pl.when(c == nc - 1)
    def _():
        c_out[...] = c_s[...]
        n_out[...] = n_s[...]
        m_out[...] = m_s[...]


def _mlstm(q, k, v, gif, c0, n0, m0, nb, t):
    nc = t // ML_CHUNK
    nst = 2 * ML_HEADS
    fw = lambda wd: pl.BlockSpec((ML_CHUNK, wd), lambda b, c: (b * nc + c, 0))
    bw = lambda wd: pl.BlockSpec((ML_CHUNK, wd), lambda b, c: (b * nc + nc - 1 - c, 0))
    st = lambda shp: pl.BlockSpec((None,) + shp, lambda b, c: (b,) + (0,) * len(shp))
    shapes = ((nst, HEAD_DIM, HEAD_DIM), (nst, 1, HEAD_DIM), (nst, 1, 1))
    return pl.pallas_call(
        functools.partial(_mlstm_kernel, nc=nc), grid=(nb, nc),
        in_specs=[fw(256), fw(256), fw(256), fw(LANE), bw(256), bw(256), bw(256), bw(LANE)]
                 + [st(s) for s in shapes],
        out_specs=[fw(256), bw(256)] + [st(s) for s in shapes],
        out_shape=[jax.ShapeDtypeStruct((nb * t, BRANCH_W), F32)] * 2
                  + [jax.ShapeDtypeStruct((nb,) + s, F32) for s in shapes],
        scratch_shapes=[pltpu.VMEM(s, F32) for s in shapes],
        compiler_params=_cp(("parallel", "arbitrary")), name="mlstm_scan",
    )(q, k, v, gif, q, k, v, gif, c0, n0, m0)


def _prep_weights(w_in, b_in, mla_q_norm, w_uq, w_uk, w_uv, w_branch, w_out, peer_wq, peer_keys, peer_u, peer_v):
    offs = np.cumsum((0,) + IN_SPLITS)
    seg = lambda a, i: a[..., offs[i]:offs[i + 1]]
    padr = lambda a, wd, lead=0: jnp.pad(a, [(0, 0)] * (a.ndim - 1) + [(lead, wd - a.shape[-1] - lead)])

    def layout(a):
        ml_if = padr(jnp.concatenate([seg(a, 7), seg(a, 8)], axis=-1), LANE)
        parts = [seg(a, i) for i in range(7)] + [ml_if, seg(a, 9), seg(a, 10), seg(a, 11), padr(seg(a, 12), 256),
                                                  seg(a, 13), padr(seg(a, 14), LANE, KROPE_SLICE.start)]
        return jnp.concatenate(parts, axis=-1)

    nl = w_in.shape[0]
    w_uq_p = padr(w_uq.reshape(nl, MLA_Q_RANK, MLA_HEADS, MLA_NOPE + MLA_ROPE), LANE).reshape(nl, MLA_Q_RANK, -1)
    w_uq_p = jnp.pad(w_uq_p, ((0, 0), (0, 256 - MLA_Q_RANK), (0, 0)))
    w_uk_p = padr(w_uk.reshape(nl, MLA_KV_RANK, MLA_HEADS, MLA_NOPE), LANE).reshape(nl, MLA_KV_RANK, -1)
    return dict(
        w_ab=layout(w_in).astype(BF16), b_ab=layout(b_in),
        w_c=seg(w_in, 15).astype(BF16), b_c=seg(b_in, 15),
        q_norm=padr(mla_q_norm, 256), w_uq=w_uq_p.astype(BF16),
        w_ukv=jnp.concatenate([w_uk_p, w_uv], axis=-1).astype(BF16),
        w_branch=w_branch.astype(BF16), w_out=w_out.astype(BF16), peer_wq=peer_wq.astype(BF16),
        peer_keys=peer_keys.astype(BF16), peer_u=peer_u.astype(BF16),
        peer_vt=jnp.swapaxes(peer_v, 1, 2).astype(BF16),
    )


def _project(x, wl, l, norm_g, sc, sh, rows_per_mod):
    outs = _linear(x, wl["w_ab"][l], [wd for _, wd in SEGS_AB], bias=wl["b_ab"][l], norm_g=norm_g,
                   mod=(sc, sh), rows_per_mod=rows_per_mod, tm=256, vmem_mb=48, name="in_proj_ab")
    p = {name: o for (name, _), o in zip(SEGS_AB, outs)}
    (p["gate"],) = _linear(x, wl["w_c"][l], [N_BRANCH * D_MODEL], bias=wl["b_c"][l], norm_g=norm_g,
                           mod=(sc, sh), rows_per_mod=rows_per_mod, tm=256, vmem_mb=48, name="in_proj_gate")
    (qm,) = _linear(p["cq"], wl["w_uq"][l], [MLA_QK_W], norm_g=wl["q_norm"][l], k_valid=MLA_Q_RANK, tm=512,
                    name="mla_q_up")
    return p, qm


def kernel(x_prompt, x_sample, cache_na_k, cache_na_v, state_mlstm_C, state_mlstm_n, state_mlstm_m,
           cache_swa_k, cache_swa_v, cache_mla_ckv, cache_mla_krope, c, c_ctx, w_mod, b_mod, norm1_g, norm2_g,
           w_in, b_in, na_rpb, sw_sink, mla_q_norm, w_uq, mla_kv_norm, w_uk, w_uv, w_branch, w_out, peer_wq,
           peer_keys, peer_u, peer_v, final_norm_g):
    nbp, tp, d = x_prompt.shape
    nbs, ts, _ = x_sample.shape
    nl = w_in.shape[0]
    lc = cache_na_k.shape[2]
    nst = 2 * ML_HEADS
    wl = _prep_weights(w_in, b_in, mla_q_norm, w_uq, w_uk, w_uv, w_branch, w_out, peer_wq, peer_keys,
                       peer_u, peer_v)

    n_cond = 1 + nbs
    cond = jnp.zeros((16, d), F32).at[0].set(c_ctx).at[1:n_cond].set(c)
    mods = _modulation(cond, w_mod, b_mod).reshape(nl, 16, 6, 1, d)

    tabs_swq = _rope_tables(ts, SW_HEADS, HEAD_DIM)
    tabs_swk = _rope_tables(ts, SW_KV_HEADS, HEAD_DIM)
    tabs_mlaq = _rope_tables(ts, MLA_HEADS, MLA_ROPE, lead=MLA_NOPE, width=LANE)
    tabs_mlak = _rope_tables(ts, 1, MLA_ROPE, lead=MLA_NOPE, width=LANE)
    na_bias = _na_bias_tables(na_rpb, ts // GRID_W)
    krope_c = jnp.pad(cache_mla_krope, ((0, 0), (0, 0), (0, 0), (KROPE_SLICE.start, LANE - KROPE_SLICE.stop)))

    xp = x_prompt.reshape(nbp * tp, d)
    xs = x_sample.reshape(nbs * ts, d)
    zeros_c = jnp.zeros((nbp, nst, HEAD_DIM, HEAD_DIM), F32)
    zeros_n = jnp.zeros((nbp, nst, 1, HEAD_DIM), F32)
    zeros_m = jnp.zeros((nbp, nst, 1, 1), F32)
    per_layer = tuple([] for _ in range(9))

    for l in range(nl):
        mod_p = [mods[l, 0:1, i] for i in range(6)]
        mod_s = [mods[l, 1:n_cond, i] for i in range(6)]
        sink = sw_sink[l]

        p, qm = _project(xp, wl, l, norm1_g[l], mod_p[1], mod_p[0], nbp * tp)
        ckv_n, kn, vm = _linear(p["ckv"], wl["w_ukv"][l], [MLA_QK_W, 256], norm_g=mla_kv_norm[l],
                                emit_normed=True, tm=512, name="mla_kv_up")
        o_na, o_sw, o_mla = _ctx_attention(p, qm, kn, vm, sink, nbp, tp)
        hf, hb, c_new, n_new, m_new = _mlstm(p["ml_q"], p["ml_k"], p["ml_v"], p["ml_if"], zeros_c, zeros_n,
                                             zeros_m, nbp, tp)
        xp = _merge(o_na, hf, hb, p["ml_o"], o_sw, o_mla, p["gate"], wl["w_branch"][l], wl["w_out"][l], xp,
                    mod_p[2], nbp * tp)
        xp = _peer(xp, norm2_g[l], mod_p[4], mod_p[3], mod_p[5], wl["peer_wq"][l], wl["peer_keys"][l],
                   wl["peer_u"][l], wl["peer_vt"][l], nbp * tp)
        ctx_t = (p["na_k"].reshape(nbp, tp, NA_HEADS, HEAD_DIM), p["na_v"].reshape(nbp, tp, NA_HEADS, HEAD_DIM),
                 c_new.reshape(nbp, 2, ML_HEADS, HEAD_DIM, HEAD_DIM), n_new.reshape(nbp, 2, ML_HEADS, HEAD_DIM),
                 m_new.reshape(nbp, 2, ML_HEADS),
                 p["sw_k"].reshape(nbp, tp, SW_KV_HEADS, HEAD_DIM), p["sw_v"].reshape(nbp, tp, SW_KV_HEADS, HEAD_DIM),
                 ckv_n.reshape(nbp, tp, MLA_KV_RANK), p["krope"][:, KROPE_SLICE].reshape(nbp, tp, MLA_ROPE))
        for i, a in enumerate(ctx_t):
            per_layer[i].append(a)

        p, qm = _project(xs, wl, l, norm1_g[l], mod_s[1], mod_s[0], ts)
        _, kn, vt = _linear(p["ckv"], wl["w_ukv"][l], [MLA_QK_W, 256], norm_g=mla_kv_norm[l], emit_normed=True,
                            transpose_last=True, tm=512, name="mla_kv_up_t")
        knc, vtc = _linear(cache_mla_ckv[:, l].reshape(nbs * lc, MLA_KV_RANK), wl["w_ukv"][l], [MLA_QK_W, 256],
                           transpose_last=True, tm=512, name="mla_kv_up_cache")
        o_na = _na_attention(p["na_q"], p["na_k"], p["na_v"], cache_na_k[:, l].reshape(nbs, lc, BRANCH_W),
                             cache_na_v[:, l].reshape(nbs, lc, BRANCH_W), na_bias[l], nbs, ts)
        o_sw = _sw_attention(p["sw_q"], p["sw_k"], p["sw_v"],
                             cache_swa_k[:, l].reshape(nbs, lc, SW_KV_HEADS * HEAD_DIM),
                             cache_swa_v[:, l].reshape(nbs, lc, SW_KV_HEADS * HEAD_DIM), sink, tabs_swq, tabs_swk,
                             nbs, ts)
        o_mla = _mla_attention(qm, kn, p["krope"], vt, knc, krope_c[:, l], vtc, tabs_mlaq, tabs_mlak, nbs, ts)
        hf, hb, _, _, _ = _mlstm(p["ml_q"], p["ml_k"], p["ml_v"], p["ml_if"],
                                 state_mlstm_C[:, l].reshape(nbs, nst, HEAD_DIM, HEAD_DIM),
                                 state_mlstm_n[:, l].reshape(nbs, nst, 1, HEAD_DIM),
                                 state_mlstm_m[:, l].reshape(nbs, nst, 1, 1), nbs, ts)
        xs = _merge(o_na, hf, hb, p["ml_o"], o_sw, o_mla, p["gate"], wl["w_branch"][l], wl["w_out"][l], xs,
                    mod_s[2], ts)
        xs = _peer(xs, norm2_g[l], mod_s[4], mod_s[3], mod_s[5], wl["peer_wq"][l], wl["peer_keys"][l],
                   wl["peer_u"][l], wl["peer_vt"][l], ts)

    y_prompt = _final_norm(xp, final_norm_g).reshape(nbp, tp, d)
    y_sample = _final_norm(xs, final_norm_g).reshape(nbs, ts, d)
    return (y_prompt, y_sample) + tuple(jnp.stack(s, axis=1) for s in per_layer)
```

```python
import functools

import numpy as np
import jax
import jax.numpy as jnp
from jax import lax
from jax.experimental import pallas as pl
from jax.experimental.pallas import tpu as pltpu

F32 = jnp.float32
BF16 = jnp.bfloat16

D_MODEL = 1024
GRID_W = 64
HEAD_DIM = 64
N_BRANCH = 4
BRANCH_W = 256
NA_HEADS = 4
NA_KR_MAX = 8
NA_KC = 16
ML_HEADS = 4
ML_CHUNK = 128
SW_HEADS = 4
SW_KV_HEADS = 2
SW_GROUP = SW_HEADS // SW_KV_HEADS
SW_WINDOW = 128
MLA_HEADS = 4
MLA_Q_RANK = 192
MLA_KV_RANK = 128
MLA_NOPE = 64
MLA_ROPE = 32
MLA_V = 64
MLA_SCALE = (MLA_NOPE + MLA_ROPE) ** -0.5
PEER_HEADS = 8
PEER_KEY_DIM = 64
PEER_N_KEYS = 128
PEER_TOPK = 16
PEER_EC = 512
ROPE_BASE = 10000.0
RMS_EPS = 1e-6
NEG = -1e30
IN_SPLITS = (256, 256, 256, 256, 256, 256, 256, 8, 8, 256, 128, 128, 192, 128, 32, 4096)

LANE = 128
MLA_QK_W = MLA_HEADS * LANE

SEGS_AB = (("na_q", 256), ("na_k", 256), ("na_v", 256), ("ml_q", 512), ("ml_k", 512), ("ml_v", 512),
           ("ml_o", 256), ("ml_if", 128), ("sw_q", 256), ("sw_k", 128), ("sw_v", 128), ("cq", 256),
           ("ckv", 128), ("krope", 128))
KROPE_SLICE = slice(MLA_NOPE, MLA_NOPE + MLA_ROPE)


def _cp(sem, vmem_mb=None):
    kw = dict(dimension_semantics=sem)
    if vmem_mb is not None:
        kw["vmem_limit_bytes"] = vmem_mb * 1024 * 1024
    return pltpu.CompilerParams(**kw)


def _dot(a, b):
    return jnp.dot(a, b, preferred_element_type=F32)


def _dot_nt(a, b):
    return lax.dot_general(a, b, (((1,), (1,)), ((), ())), preferred_element_type=F32)


def _rms(x, inv_n):
    return x * lax.rsqrt(jnp.sum(x * x, axis=-1, keepdims=True) * inv_n + RMS_EPS)


def _sigmoid(x):
    return 1.0 / (1.0 + jnp.exp(-x))


def _mod_kernel(c_ref, w_ref, b_ref, o_ref):
    c = c_ref[...]
    s = c * _sigmoid(c)
    o_ref[...] = _dot(s.astype(BF16), w_ref[...].astype(BF16)) + b_ref[...]


def _modulation(cond, w_mod, b_mod):
    nl, d, n = w_mod.shape
    r = cond.shape[0]
    tn = 1024
    return pl.pallas_call(
        _mod_kernel,
        grid=(nl, n // tn),
        in_specs=[pl.BlockSpec((r, d), lambda l, j: (0, 0)),
                  pl.BlockSpec((None, d, tn), lambda l, j: (l, 0, j)),
                  pl.BlockSpec((None, 1, tn), lambda l, j: (l, 0, j))],
        out_specs=pl.BlockSpec((None, r, tn), lambda l, j: (l, 0, j)),
        out_shape=jax.ShapeDtypeStruct((nl, r, n), F32),
        compiler_params=_cp(("parallel", "parallel")),
        name="adaln_modulation",
    )(cond, w_mod, b_mod.reshape(nl, 1, n))


def _linear_kernel(*refs, segs, has_norm, has_mod, has_bias, emit_normed, transpose_last, inv_k):
    it = iter(refs)
    x_ref = next(it)
    g_ref = next(it) if has_norm else None
    sc_ref = next(it) if has_mod else None
    sh_ref = next(it) if has_mod else None
    w_ref = next(it)
    b_ref = next(it) if has_bias else None
    outs = list(it)
    x = x_ref[...]
    if has_norm:
        x = _rms(x, inv_k) * g_ref[...]
    if has_mod:
        x = x * (1.0 + sc_ref[0]) + sh_ref[0]
    if emit_normed:
        outs[0][...] = x
        outs = outs[1:]
    xb = x.astype(BF16)
    for idx, ((s, wd), o) in enumerate(zip(segs, outs)):
        y = _dot(xb, w_ref[:, s:s + wd])
        if has_bias:
            y = y + b_ref[:, s:s + wd]
        o[...] = y.T if (transpose_last and idx == len(segs) - 1) else y


def _linear(x, w, widths, *, bias=None, norm_g=None, k_valid=None, mod=None, rows_per_mod=None,
            emit_normed=False, transpose_last=False, tm=256, vmem_mb=None, name="linear"):
    n, k = x.shape
    tm = min(tm, n)
    segs, s = [], 0
    for wd in widths:
        segs.append((s, wd))
        s += wd
    assert s == w.shape[1] and n % tm == 0
    args = [x]
    in_specs = [pl.BlockSpec((tm, k), lambda i: (i, 0))]
    if norm_g is not None:
        args.append(norm_g.reshape(1, k))
        in_specs.append(pl.BlockSpec((1, k), lambda i: (0, 0)))
    if mod is not None:
        assert rows_per_mod % tm == 0
        for m in mod:
            args.append(m)
            in_specs.append(pl.BlockSpec((1, 1, k), lambda i: ((i * tm) // rows_per_mod, 0, 0)))
    args.append(w)
    in_specs.append(pl.BlockSpec(w.shape, lambda i: (0, 0)))
    if bias is not None:
        args.append(bias.reshape(1, -1))
        in_specs.append(pl.BlockSpec((1, w.shape[1]), lambda i: (0, 0)))
    out_shapes, out_specs = [], []
    if emit_normed:
        out_shapes.append(jax.ShapeDtypeStruct((n, k), F32))
        out_specs.append(pl.BlockSpec((tm, k), lambda i: (i, 0)))
    for idx, wd in enumerate(widths):
        if transpose_last and idx == len(widths) - 1:
            out_shapes.append(jax.ShapeDtypeStruct((wd, n), F32))
            out_specs.append(pl.BlockSpec((wd, tm), lambda i: (0, i)))
        else:
            out_shapes.append(jax.ShapeDtypeStruct((n, wd), F32))
            out_specs.append(pl.BlockSpec((tm, wd), lambda i: (i, 0)))
    kern = functools.partial(_linear_kernel, segs=tuple(segs), has_norm=norm_g is not None,
                             has_mod=mod is not None, has_bias=bias is not None, emit_normed=emit_normed,
                             transpose_last=transpose_last, inv_k=1.0 / (k_valid or k))
    return pl.pallas_call(kern, grid=(n // tm,), in_specs=in_specs, out_specs=out_specs,
                          out_shape=out_shapes, compiler_params=_cp(("parallel",), vmem_mb), name=name)(*args)


def _merge_kernel(ona, hf, hb, mlo, osw, omla, gate, wb, wo, x, g1, bd, o):
    h = hf[...] + hb[...]
    hsq = h * h
    hi = hsq.astype(BF16)
    lo = (hsq - hi.astype(F32)).astype(BF16)
    ms = (_dot(hi, bd[...]) + _dot(lo, bd[...])) * (1.0 / HEAD_DIM)
    oml = h * lax.rsqrt(ms + RMS_EPS) * _sigmoid(mlo[...])
    branches = (ona[...], oml, osw[...], omla[...])
    acc = None
    for n in range(N_BRANCH):
        y = _sigmoid(gate[:, n * D_MODEL:(n + 1) * D_MODEL]) * _dot(branches[n].astype(BF16), wb[n])
        acc = y if acc is None else acc + y
    o[...] = x[...] + g1[0] * _dot(acc.astype(BF16), wo[...])


def _merge(ona, hf, hb, mlo, osw, omla, gate, wb, wo, x, g1, rows_per_mod, tm=256):
    n = x.shape[0]
    tm = min(tm, n)
    bd = jnp.asarray(np.kron(np.eye(BRANCH_W // HEAD_DIM), np.ones((HEAD_DIM, HEAD_DIM))), BF16)
    row = lambda wd: pl.BlockSpec((tm, wd), lambda i: (i, 0))
    return pl.pallas_call(
        _merge_kernel,
        grid=(n // tm,),
        in_specs=[row(BRANCH_W)] * 6 + [row(N_BRANCH * D_MODEL),
                  pl.BlockSpec(wb.shape, lambda i: (0, 0, 0)),
                  pl.BlockSpec(wo.shape, lambda i: (0, 0)),
                  row(D_MODEL),
                  pl.BlockSpec((1, 1, D_MODEL), lambda i: ((i * tm) // rows_per_mod, 0, 0)),
                  pl.BlockSpec(bd.shape, lambda i: (0, 0))],
        out_specs=row(D_MODEL),
        out_shape=jax.ShapeDtypeStruct((n, D_MODEL), F32),
        compiler_params=_cp(("parallel",), 48),
        name="branch_merge",
    )(ona, hf, hb, mlo, osw, omla, gate, wb, wo, x, g1, bd)


def _top_rows(s, k):
    cur, rows = s, []
    for _ in range(k):
        mx = jnp.max(cur, axis=0, keepdims=True)
        rows.append(mx)
        cur = jnp.where(cur == mx, NEG, cur)
    return rows


def _peer_select(s1, s2):
    rows8 = lax.broadcasted_iota(jnp.int32, (8, 1), 0)
    a1, a2 = _top_rows(s1, PEER_TOPK), _top_rows(s2, PEER_TOPK)
    a1_16 = jnp.concatenate(a1, axis=0)
    a1_8 = a1_16[:8]
    cands = [a1_16 + a2[0], a1_8 + a2[1]]
    for k2 in range(2, 8):
        cands.append(jnp.where(rows8 < PEER_TOPK // (k2 + 1), a1_8 + a2[k2], NEG))
    cands.append(a1[0] + jnp.concatenate(a2[8:], axis=0))
    cand = jnp.concatenate(cands, axis=0)
    top = _top_rows(cand, PEER_TOPK + 1)
    tau = 0.5 * (top[PEER_TOPK - 1] + top[PEER_TOPK])
    z = jnp.sum(jnp.where(cand >= tau, jnp.exp(cand - top[0]), 0.0), axis=0, keepdims=True)
    return tau - s1, jnp.exp(s1 - a1[0]), jnp.exp(s2 - a2[0]) * (0.5 / z)


def _peer_kernel(x_ref, ng_ref, sc_ref, sh_ref, g2_ref, wq_ref, keys_ref, u_ref, vt_ref, o_ref,
                 h2t_ref, acc_ref, t1_ref, e1_ref, s2_ref, e2_ref, ht0_ref, ht1_ref, z0_ref, z1_ref, *,
                 ec, n_chunks, tm):
    s = pl.program_id(1)
    n_i1 = ec // PEER_N_KEYS
    n_tc = tm // LANE

    @pl.when(s == 0)
    def _():
        x = x_ref[...]
        h2 = _rms(x, 1.0 / D_MODEL) * ng_ref[...] * (1.0 + sc_ref[0]) + sh_ref[0]
        h2t_ref[...] = h2.T.astype(BF16)
        q = _dot(h2.astype(BF16), wq_ref[...])
        qt = q.T.astype(BF16)
        for h in range(PEER_HEADS):
            r0 = h * 2 * PEER_KEY_DIM
            s1 = _dot(keys_ref[h, 0], qt[r0:r0 + PEER_KEY_DIM, :])
            s2 = _dot(keys_ref[h, 1], qt[r0 + PEER_KEY_DIM:r0 + 2 * PEER_KEY_DIM, :])
            for tc in range(n_tc):
                cols = slice(tc * LANE, (tc + 1) * LANE)
                t1, e1, e2 = _peer_select(s1[:, cols], s2[:, cols])
                t1_ref[h, :, cols] = t1
                e1_ref[h, :, cols] = e1
                s2_ref[h, tc] = s2[:, cols]
                e2_ref[h, tc] = e2
        for r in (ht0_ref, ht1_ref, z0_ref, z1_ref, acc_ref):
            r[...] = jnp.zeros_like(r)

    cb = jnp.clip(s - 1, 0, n_chunks - 1)

    def stages(ht_w, ht_r, z_w, z_r):
        orows = D_MODEL // n_i1
        half = n_tc // 2

        def up_piece(j, nh):
            rows = slice(j * PEER_N_KEYS, (j + 1) * PEER_N_KEYS)
            hj = _dot(u_ref[rows, :], h2t_ref[:, nh * half * LANE:(nh + 1) * half * LANE])
            for k in range(half):
                ht_w[nh * half + k, rows, :] = hj[:, k * LANE:(k + 1) * LANE]

        def down_piece(j, nh):
            osl = slice(j * orows, (j + 1) * orows)
            zh = jnp.concatenate([z_r[nh * half + k] for k in range(half)], axis=1)
            acc_ref[osl, nh * half * LANE:(nh + 1) * half * LANE] += _dot(vt_ref[osl, :], zh)

        pieces = [(fn, j, nh) for j in range(n_i1) for nh in (0, 1) for fn in (up_piece, down_piece)]
        th_rows = [[t1_ref[h, pl.ds(cb * n_i1 + j, 1), :] for h in range(PEER_HEADS)] for j in range(n_i1)]
        e1_rows = [[e1_ref[h, pl.ds(cb * n_i1 + j, 1), :] for h in range(PEER_HEADS)] for j in range(n_i1)]
        n_rh = 4
        rh_rows = PEER_N_KEYS // n_rh
        n_units = n_tc * n_rh
        for unit in range(n_units):
            for fn, j, nh in pieces[unit * len(pieces) // n_units:(unit + 1) * len(pieces) // n_units]:
                fn(j, nh)
            tc, rh = divmod(unit, n_rh)
            cols = slice(tc * LANE, (tc + 1) * LANE)
            r2 = slice(rh * rh_rows, (rh + 1) * rh_rows)
            ws = [None] * n_i1
            for h in range(PEER_HEADS):
                s2t, e2t = s2_ref[h, tc, r2, :], e2_ref[h, tc, r2, :]
                for j in range(n_i1):
                    term = jnp.where(s2t >= th_rows[j][h][:, cols], e2t, 0.0) * e1_rows[j][h][:, cols]
                    ws[j] = term if ws[j] is None else ws[j] + term
            for j in range(n_i1):
                rows = slice(j * PEER_N_KEYS + rh * rh_rows, j * PEER_N_KEYS + (rh + 1) * rh_rows)
                xh = ht_r[tc, rows, :]
                t = jnp.tanh(xh * (0.7978845608028654 + 0.035677408136300125 * (xh * xh)))
                z_w[tc, rows, :] = (ws[j] * (xh + xh * t)).astype(BF16)

    @pl.when(s % 2 == 0)
    def _():
        stages(ht0_ref, ht1_ref, z1_ref, z0_ref)

    @pl.when(s % 2 == 1)
    def _():
        stages(ht1_ref, ht0_ref, z0_ref, z1_ref)

    @pl.when(s == n_chunks + 1)
    def _():
        o_ref[...] = x_ref[...] + g2_ref[0] * acc_ref[...].T


def _peer(x, ng, sc, sh, g2, wq, keys, u, vt, rows_per_mod, tm=512):
    n = x.shape[0]
    tm = min(tm, n)
    n_chunks, _, ec = vt.shape
    last = n_chunks - 1
    modspec = pl.BlockSpec((1, 1, D_MODEL), lambda i, s: ((i * tm) // rows_per_mod, 0, 0))
    kern = functools.partial(_peer_kernel, ec=ec, n_chunks=n_chunks, tm=tm)
    head_buf = pltpu.VMEM((PEER_HEADS, PEER_N_KEYS, tm), F32)
    tile_buf = pltpu.VMEM((PEER_HEADS, tm // LANE, PEER_N_KEYS, LANE), F32)
    return pl.pallas_call(
        kern,
        grid=(n // tm, n_chunks + 2),
        in_specs=[pl.BlockSpec((tm, D_MODEL), lambda i, s: (i, 0)),
                  pl.BlockSpec((1, D_MODEL), lambda i, s: (0, 0)),
                  modspec, modspec, modspec,
                  pl.BlockSpec(wq.shape, lambda i, s: (0, 0)),
                  pl.BlockSpec(keys.shape, lambda i, s: (0, 0, 0, 0)),
                  pl.BlockSpec((ec, D_MODEL), lambda i, s: (jnp.minimum(s, last), 0)),
                  pl.BlockSpec((None, D_MODEL, ec), lambda i, s: (jnp.clip(s - 2, 0, last), 0, 0))],
        out_specs=pl.BlockSpec((tm, D_MODEL), lambda i, s: (i, 0)),
        out_shape=jax.ShapeDtypeStruct((n, D_MODEL), F32),
        scratch_shapes=[pltpu.VMEM((D_MODEL, tm), BF16), pltpu.VMEM((D_MODEL, tm), F32),
                        head_buf, head_buf, tile_buf, tile_buf,
                        pltpu.VMEM((tm // LANE, ec, LANE), F32), pltpu.VMEM((tm // LANE, ec, LANE), F32),
                        pltpu.VMEM((tm // LANE, ec, LANE), BF16), pltpu.VMEM((tm // LANE, ec, LANE), BF16)],
        compiler_params=_cp(("parallel", "arbitrary"), 56),
        name="peer_ffn",
    )(x, ng.reshape(1, D_MODEL), sc, sh, g2, wq, keys, u, vt)


def _final_norm_kernel(x_ref, g_ref, o_ref):
    o_ref[...] = _rms(x_ref[...], 1.0 / D_MODEL) * g_ref[...]


def _final_norm(x, g, tm=512):
    n = x.shape[0]
    tm = min(tm, n)
    return pl.pallas_call(
        _final_norm_kernel, grid=(n // tm,),
        in_specs=[pl.BlockSpec((tm, D_MODEL), lambda i: (i, 0)), pl.BlockSpec((1, D_MODEL), lambda i: (0, 0))],
        out_specs=pl.BlockSpec((tm, D_MODEL), lambda i: (i, 0)),
        out_shape=jax.ShapeDtypeStruct((n, D_MODEL), F32),
        compiler_params=_cp(("parallel",)), name="final_norm")(x, g.reshape(1, D_MODEL))


def _rope_tables(t, n_heads, dh, lead=0, width=None):
    width = width or dh
    pos = jnp.arange(t)
    sec, half = dh // 2, dh // 4
    freqs = ROPE_BASE ** (-jnp.arange(half, dtype=F32) / half)
    first = jnp.asarray((np.arange(dh) % sec) < half)
    ang_r = (pos // GRID_W).astype(F32)[:, None] * freqs[None, :]
    ang_c = (pos % GRID_W).astype(F32)[:, None] * freqs[None, :]
    ang = jnp.concatenate([ang_r, ang_r, ang_c, ang_c], axis=1)
    cos, sin = jnp.cos(ang), jnp.sin(ang)
    sa = jnp.where(first[None, :], -sin, 0.0)
    sb = jnp.where(first[None, :], 0.0, sin)
    padw = ((0, 0), (lead, width - dh - lead))
    place = lambda a, fill: jnp.tile(jnp.pad(a, padw, constant_values=fill), (1, n_heads))
    return place(cos, 1.0), place(sa, 0.0), place(sb, 0.0)


def _rope(x, cos, sa, sb, half):
    w = x.shape[-1]
    return x * cos + pltpu.roll(x, w - half, 1) * sa + pltpu.roll(x, half, 1) * sb


def _softmax_pv(blocks, sink=None):
    m = None
    for s, _ in blocks:
        bm = jnp.max(s, axis=1, keepdims=True)
        m = bm if m is None else jnp.maximum(m, bm)
    if sink is not None:
        m = jnp.maximum(m, sink)
    l, acc = None, None
    for s, v in blocks:
        p = jnp.exp(s - m)
        bl = jnp.sum(p, axis=1, keepdims=True)
        pv = _dot(p.astype(BF16), v)
        l = bl if l is None else l + bl
        acc = pv if acc is None else acc + pv
    if sink is not None:
        l = l + jnp.exp(sink - m)
    return acc / l


def _ctx_attn_kernel(naq, nak, nav, swq, swk, swv, qm, kn, kr, vm, sink_ref, o_na, o_sw, o_mla):
    hd = HEAD_DIM
    scale = hd ** -0.5
    q, k, v = naq[...].astype(BF16), nak[...].astype(BF16), nav[...].astype(BF16)
    for h in range(NA_HEADS):
        hs = slice(h * hd, (h + 1) * hd)
        o_na[:, hs] = _softmax_pv([(_dot_nt(q[:, hs], k[:, hs]) * scale, v[:, hs])])
    q, k, v = swq[...].astype(BF16), swk[...].astype(BF16), swv[...].astype(BF16)
    for h in range(SW_HEADS):
        g = h // SW_GROUP
        hs, gs = slice(h * hd, (h + 1) * hd), slice(g * hd, (g + 1) * hd)
        o_sw[:, hs] = _softmax_pv([(_dot_nt(q[:, hs], k[:, gs]) * scale, v[:, gs])], sink=sink_ref[h])
    q, v = qm[...].astype(BF16), vm[...].astype(BF16)
    krv = kr[...]
    for h in range(MLA_HEADS):
        ks = slice(h * LANE, (h + 1) * LANE)
        kcat = (kn[:, ks] + krv).astype(BF16)
        s = _dot_nt(q[:, ks], kcat) * MLA_SCALE
        o_mla[:, h * MLA_V:(h + 1) * MLA_V] = _softmax_pv([(s, v[:, h * MLA_V:(h + 1) * MLA_V])])


def _ctx_attention(p, qm, kn, vm, sink, nb, t):
    row = lambda wd: pl.BlockSpec((t, wd), lambda b: (b, 0))
    outs = pl.pallas_call(
        _ctx_attn_kernel, grid=(nb,),
        in_specs=[row(256), row(256), row(256), row(256), row(128), row(128), row(MLA_QK_W), row(MLA_QK_W),
                  row(LANE), row(256), pl.BlockSpec(memory_space=pltpu.SMEM)],
        out_specs=[row(256)] * 3,
        out_shape=[jax.ShapeDtypeStruct((nb * t, BRANCH_W), F32)] * 3,
        compiler_params=_cp(("parallel",)), name="ctx_attention",
    )(p["na_q"], p["na_k"], p["na_v"], p["sw_q"], p["sw_k"], p["sw_v"], qm, kn, p["krope"], vm, sink)
    return outs


NA_TQ = 256


def _na_bias_tables(rpb, rows_n):
    nr, nc = 2 * NA_KR_MAX - 1, 2 * NA_KC - 1
    p_sel = np.zeros((3, 4, 12, nr), np.float32)
    ok_r = np.zeros((3, 4, 12), bool)
    for ti, j in enumerate((0, 1, rows_n // 4 - 1)):
        for a in range(4):
            r = 4 * j + a
            r0 = min(max(r - NA_KR_MAX // 2, 0), rows_n - NA_KR_MAX)
            for e in range(12):
                krow = 4 * (j - 1) + e
                if r0 <= krow < r0 + NA_KR_MAX:
                    ok_r[ti, a, e] = True
                    p_sel[ti, a, e, krow - r + NA_KR_MAX - 1] = 1.0
    c = np.arange(GRID_W)[:, None]
    w = np.arange(GRID_W)[None, :]
    cs = np.clip(c - NA_KC // 2, 0, GRID_W - NA_KC)
    ok_c = (w >= cs) & (w < cs + NA_KC)
    q_sel = np.eye(nc, dtype=np.float32)[np.clip(w - c + NA_KC - 1, 0, nc - 1)]
    hp = lax.Precision.HIGHEST
    m1 = jnp.einsum("taei,lhij->lthaej", jnp.asarray(p_sel), rpb, precision=hp)
    b = jnp.einsum("lthaej,cwj->lthacew", m1, jnp.asarray(q_sel), precision=hp)
    ok = ok_r[:, :, None, :, None] & ok_c[None, None, :, None, :]
    b = jnp.where(jnp.asarray(ok)[None, :, None], b, NEG)
    return b.reshape(rpb.shape[0], 3, NA_HEADS, 4 * GRID_W, 12 * GRID_W)


def _na_kernel(q, k0, k1, k2, v0, v1, v2, kc, vc, bias, o):
    hd = HEAD_DIM
    scale = hd ** -0.5
    qb = q[...].astype(BF16)
    kk = jnp.concatenate([k0[...], k1[...], k2[...]], axis=0).astype(BF16)
    vv = jnp.concatenate([v0[...], v1[...], v2[...]], axis=0).astype(BF16)
    kcb, vcb = kc[...].astype(BF16), vc[...].astype(BF16)
    for h in range(NA_HEADS):
        hs = slice(h * hd, (h + 1) * hd)
        s_nb = _dot_nt(qb[:, hs], kk[:, hs]) * scale + bias[h]
        s_cx = _dot_nt(qb[:, hs], kcb[:, hs]) * scale
        o[:, hs] = _softmax_pv([(s_nb, vv[:, hs]), (s_cx, vcb[:, hs])])


def _na_attention(q, k, v, kc, vc, bias, nb, t):
    nj = t // NA_TQ
    lc = kc.shape[1]
    blk = lambda f: pl.BlockSpec((NA_TQ, BRANCH_W), lambda b, j: (b * nj + f(j), 0))
    prev = lambda j: jnp.maximum(j - 1, 0)
    nxt = lambda j: jnp.minimum(j + 1, nj - 1)
    cur = lambda j: j
    ctx = pl.BlockSpec((None, lc, BRANCH_W), lambda b, j: (b, 0, 0))
    tsel = lambda b, j: (jnp.where(j == 0, 0, jnp.where(j == nj - 1, 2, 1)), 0, 0, 0)
    return pl.pallas_call(
        _na_kernel, grid=(nb, nj),
        in_specs=[blk(cur), blk(prev), blk(cur), blk(nxt), blk(prev), blk(cur), blk(nxt), ctx, ctx,
                  pl.BlockSpec((None,) + bias.shape[1:], tsel)],
        out_specs=blk(cur),
        out_shape=jax.ShapeDtypeStruct((nb * t, BRANCH_W), F32),
        compiler_params=_cp(("parallel", "arbitrary"), 48), name="na_attention",
    )(q, k, k, k, v, v, v, kc, vc, bias)


SW_TQ = 256


def _sw_kernel(q, k0, k1, k2, v0, v1, v2, kc, vc, cq, saq, sbq, ck0, sak0, sbk0, ck1, sak1, sbk1,
               ck2, sak2, sbk2, sink_ref, o, *, t):
    hd = HEAD_DIM
    scale = hd ** -0.5
    half = hd // 4
    j = pl.program_id(1)
    qr = _rope(q[...], cq[...], saq[...], sbq[...], half).astype(BF16)
    kk = jnp.concatenate([_rope(k0[...], ck0[...], sak0[...], sbk0[...], half),
                          _rope(k1[...], ck1[...], sak1[...], sbk1[...], half),
                          _rope(k2[...], ck2[...], sak2[...], sbk2[...], half)], axis=0).astype(BF16)
    vv = jnp.concatenate([v0[...], v1[...], v2[...]], axis=0).astype(BF16)
    kcb, vcb = kc[...].astype(BF16), vc[...].astype(BF16)
    qpos = j * SW_TQ + lax.broadcasted_iota(jnp.int32, (SW_TQ, 3 * SW_TQ), 0)
    kpos = (j - 1) * SW_TQ + lax.broadcasted_iota(jnp.int32, (SW_TQ, 3 * SW_TQ), 1)
    valid = (jnp.abs(qpos - kpos) <= SW_WINDOW) & (kpos >= 0) & (kpos < t)
    for h in range(SW_HEADS):
        g = h // SW_GROUP
        hs, gs = slice(h * hd, (h + 1) * hd), slice(g * hd, (g + 1) * hd)
        s_loc = jnp.where(valid, _dot_nt(qr[:, hs], kk[:, gs]) * scale, NEG)
        s_cx = _dot_nt(qr[:, hs], kcb[:, gs]) * scale
        o[:, hs] = _softmax_pv([(s_loc, vv[:, gs]), (s_cx, vcb[:, gs])], sink=sink_ref[h])


def _sw_attention(q, k, v, kc, vc, sink, tabs_q, tabs_k, nb, t):
    nj = t // SW_TQ
    lc = kc.shape[1]
    kvw = SW_KV_HEADS * HEAD_DIM
    prev = lambda j: jnp.maximum(j - 1, 0)
    nxt = lambda j: jnp.minimum(j + 1, nj - 1)
    cur = lambda j: j
    qblk = pl.BlockSpec((SW_TQ, BRANCH_W), lambda b, j: (b * nj + j, 0))
    kblk = lambda f: pl.BlockSpec((SW_TQ, kvw), lambda b, j: (b * nj + f(j), 0))
    ctx = pl.BlockSpec((None, lc, kvw), lambda b, j: (b, 0, 0))
    tq = pl.BlockSpec((SW_TQ, BRANCH_W), lambda b, j: (j, 0))
    tk = lambda f: pl.BlockSpec((SW_TQ, kvw), lambda b, j: (f(j), 0))
    in_specs = [qblk, kblk(prev), kblk(cur), kblk(nxt), kblk(prev), kblk(cur), kblk(nxt), ctx, ctx, tq, tq, tq]
    args = [q, k, k, k, v, v, v, kc, vc, *tabs_q]
    for f in (prev, cur, nxt):
        in_specs += [tk(f)] * 3
        args += list(tabs_k)
    in_specs.append(pl.BlockSpec(memory_space=pltpu.SMEM))
    args.append(sink)
    return pl.pallas_call(
        functools.partial(_sw_kernel, t=t), grid=(nb, nj), in_specs=in_specs, out_specs=qblk,
        out_shape=jax.ShapeDtypeStruct((nb * t, BRANCH_W), F32),
        compiler_params=_cp(("parallel", "arbitrary"), 48), name="sw_attention",
    )(*args)


MLA_TQ = 512
MLA_TK = 512


def _mla_kernel(qm, kn, kr, vt, knc, krc, vtc, cq, saq, sbq, ck, sak, sbk, o, qt_ref, m_ref, l_ref, acc_ref, *,
                n_lat):
    kj = pl.program_id(2)
    half = MLA_ROPE // 4

    @pl.when(kj == 0)
    def _():
        qt_ref[...] = _rope(qm[...], cq[...], saq[...], sbq[...], half).T.astype(BF16)
        m_ref[...] = jnp.full(m_ref.shape, NEG, F32)
        l_ref[...] = jnp.zeros(l_ref.shape, F32)
        acc_ref[...] = jnp.zeros(acc_ref.shape, F32)

    def step(kcat, vt_b):
        for h in range(MLA_HEADS):
            ks = slice(h * LANE, (h + 1) * LANE)
            vs = slice(h * MLA_V, (h + 1) * MLA_V)
            s = _dot(kcat[:, ks], qt_ref[ks, :]) * MLA_SCALE
            m_old = m_ref[h]
            m_new = jnp.maximum(m_old, jnp.max(s, axis=0, keepdims=True))
            alpha = jnp.exp(m_old - m_new)
            p = jnp.exp(s - m_new)
            l_ref[h] = alpha * l_ref[h] + jnp.sum(p, axis=0, keepdims=True)
            acc_ref[vs, :] = alpha * acc_ref[vs, :] + _dot(vt_b[vs, :], p.astype(BF16))
            m_ref[h] = m_new

    @pl.when(kj < n_lat)
    def _():
        krr = _rope(kr[...], ck[...], sak[...], sbk[...], half)
        step((kn[...] + jnp.concatenate([krr] * MLA_HEADS, axis=1)).astype(BF16), vt[...].astype(BF16))

    @pl.when(kj == n_lat)
    def _():
        step((knc[...] + jnp.concatenate([krc[...]] * MLA_HEADS, axis=1)).astype(BF16), vtc[...].astype(BF16))
        for h in range(MLA_HEADS):
            vs = slice(h * MLA_V, (h + 1) * MLA_V)
            acc_ref[vs, :] = acc_ref[vs, :] / l_ref[h]
        o[...] = acc_ref[...].T


def _mla_attention(qm, kn, kr, vt, knc, krc, vtc, tabs_q, tabs_k, nb, t):
    nq, n_lat = t // MLA_TQ, t // MLA_TK
    lc = krc.shape[1]
    assert lc == MLA_TK
    kb = lambda b, i, kj: b * n_lat + jnp.minimum(kj, n_lat - 1)
    kblk = lambda wd: pl.BlockSpec((MLA_TK, wd), lambda b, i, kj: (kb(b, i, kj), 0))
    tkb = pl.BlockSpec((MLA_TK, LANE), lambda b, i, kj: (jnp.minimum(kj, n_lat - 1), 0))
    tqb = pl.BlockSpec((MLA_TQ, MLA_QK_W), lambda b, i, kj: (i, 0))
    return pl.pallas_call(
        functools.partial(_mla_kernel, n_lat=n_lat), grid=(nb, nq, n_lat + 1),
        in_specs=[pl.BlockSpec((MLA_TQ, MLA_QK_W), lambda b, i, kj: (b * nq + i, 0)),
                  kblk(MLA_QK_W), kblk(LANE),
                  pl.BlockSpec((BRANCH_W, MLA_TK), lambda b, i, kj: (0, kb(b, i, kj))),
                  pl.BlockSpec((lc, MLA_QK_W), lambda b, i, kj: (b, 0)),
                  pl.BlockSpec((None, lc, LANE), lambda b, i, kj: (b, 0, 0)),
                  pl.BlockSpec((BRANCH_W, lc), lambda b, i, kj: (0, b)),
                  tqb, tqb, tqb, tkb, tkb, tkb],
        out_specs=pl.BlockSpec((MLA_TQ, BRANCH_W), lambda b, i, kj: (b * nq + i, 0)),
        out_shape=jax.ShapeDtypeStruct((nb * t, BRANCH_W), F32),
        scratch_shapes=[pltpu.VMEM((MLA_QK_W, MLA_TQ), BF16), pltpu.VMEM((MLA_HEADS, 1, MLA_TQ), F32),
                        pltpu.VMEM((MLA_HEADS, 1, MLA_TQ), F32), pltpu.VMEM((BRANCH_W, MLA_TQ), F32)],
        compiler_params=_cp(("parallel", "parallel", "arbitrary"), 48), name="mla_attention",
    )(qm, kn, kr, vt, knc, krc, vtc, *tabs_q, *tabs_k)


def _log_sigmoid(x):
    return jnp.minimum(x, 0.0) - jnp.log(1.0 + jnp.exp(-jnp.abs(x)))


def _split3(x):
    hi = x.astype(BF16)
    r1 = x - hi.astype(F32)
    mid = r1.astype(BF16)
    lo = (r1 - mid.astype(F32)).astype(BF16)
    return jnp.concatenate([hi, mid, lo], axis=1)


def _mlstm_dir(d, q_ref, k_ref, v_ref, if_ref, o_ref, c_aug, m_prev, negsel_ref):
    ln = ML_CHUNK
    scale = HEAD_DIM ** -0.5
    t_in = if_ref[...]
    logf = _log_sigmoid(t_in)
    ri = lax.broadcasted_iota(jnp.int32, (ln, ln), 0)
    ci = lax.broadcasted_iota(jnp.int32, (ln, ln), 1)
    mask = (ci <= ri) if d == 0 else (ci >= ri)
    tri = jnp.where(mask, 1.0, 0.0).astype(BF16)
    fs = _split3(logf)
    bcol = _dot(tri, fs[:, :LANE]) + _dot(tri, fs[:, LANE:2 * LANE]) + _dot(tri, fs[:, 2 * LANE:])
    b = pltpu.roll(bcol, LANE - 2 * ML_HEADS, 1)
    a = t_in - b
    row = lax.broadcasted_iota(jnp.int32, (ln, LANE), 0)
    cm, sh = a, 1
    while sh < ln:
        if d == 0:
            cm = jnp.maximum(cm, jnp.where(row >= sh, pltpu.roll(cm, sh, 0), NEG))
        else:
            cm = jnp.maximum(cm, jnp.where(row < ln - sh, pltpu.roll(cm, ln - sh, 0), NEG))
        sh *= 2
    big_m = jnp.maximum(m_prev, cm)
    xs = _split3(big_m - m_prev)
    ys = _split3(b + big_m)
    at = (a - m_prev).T
    last = ln - 1 if d == 0 else 0
    m_last = big_m[last:last + 1, :]
    m_new = b[last:last + 1, :] + m_last
    decay = jnp.exp(m_prev - m_last)
    wt = jnp.exp(a - m_last).T
    q, k, v = q_ref[...], k_ref[...], v_ref[...]
    new_c = []
    for h in range(ML_HEADS):
        c = d * ML_HEADS + h
        hsl = slice(h * LANE, (h + 1) * LANE)
        neg_x = _dot(xs, negsel_ref[c])
        neg_y = _dot(ys, negsel_ref[c])
        w_intra = jnp.exp(jnp.where(mask, at[c:c + 1, :] + neg_x, NEG))
        kh = k[:, hsl] * scale
        qb, kb, vb = q[:, hsl].astype(BF16), kh.astype(BF16), v[:, hsl].astype(BF16)
        s = _dot_nt(qb, kb) * w_intra
        c_prev = c_aug[c]
        nd = _dot(s.astype(BF16), vb) + jnp.exp(neg_x) * _dot(qb, c_prev.astype(BF16))
        den = pltpu.roll(nd, HEAD_DIM, 1)
        hout = nd / jnp.maximum(jnp.abs(den), jnp.exp(neg_y))
        o_ref[:, h * HEAD_DIM:(h + 1) * HEAD_DIM] = hout[:, :HEAD_DIM]
        new_c.append(decay[:, c:c + 1] * c_prev + _dot((kh.T * wt[c:c + 1, :]).astype(BF16), vb))
    return new_c, m_new


def _mlstm_kernel(qf, kf, vf, iff, qb, kb, vb, ifb, c0, m0, negsel, hf, hb, c_out, m_out, c_s, m_s, *, nc):
    c = pl.program_id(1)
    nst = 2 * ML_HEADS

    @pl.when(c == 0)
    def _():
        c_s[...] = c0[...]
        m_s[...] = m0[...]

    c_aug = [c_s[i] for i in range(nst)]
    m_prev = m_s[...]
    cf, mf = _mlstm_dir(0, qf, kf, vf, iff, hf, c_aug, m_prev, negsel)
    cb, mb = _mlstm_dir(1, qb, kb, vb, ifb, hb, c_aug, m_prev, negsel)
    for i, cn in enumerate(cf + cb):
        c_s[i] = cn
    lane = lax.broadcasted_iota(jnp.int32, (1, LANE), 1)
    m_s[...] = jnp.where(lane < ML_HEADS, mf, mb)

    @pl.when(c == nc - 1)
    def _():
        c_out[...] = c_s[...]
        m_out[...] = m_s[...]


def _mlstm(q, k, v, gif, c0, n0, m0, nb, t):
    nc = t // ML_CHUNK
    nst = 2 * ML_HEADS
    hd = HEAD_DIM
    qkw = ML_HEADS * LANE
    c_aug0 = jnp.concatenate([c0.reshape(nb, nst, hd, hd),
                              jnp.broadcast_to(n0.reshape(nb, nst, hd, 1), (nb, nst, hd, hd))], axis=-1)
    c_aug0 = jnp.pad(c_aug0, ((0, 0), (0, 0), (0, LANE - hd), (0, 0)))
    m_row0 = jnp.pad(m0.reshape(nb, 1, nst), ((0, 0), (0, 0), (0, LANE - nst)))
    sel = np.zeros((nst, 3 * LANE, LANE), np.float32)
    for ch in range(nst):
        sel[ch, [ch, LANE + ch, 2 * LANE + ch], :] = -1.0
    fw = lambda wd: pl.BlockSpec((ML_CHUNK, wd), lambda b, c: (b * nc + c, 0))
    bw = lambda wd: pl.BlockSpec((ML_CHUNK, wd), lambda b, c: (b * nc + nc - 1 - c, 0))
    st = lambda shp: pl.BlockSpec((None,) + shp, lambda b, c: (b,) + (0,) * len(shp))
    shapes = ((nst, LANE, LANE), (1, LANE))
    hf, hb, c_fin, m_fin = pl.pallas_call(
        functools.partial(_mlstm_kernel, nc=nc), grid=(nb, nc),
        in_specs=[fw(qkw), fw(qkw), fw(qkw), fw(LANE), bw(qkw), bw(qkw), bw(qkw), bw(LANE)]
                 + [st(s) for s in shapes] + [pl.BlockSpec(sel.shape, lambda b, c: (0, 0, 0))],
        out_specs=[fw(BRANCH_W), bw(BRANCH_W)] + [st(s) for s in shapes],
        out_shape=[jax.ShapeDtypeStruct((nb * t, BRANCH_W), F32)] * 2
                  + [jax.ShapeDtypeStruct((nb,) + s, F32) for s in shapes],
        scratch_shapes=[pltpu.VMEM(s, F32) for s in shapes],
        compiler_params=_cp(("parallel", "arbitrary")), name="mlstm_scan",
    )(q, k, v, gif, q, k, v, gif, c_aug0, m_row0, jnp.asarray(sel, BF16))
    c_new = c_fin[:, :, :hd, :hd].reshape(nb, 2, ML_HEADS, hd, hd)
    n_new = c_fin[:, :, :hd, hd].reshape(nb, 2, ML_HEADS, hd)
    m_new = m_fin[:, 0, :nst].reshape(nb, 2, ML_HEADS)
    return hf, hb, c_new, n_new, m_new


def _prep_weights(w_in, b_in, mla_q_norm, w_uq, w_uk, w_uv, w_branch, w_out, peer_wq, peer_keys, peer_u, peer_v):
    offs = np.cumsum((0,) + IN_SPLITS)
    seg = lambda a, i: a[..., offs[i]:offs[i + 1]]
    padr = lambda a, wd, lead=0: jnp.pad(a, [(0, 0)] * (a.ndim - 1) + [(lead, wd - a.shape[-1] - lead)])

    def headpad(a, fill):
        a4 = a.reshape(a.shape[:-1] + (ML_HEADS, HEAD_DIM))
        a4 = jnp.pad(a4, [(0, 0)] * (a4.ndim - 1) + [(0, LANE - HEAD_DIM)], constant_values=fill)
        return a4.reshape(a.shape[:-1] + (ML_HEADS * LANE,))

    def layout(a, is_bias):
        ml_if = padr(jnp.concatenate([seg(a, 7), seg(a, 8)], axis=-1), LANE)
        ml = [headpad(seg(a, 3), 0.0), headpad(seg(a, 4), 0.0), headpad(seg(a, 5), 1.0 if is_bias else 0.0)]
        parts = [seg(a, i) for i in range(3)] + ml + [seg(a, 6), ml_if, seg(a, 9), seg(a, 10), seg(a, 11),
                                                       padr(seg(a, 12), 256), seg(a, 13),
                                                       padr(seg(a, 14), LANE, KROPE_SLICE.start)]
        return jnp.concatenate(parts, axis=-1)

    nl = w_in.shape[0]
    w_uq_p = padr(w_uq.reshape(nl, MLA_Q_RANK, MLA_HEADS, MLA_NOPE + MLA_ROPE), LANE).reshape(nl, MLA_Q_RANK, -1)
    w_uq_p = jnp.pad(w_uq_p, ((0, 0), (0, 256 - MLA_Q_RANK), (0, 0)))
    w_uk_p = padr(w_uk.reshape(nl, MLA_KV_RANK, MLA_HEADS, MLA_NOPE), LANE).reshape(nl, MLA_KV_RANK, -1)
    return dict(
        w_ab=layout(w_in, False).astype(BF16), b_ab=layout(b_in, True),
        w_c=seg(w_in, 15).astype(BF16), b_c=seg(b_in, 15),
        q_norm=padr(mla_q_norm, 256), w_uq=w_uq_p.astype(BF16),
        w_ukv=jnp.concatenate([w_uk_p, w_uv], axis=-1).astype(BF16),
        w_branch=w_branch.astype(BF16), w_out=w_out.astype(BF16), peer_wq=peer_wq.astype(BF16),
        peer_keys=peer_keys.astype(BF16), peer_u=peer_u.astype(BF16),
        peer_vt=jnp.swapaxes(peer_v.reshape(nl, -1, PEER_EC, peer_v.shape[-1]), 2, 3).astype(BF16),
    )


def _project(x, wl, l, norm_g, sc, sh, rows_per_mod):
    outs = _linear(x, wl["w_ab"][l], [wd for _, wd in SEGS_AB], bias=wl["b_ab"][l], norm_g=norm_g,
                   mod=(sc, sh), rows_per_mod=rows_per_mod, tm=256, vmem_mb=48, name="in_proj_ab")
    p = {name: o for (name, _), o in zip(SEGS_AB, outs)}
    (p["gate"],) = _linear(x, wl["w_c"][l], [N_BRANCH * D_MODEL], bias=wl["b_c"][l], norm_g=norm_g,
                           mod=(sc, sh), rows_per_mod=rows_per_mod, tm=256, vmem_mb=48, name="in_proj_gate")
    (qm,) = _linear(p["cq"], wl["w_uq"][l], [MLA_QK_W], norm_g=wl["q_norm"][l], k_valid=MLA_Q_RANK, tm=512,
                    name="mla_q_up")
    return p, qm


def kernel(x_prompt, x_sample, cache_na_k, cache_na_v, state_mlstm_C, state_mlstm_n, state_mlstm_m,
           cache_swa_k, cache_swa_v, cache_mla_ckv, cache_mla_krope, c, c_ctx, w_mod, b_mod, norm1_g, norm2_g,
           w_in, b_in, na_rpb, sw_sink, mla_q_norm, w_uq, mla_kv_norm, w_uk, w_uv, w_branch, w_out, peer_wq,
           peer_keys, peer_u, peer_v, final_norm_g):
    nbp, tp, d = x_prompt.shape
    nbs, ts, _ = x_sample.shape
    nl = w_in.shape[0]
    lc = cache_na_k.shape[2]
    nst = 2 * ML_HEADS
    wl = _prep_weights(w_in, b_in, mla_q_norm, w_uq, w_uk, w_uv, w_branch, w_out, peer_wq, peer_keys,
                       peer_u, peer_v)

    n_cond = 1 + nbs
    cond = jnp.zeros((16, d), F32).at[0].set(c_ctx).at[1:n_cond].set(c)
    mods = _modulation(cond, w_mod, b_mod).reshape(nl, 16, 6, 1, d)

    tabs_swq = _rope_tables(ts, SW_HEADS, HEAD_DIM)
    tabs_swk = _rope_tables(ts, SW_KV_HEADS, HEAD_DIM)
    tabs_mlaq = _rope_tables(ts, MLA_HEADS, MLA_ROPE, lead=MLA_NOPE, width=LANE)
    tabs_mlak = _rope_tables(ts, 1, MLA_ROPE, lead=MLA_NOPE, width=LANE)
    na_bias = _na_bias_tables(na_rpb, ts // GRID_W)
    krope_c = jnp.pad(cache_mla_krope, ((0, 0), (0, 0), (0, 0), (KROPE_SLICE.start, LANE - KROPE_SLICE.stop)))

    xp = x_prompt.reshape(nbp * tp, d)
    xs = x_sample.reshape(nbs * ts, d)
    zeros_c = jnp.zeros((nbp, 2, ML_HEADS, HEAD_DIM, HEAD_DIM), F32)
    zeros_n = jnp.zeros((nbp, 2, ML_HEADS, HEAD_DIM), F32)
    zeros_m = jnp.zeros((nbp, 2, ML_HEADS), F32)
    per_layer = tuple([] for _ in range(9))

    for l in range(nl):
        mod_p = [mods[l, 0:1, i] for i in range(6)]
        mod_s = [mods[l, 1:n_cond, i] for i in range(6)]
        sink = sw_sink[l]

        p, qm = _project(xp, wl, l, norm1_g[l], mod_p[1], mod_p[0], nbp * tp)
        ckv_n, kn, vm = _linear(p["ckv"], wl["w_ukv"][l], [MLA_QK_W, 256], norm_g=mla_kv_norm[l],
                                emit_normed=True, tm=512, name="mla_kv_up")
        o_na, o_sw, o_mla = _ctx_attention(p, qm, kn, vm, sink, nbp, tp)
        hf, hb, c_new, n_new, m_new = _mlstm(p["ml_q"], p["ml_k"], p["ml_v"], p["ml_if"], zeros_c, zeros_n,
                                             zeros_m, nbp, tp)
        xp = _merge(o_na, hf, hb, p["ml_o"], o_sw, o_mla, p["gate"], wl["w_branch"][l], wl["w_out"][l], xp,
                    mod_p[2], nbp * tp)
        xp = _peer(xp, norm2_g[l], mod_p[4], mod_p[3], mod_p[5], wl["peer_wq"][l], wl["peer_keys"][l],
                   wl["peer_u"][l], wl["peer_vt"][l], nbp * tp)
        ctx_t = (p["na_k"].reshape(nbp, tp, NA_HEADS, HEAD_DIM), p["na_v"].reshape(nbp, tp, NA_HEADS, HEAD_DIM),
                 c_new, n_new, m_new,
                 p["sw_k"].reshape(nbp, tp, SW_KV_HEADS, HEAD_DIM), p["sw_v"].reshape(nbp, tp, SW_KV_HEADS, HEAD_DIM),
                 ckv_n.reshape(nbp, tp, MLA_KV_RANK), p["krope"][:, KROPE_SLICE].reshape(nbp, tp, MLA_ROPE))
        for i, a in enumerate(ctx_t):
            per_layer[i].append(a)

        p, qm = _project(xs, wl, l, norm1_g[l], mod_s[1], mod_s[0], ts)
        _, kn, vt = _linear(p["ckv"], wl["w_ukv"][l], [MLA_QK_W, 256], norm_g=mla_kv_norm[l], emit_normed=True,
                            transpose_last=True, tm=512, name="mla_kv_up_t")
        knc, vtc = _linear(cache_mla_ckv[:, l].reshape(nbs * lc, MLA_KV_RANK), wl["w_ukv"][l], [MLA_QK_W, 256],
                           transpose_last=True, tm=512, name="mla_kv_up_cache")
        o_na = _na_attention(p["na_q"], p["na_k"], p["na_v"], cache_na_k[:, l].reshape(nbs, lc, BRANCH_W),
                             cache_na_v[:, l].reshape(nbs, lc, BRANCH_W), na_bias[l], nbs, ts)
        o_sw = _sw_attention(p["sw_q"], p["sw_k"], p["sw_v"],
                             cache_swa_k[:, l].reshape(nbs, lc, SW_KV_HEADS * HEAD_DIM),
                             cache_swa_v[:, l].reshape(nbs, lc, SW_KV_HEADS * HEAD_DIM), sink, tabs_swq, tabs_swk,
                             nbs, ts)
        o_mla = _mla_attention(qm, kn, p["krope"], vt, knc, krope_c[:, l], vtc, tabs_mlaq, tabs_mlak, nbs, ts)
        hf, hb, _, _, _ = _mlstm(p["ml_q"], p["ml_k"], p["ml_v"], p["ml_if"], state_mlstm_C[:, l],
                                 state_mlstm_n[:, l], state_mlstm_m[:, l], nbs, ts)
        xs = _merge(o_na, hf, hb, p["ml_o"], o_sw, o_mla, p["gate"], wl["w_branch"][l], wl["w_out"][l], xs,
                    mod_s[2], ts)
        xs = _peer(xs, norm2_g[l], mod_s[4], mod_s[3], mod_s[5], wl["peer_wq"][l], wl["peer_keys"][l],
                   wl["peer_u"][l], wl["peer_vt"][l], ts)

    y_prompt = _final_norm(xp, final_norm_g).reshape(nbp, tp, d)
    y_sample = _final_norm(xs, final_norm_g).reshape(nbs, ts, d)
    return (y_prompt, y_sample) + tuple(jnp.stack(s, axis=1) for s in per_layer)
```

```python
import functools

import numpy as np
import jax
import jax.numpy as jnp
from jax import lax
from jax.experimental import pallas as pl
from jax.experimental.pallas import tpu as pltpu

F32 = jnp.float32
BF16 = jnp.bfloat16

D_MODEL = 1024
GRID_W = 64
HEAD_DIM = 64
N_BRANCH = 4
BRANCH_W = 256
NA_HEADS = 4
NA_KR_MAX = 8
NA_KC = 16
ML_HEADS = 4
ML_CHUNK = 128
SW_HEADS = 4
SW_KV_HEADS = 2
SW_GROUP = SW_HEADS // SW_KV_HEADS
SW_WINDOW = 128
MLA_HEADS = 4
MLA_Q_RANK = 192
MLA_KV_RANK = 128
MLA_NOPE = 64
MLA_ROPE = 32
MLA_V = 64
MLA_SCALE = (MLA_NOPE + MLA_ROPE) ** -0.5
PEER_HEADS = 8
PEER_KEY_DIM = 64
PEER_N_KEYS = 128
PEER_TOPK = 16
PEER_EC = 512
ROPE_BASE = 10000.0
RMS_EPS = 1e-6
NEG = -1e30
IN_SPLITS = (256, 256, 256, 256, 256, 256, 256, 8, 8, 256, 128, 128, 192, 128, 32, 4096)

LANE = 128
MLA_QK_W = MLA_HEADS * LANE

SEGS_AB = (("na_q", 256), ("na_k", 256), ("na_v", 256), ("ml_q", 512), ("ml_k", 512), ("ml_v", 512),
           ("ml_o", 256), ("ml_if", 128), ("sw_q", 256), ("sw_k", 128), ("sw_v", 128), ("cq", 256),
           ("ckv", 128), ("krope", 128))
KROPE_SLICE = slice(MLA_NOPE, MLA_NOPE + MLA_ROPE)


def _cp(sem, vmem_mb=None):
    kw = dict(dimension_semantics=sem)
    if vmem_mb is not None:
        kw["vmem_limit_bytes"] = vmem_mb * 1024 * 1024
    return pltpu.CompilerParams(**kw)


def _dot(a, b):
    return jnp.dot(a, b, preferred_element_type=F32)


def _dot_nt(a, b):
    return lax.dot_general(a, b, (((1,), (1,)), ((), ())), preferred_element_type=F32)


def _rms(x, inv_n):
    return x * lax.rsqrt(jnp.sum(x * x, axis=-1, keepdims=True) * inv_n + RMS_EPS)


def _sigmoid(x):
    return 1.0 / (1.0 + jnp.exp(-x))


def _mod_kernel(c_ref, w_ref, b_ref, o_ref):
    c = c_ref[...]
    s = c * _sigmoid(c)
    o_ref[...] = _dot(s.astype(BF16), w_ref[...].astype(BF16)) + b_ref[...]


def _modulation(cond, w_mod, b_mod):
    nl, d, n = w_mod.shape
    r = cond.shape[0]
    tn = 1024
    return pl.pallas_call(
        _mod_kernel,
        grid=(nl, n // tn),
        in_specs=[pl.BlockSpec((r, d), lambda l, j: (0, 0)),
                  pl.BlockSpec((None, d, tn), lambda l, j: (l, 0, j)),
                  pl.BlockSpec((None, 1, tn), lambda l, j: (l, 0, j))],
        out_specs=pl.BlockSpec((None, r, tn), lambda l, j: (l, 0, j)),
        out_shape=jax.ShapeDtypeStruct((nl, r, n), F32),
        compiler_params=_cp(("parallel", "parallel")),
        name="adaln_modulation",
    )(cond, w_mod, b_mod.reshape(nl, 1, n))


def _linear_kernel(*refs, segs, has_norm, has_mod, has_bias, emit_normed, transpose_last, inv_k):
    it = iter(refs)
    x_ref = next(it)
    g_ref = next(it) if has_norm else None
    sc_ref = next(it) if has_mod else None
    sh_ref = next(it) if has_mod else None
    w_ref = next(it)
    b_ref = next(it) if has_bias else None
    outs = list(it)
    x = x_ref[...]
    if has_norm:
        x = _rms(x, inv_k) * g_ref[...]
    if has_mod:
        x = x * (1.0 + sc_ref[0]) + sh_ref[0]
    if emit_normed:
        outs[0][...] = x
        outs = outs[1:]
    xb = x.astype(BF16)
    for idx, ((s, wd), o) in enumerate(zip(segs, outs)):
        y = _dot(xb, w_ref[:, s:s + wd])
        if has_bias:
            y = y + b_ref[:, s:s + wd]
        o[...] = y.T if (transpose_last and idx == len(segs) - 1) else y


def _linear(x, w, widths, *, bias=None, norm_g=None, k_valid=None, mod=None, rows_per_mod=None,
            emit_normed=False, transpose_last=False, tm=256, vmem_mb=None, name="linear"):
    n, k = x.shape
    tm = min(tm, n)
    segs, s = [], 0
    for wd in widths:
        segs.append((s, wd))
        s += wd
    assert s == w.shape[1] and n % tm == 0
    args = [x]
    in_specs = [pl.BlockSpec((tm, k), lambda i: (i, 0))]
    if norm_g is not None:
        args.append(norm_g.reshape(1, k))
        in_specs.append(pl.BlockSpec((1, k), lambda i: (0, 0)))
    if mod is not None:
        assert rows_per_mod % tm == 0
        for m in mod:
            args.append(m)
            in_specs.append(pl.BlockSpec((1, 1, k), lambda i: ((i * tm) // rows_per_mod, 0, 0)))
    args.append(w)
    in_specs.append(pl.BlockSpec(w.shape, lambda i: (0, 0)))
    if bias is not None:
        args.append(bias.reshape(1, -1))
        in_specs.append(pl.BlockSpec((1, w.shape[1]), lambda i: (0, 0)))
    out_shapes, out_specs = [], []
    if emit_normed:
        out_shapes.append(jax.ShapeDtypeStruct((n, k), F32))
        out_specs.append(pl.BlockSpec((tm, k), lambda i: (i, 0)))
    for idx, wd in enumerate(widths):
        if transpose_last and idx == len(widths) - 1:
            out_shapes.append(jax.ShapeDtypeStruct((wd, n), F32))
            out_specs.append(pl.BlockSpec((wd, tm), lambda i: (0, i)))
        else:
            out_shapes.append(jax.ShapeDtypeStruct((n, wd), F32))
            out_specs.append(pl.BlockSpec((tm, wd), lambda i: (i, 0)))
    kern = functools.partial(_linear_kernel, segs=tuple(segs), has_norm=norm_g is not None,
                             has_mod=mod is not None, has_bias=bias is not None, emit_normed=emit_normed,
                             transpose_last=transpose_last, inv_k=1.0 / (k_valid or k))
    return pl.pallas_call(kern, grid=(n // tm,), in_specs=in_specs, out_specs=out_specs,
                          out_shape=out_shapes, compiler_params=_cp(("parallel",), vmem_mb), name=name)(*args)


def _merge_kernel(ona, hf, hb, mlo, osw, omla, gate, wb, wo, x, g1, bd, o):
    h = hf[...] + hb[...]
    hsq = h * h
    hi = hsq.astype(BF16)
    lo = (hsq - hi.astype(F32)).astype(BF16)
    ms = (_dot(hi, bd[...]) + _dot(lo, bd[...])) * (1.0 / HEAD_DIM)
    oml = h * lax.rsqrt(ms + RMS_EPS) * _sigmoid(mlo[...])
    branches = (ona[...], oml, osw[...], omla[...])
    acc = None
    for n in range(N_BRANCH):
        y = _sigmoid(gate[:, n * D_MODEL:(n + 1) * D_MODEL]) * _dot(branches[n].astype(BF16), wb[n])
        acc = y if acc is None else acc + y
    o[...] = x[...] + g1[0] * _dot(acc.astype(BF16), wo[...])


def _merge(ona, hf, hb, mlo, osw, omla, gate, wb, wo, x, g1, rows_per_mod, tm=256):
    n = x.shape[0]
    tm = min(tm, n)
    bd = jnp.asarray(np.kron(np.eye(BRANCH_W // HEAD_DIM), np.ones((HEAD_DIM, HEAD_DIM))), BF16)
    row = lambda wd: pl.BlockSpec((tm, wd), lambda i: (i, 0))
    return pl.pallas_call(
        _merge_kernel,
        grid=(n // tm,),
        in_specs=[row(BRANCH_W)] * 6 + [row(N_BRANCH * D_MODEL),
                  pl.BlockSpec(wb.shape, lambda i: (0, 0, 0)),
                  pl.BlockSpec(wo.shape, lambda i: (0, 0)),
                  row(D_MODEL),
                  pl.BlockSpec((1, 1, D_MODEL), lambda i: ((i * tm) // rows_per_mod, 0, 0)),
                  pl.BlockSpec(bd.shape, lambda i: (0, 0))],
        out_specs=row(D_MODEL),
        out_shape=jax.ShapeDtypeStruct((n, D_MODEL), F32),
        compiler_params=_cp(("parallel",), 48),
        name="branch_merge",
    )(ona, hf, hb, mlo, osw, omla, gate, wb, wo, x, g1, bd)


def _top_rows(s, k):
    cur, rows = s, []
    for _ in range(k):
        mx = jnp.max(cur, axis=0, keepdims=True)
        rows.append(mx)
        cur = jnp.where(cur == mx, NEG, cur)
    return rows


def _peer_select(s1, s2):
    rows8 = lax.broadcasted_iota(jnp.int32, (8, 1), 0)
    a1, a2 = _top_rows(s1, PEER_TOPK), _top_rows(s2, PEER_TOPK)
    a1_16 = jnp.concatenate(a1, axis=0)
    a1_8 = a1_16[:8]
    cands = [a1_16 + a2[0], a1_8 + a2[1]]
    for k2 in range(2, 8):
        cands.append(jnp.where(rows8 < PEER_TOPK // (k2 + 1), a1_8 + a2[k2], NEG))
    cands.append(a1[0] + jnp.concatenate(a2[8:], axis=0))
    cand = jnp.concatenate(cands, axis=0)
    top = _top_rows(cand, PEER_TOPK + 1)
    tau = 0.5 * (top[PEER_TOPK - 1] + top[PEER_TOPK])
    z = jnp.sum(jnp.where(cand >= tau, jnp.exp(cand - top[0]), 0.0), axis=0, keepdims=True)
    return tau - s1, jnp.exp(s1 - a1[0]), jnp.exp(s2 - a2[0]) * (0.5 / z)


def _peer_kernel(x_ref, ng_ref, sc_ref, sh_ref, g2_ref, wq_ref, keys_ref, u_ref, vt_ref, o_ref,
                 h2t_ref, acc_ref, t1_ref, e1_ref, s2_ref, e2_ref, ht0_ref, ht1_ref, z0_ref, z1_ref, *,
                 ec, n_chunks, tm):
    s = pl.program_id(1)
    n_i1 = ec // PEER_N_KEYS
    n_tc = tm // LANE

    @pl.when(s == 0)
    def _():
        x = x_ref[...]
        h2 = _rms(x, 1.0 / D_MODEL) * ng_ref[...] * (1.0 + sc_ref[0]) + sh_ref[0]
        h2t_ref[...] = h2.T.astype(BF16)
        q = _dot(h2.astype(BF16), wq_ref[...])
        qt = q.T.astype(BF16)
        for h in range(PEER_HEADS):
            r0 = h * 2 * PEER_KEY_DIM
            s1 = _dot(keys_ref[h, 0], qt[r0:r0 + PEER_KEY_DIM, :])
            s2 = _dot(keys_ref[h, 1], qt[r0 + PEER_KEY_DIM:r0 + 2 * PEER_KEY_DIM, :])
            for tc in range(n_tc):
                cols = slice(tc * LANE, (tc + 1) * LANE)
                t1, e1, e2 = _peer_select(s1[:, cols], s2[:, cols])
                t1_ref[h, :, cols] = t1
                e1_ref[h, :, cols] = e1
                s2_ref[h, tc] = s2[:, cols]
                e2_ref[h, tc] = e2
        for r in (ht0_ref, ht1_ref, z0_ref, z1_ref, acc_ref):
            r[...] = jnp.zeros_like(r)

    cb = jnp.clip(s - 1, 0, n_chunks - 1)

    def stages(ht_w, ht_r, z_w, z_r):
        orows = D_MODEL // n_i1
        half = n_tc // 2

        def up_piece(j, nh):
            rows = slice(j * PEER_N_KEYS, (j + 1) * PEER_N_KEYS)
            hj = _dot(u_ref[rows, :], h2t_ref[:, nh * half * LANE:(nh + 1) * half * LANE])
            for k in range(half):
                ht_w[nh * half + k, rows, :] = hj[:, k * LANE:(k + 1) * LANE]

        def down_piece(j, nh):
            osl = slice(j * orows, (j + 1) * orows)
            zh = jnp.concatenate([z_r[nh * half + k] for k in range(half)], axis=1)
            acc_ref[osl, nh * half * LANE:(nh + 1) * half * LANE] += _dot(vt_ref[osl, :], zh)

        pieces = [(fn, j, nh) for j in range(n_i1) for nh in (0, 1) for fn in (up_piece, down_piece)]
        th_rows = [[t1_ref[h, pl.ds(cb * n_i1 + j, 1), :] for h in range(PEER_HEADS)] for j in range(n_i1)]
        e1_rows = [[e1_ref[h, pl.ds(cb * n_i1 + j, 1), :] for h in range(PEER_HEADS)] for j in range(n_i1)]
        n_rh = 4
        rh_rows = PEER_N_KEYS // n_rh
        n_units = n_tc * n_rh
        for unit in range(n_units):
            for fn, j, nh in pieces[unit * len(pieces) // n_units:(unit + 1) * len(pieces) // n_units]:
                fn(j, nh)
            tc, rh = divmod(unit, n_rh)
            cols = slice(tc * LANE, (tc + 1) * LANE)
            r2 = slice(rh * rh_rows, (rh + 1) * rh_rows)
            ws = [None] * n_i1
            for h in range(PEER_HEADS):
                s2t, e2t = s2_ref[h, tc, r2, :], e2_ref[h, tc, r2, :]
                for j in range(n_i1):
                    term = jnp.where(s2t >= th_rows[j][h][:, cols], e2t, 0.0) * e1_rows[j][h][:, cols]
                    ws[j] = term if ws[j] is None else ws[j] + term
            for j in range(n_i1):
                rows = slice(j * PEER_N_KEYS + rh * rh_rows, j * PEER_N_KEYS + (rh + 1) * rh_rows)
                xh = ht_r[tc, rows, :]
                t = jnp.tanh(xh * (0.7978845608028654 + 0.035677408136300125 * (xh * xh)))
                z_w[tc, rows, :] = (ws[j] * (xh + xh * t)).astype(BF16)

    @pl.when(s % 2 == 0)
    def _():
        stages(ht0_ref, ht1_ref, z1_ref, z0_ref)

    @pl.when(s % 2 == 1)
    def _():
        stages(ht1_ref, ht0_ref, z0_ref, z1_ref)

    @pl.when(s == n_chunks + 1)
    def _():
        o_ref[...] = x_ref[...] + g2_ref[0] * acc_ref[...].T


def _peer(x, ng, sc, sh, g2, wq, keys, u, vt, rows_per_mod, tm=1024):
    n = x.shape[0]
    tm = min(tm, n)
    n_chunks, _, ec = vt.shape
    last = n_chunks - 1
    once = pl.Buffered(1)
    modspec = pl.BlockSpec((1, 1, D_MODEL), lambda i, s: ((i * tm) // rows_per_mod, 0, 0))
    kern = functools.partial(_peer_kernel, ec=ec, n_chunks=n_chunks, tm=tm)
    head_buf = pltpu.VMEM((PEER_HEADS, PEER_N_KEYS, tm), F32)
    tile_buf = pltpu.VMEM((PEER_HEADS, tm // LANE, PEER_N_KEYS, LANE), F32)
    return pl.pallas_call(
        kern,
        grid=(n // tm, n_chunks + 2),
        in_specs=[pl.BlockSpec((tm, D_MODEL), lambda i, s: (i, 0), pipeline_mode=once),
                  pl.BlockSpec((1, D_MODEL), lambda i, s: (0, 0)),
                  modspec, modspec, modspec,
                  pl.BlockSpec(wq.shape, lambda i, s: (0, 0), pipeline_mode=once),
                  pl.BlockSpec(keys.shape, lambda i, s: (0, 0, 0, 0), pipeline_mode=once),
                  pl.BlockSpec((ec, D_MODEL), lambda i, s: (jnp.minimum(s, last), 0)),
                  pl.BlockSpec((None, D_MODEL, ec), lambda i, s: (jnp.clip(s - 2, 0, last), 0, 0))],
        out_specs=pl.BlockSpec((tm, D_MODEL), lambda i, s: (i, 0)),
        out_shape=jax.ShapeDtypeStruct((n, D_MODEL), F32),
        scratch_shapes=[pltpu.VMEM((D_MODEL, tm), BF16), pltpu.VMEM((D_MODEL, tm), F32),
                        head_buf, head_buf, tile_buf, tile_buf,
                        pltpu.VMEM((tm // LANE, ec, LANE), F32), pltpu.VMEM((tm // LANE, ec, LANE), F32),
                        pltpu.VMEM((tm // LANE, ec, LANE), BF16), pltpu.VMEM((tm // LANE, ec, LANE), BF16)],
        compiler_params=_cp(("parallel", "arbitrary"), 56),
        name="peer_ffn",
    )(x, ng.reshape(1, D_MODEL), sc, sh, g2, wq, keys, u, vt)


def _final_norm_kernel(x_ref, g_ref, o_ref):
    o_ref[...] = _rms(x_ref[...], 1.0 / D_MODEL) * g_ref[...]


def _final_norm(x, g, tm=512):
    n = x.shape[0]
    tm = min(tm, n)
    return pl.pallas_call(
        _final_norm_kernel, grid=(n // tm,),
        in_specs=[pl.BlockSpec((tm, D_MODEL), lambda i: (i, 0)), pl.BlockSpec((1, D_MODEL), lambda i: (0, 0))],
        out_specs=pl.BlockSpec((tm, D_MODEL), lambda i: (i, 0)),
        out_shape=jax.ShapeDtypeStruct((n, D_MODEL), F32),
        compiler_params=_cp(("parallel",)), name="final_norm")(x, g.reshape(1, D_MODEL))


def _rope_tables(t, n_heads, dh, lead=0, width=None):
    width = width or dh
    pos = jnp.arange(t)
    sec, half = dh // 2, dh // 4
    freqs = ROPE_BASE ** (-jnp.arange(half, dtype=F32) / half)
    first = jnp.asarray((np.arange(dh) % sec) < half)
    ang_r = (pos // GRID_W).astype(F32)[:, None] * freqs[None, :]
    ang_c = (pos % GRID_W).astype(F32)[:, None] * freqs[None, :]
    ang = jnp.concatenate([ang_r, ang_r, ang_c, ang_c], axis=1)
    cos, sin = jnp.cos(ang), jnp.sin(ang)
    sa = jnp.where(first[None, :], -sin, 0.0)
    sb = jnp.where(first[None, :], 0.0, sin)
    padw = ((0, 0), (lead, width - dh - lead))
    place = lambda a, fill: jnp.tile(jnp.pad(a, padw, constant_values=fill), (1, n_heads))
    return place(cos, 1.0), place(sa, 0.0), place(sb, 0.0)


def _rope(x, cos, sa, sb, half):
    w = x.shape[-1]
    return x * cos + pltpu.roll(x, w - half, 1) * sa + pltpu.roll(x, half, 1) * sb


def _softmax_pv(blocks, sink=None):
    m = None
    for s, _ in blocks:
        bm = jnp.max(s, axis=1, keepdims=True)
        m = bm if m is None else jnp.maximum(m, bm)
    if sink is not None:
        m = jnp.maximum(m, sink)
    l, acc = None, None
    for s, v in blocks:
        p = jnp.exp(s - m)
        bl = jnp.sum(p, axis=1, keepdims=True)
        pv = _dot(p.astype(BF16), v)
        l = bl if l is None else l + bl
        acc = pv if acc is None else acc + pv
    if sink is not None:
        l = l + jnp.exp(sink - m)
    return acc / l


def _ctx_attn_kernel(naq, nak, nav, swq, swk, swv, qm, kn, kr, vm, sink_ref, o_na, o_sw, o_mla):
    hd = HEAD_DIM
    scale = hd ** -0.5
    q, k, v = naq[...].astype(BF16), nak[...].astype(BF16), nav[...].astype(BF16)
    for h in range(NA_HEADS):
        hs = slice(h * hd, (h + 1) * hd)
        o_na[:, hs] = _softmax_pv([(_dot_nt(q[:, hs], k[:, hs]) * scale, v[:, hs])])
    q, k, v = swq[...].astype(BF16), swk[...].astype(BF16), swv[...].astype(BF16)
    for h in range(SW_HEADS):
        g = h // SW_GROUP
        hs, gs = slice(h * hd, (h + 1) * hd), slice(g * hd, (g + 1) * hd)
        o_sw[:, hs] = _softmax_pv([(_dot_nt(q[:, hs], k[:, gs]) * scale, v[:, gs])], sink=sink_ref[h])
    q, v = qm[...].astype(BF16), vm[...].astype(BF16)
    krv = kr[...]
    for h in range(MLA_HEADS):
        ks = slice(h * LANE, (h + 1) * LANE)
        kcat = (kn[:, ks] + krv).astype(BF16)
        s = _dot_nt(q[:, ks], kcat) * MLA_SCALE
        o_mla[:, h * MLA_V:(h + 1) * MLA_V] = _softmax_pv([(s, v[:, h * MLA_V:(h + 1) * MLA_V])])


def _ctx_attention(p, qm, kn, vm, sink, nb, t):
    row = lambda wd: pl.BlockSpec((t, wd), lambda b: (b, 0))
    outs = pl.pallas_call(
        _ctx_attn_kernel, grid=(nb,),
        in_specs=[row(256), row(256), row(256), row(256), row(128), row(128), row(MLA_QK_W), row(MLA_QK_W),
                  row(LANE), row(256), pl.BlockSpec(memory_space=pltpu.SMEM)],
        out_specs=[row(256)] * 3,
        out_shape=[jax.ShapeDtypeStruct((nb * t, BRANCH_W), F32)] * 3,
        compiler_params=_cp(("parallel",)), name="ctx_attention",
    )(p["na_q"], p["na_k"], p["na_v"], p["sw_q"], p["sw_k"], p["sw_v"], qm, kn, p["krope"], vm, sink)
    return outs


NA_TQ = 256


def _na_bias_tables(rpb, rows_n):
    nr, nc = 2 * NA_KR_MAX - 1, 2 * NA_KC - 1
    p_sel = np.zeros((3, 4, 12, nr), np.float32)
    ok_r = np.zeros((3, 4, 12), bool)
    for ti, j in enumerate((0, 1, rows_n // 4 - 1)):
        for a in range(4):
            r = 4 * j + a
            r0 = min(max(r - NA_KR_MAX // 2, 0), rows_n - NA_KR_MAX)
            for e in range(12):
                krow = 4 * (j - 1) + e
                if r0 <= krow < r0 + NA_KR_MAX:
                    ok_r[ti, a, e] = True
                    p_sel[ti, a, e, krow - r + NA_KR_MAX - 1] = 1.0
    c = np.arange(GRID_W)[:, None]
    w = np.arange(GRID_W)[None, :]
    cs = np.clip(c - NA_KC // 2, 0, GRID_W - NA_KC)
    ok_c = (w >= cs) & (w < cs + NA_KC)
    q_sel = np.eye(nc, dtype=np.float32)[np.clip(w - c + NA_KC - 1, 0, nc - 1)]
    hp = lax.Precision.HIGHEST
    m1 = jnp.einsum("taei,lhij->lthaej", jnp.asarray(p_sel), rpb, precision=hp)
    b = jnp.einsum("lthaej,cwj->lthacew", m1, jnp.asarray(q_sel), precision=hp)
    ok = ok_r[:, :, None, :, None] & ok_c[None, None, :, None, :]
    b = jnp.where(jnp.asarray(ok)[None, :, None], b, NEG)
    return b.reshape(rpb.shape[0], 3, NA_HEADS, 4 * GRID_W, 12 * GRID_W)


def _na_kernel(q, k0, k1, k2, v0, v1, v2, kc, vc, bias, o):
    hd = HEAD_DIM
    scale = hd ** -0.5
    qb = q[...].astype(BF16)
    kk = jnp.concatenate([k0[...], k1[...], k2[...]], axis=0).astype(BF16)
    vv = jnp.concatenate([v0[...], v1[...], v2[...]], axis=0).astype(BF16)
    kcb, vcb = kc[...].astype(BF16), vc[...].astype(BF16)
    for h in range(NA_HEADS):
        hs = slice(h * hd, (h + 1) * hd)
        s_nb = _dot_nt(qb[:, hs], kk[:, hs]) * scale + bias[h]
        s_cx = _dot_nt(qb[:, hs], kcb[:, hs]) * scale
        o[:, hs] = _softmax_pv([(s_nb, vv[:, hs]), (s_cx, vcb[:, hs])])


def _na_attention(q, k, v, kc, vc, bias, nb, t):
    nj = t // NA_TQ
    lc = kc.shape[1]
    blk = lambda f: pl.BlockSpec((NA_TQ, BRANCH_W), lambda b, j: (b * nj + f(j), 0))
    prev = lambda j: jnp.maximum(j - 1, 0)
    nxt = lambda j: jnp.minimum(j + 1, nj - 1)
    cur = lambda j: j
    ctx = pl.BlockSpec((None, lc, BRANCH_W), lambda b, j: (b, 0, 0))
    tsel = lambda b, j: (jnp.where(j == 0, 0, jnp.where(j == nj - 1, 2, 1)), 0, 0, 0)
    return pl.pallas_call(
        _na_kernel, grid=(nb, nj),
        in_specs=[blk(cur), blk(prev), blk(cur), blk(nxt), blk(prev), blk(cur), blk(nxt), ctx, ctx,
                  pl.BlockSpec((None,) + bias.shape[1:], tsel)],
        out_specs=blk(cur),
        out_shape=jax.ShapeDtypeStruct((nb * t, BRANCH_W), F32),
        compiler_params=_cp(("parallel", "arbitrary"), 48), name="na_attention",
    )(q, k, k, k, v, v, v, kc, vc, bias)


SW_TQ = 256


def _sw_kernel(q, k0, k1, k2, v0, v1, v2, kc, vc, cq, saq, sbq, ck0, sak0, sbk0, ck1, sak1, sbk1,
               ck2, sak2, sbk2, sink_ref, o, *, t):
    hd = HEAD_DIM
    scale = hd ** -0.5
    half = hd // 4
    j = pl.program_id(1)
    qr = _rope(q[...], cq[...], saq[...], sbq[...], half).astype(BF16)
    kk = jnp.concatenate([_rope(k0[...], ck0[...], sak0[...], sbk0[...], half),
                          _rope(k1[...], ck1[...], sak1[...], sbk1[...], half),
                          _rope(k2[...], ck2[...], sak2[...], sbk2[...], half)], axis=0).astype(BF16)
    vv = jnp.concatenate([v0[...], v1[...], v2[...]], axis=0).astype(BF16)
    kcb, vcb = kc[...].astype(BF16), vc[...].astype(BF16)
    qpos = j * SW_TQ + lax.broadcasted_iota(jnp.int32, (SW_TQ, 3 * SW_TQ), 0)
    kpos = (j - 1) * SW_TQ + lax.broadcasted_iota(jnp.int32, (SW_TQ, 3 * SW_TQ), 1)
    valid = (jnp.abs(qpos - kpos) <= SW_WINDOW) & (kpos >= 0) & (kpos < t)
    for h in range(SW_HEADS):
        g = h // SW_GROUP
        hs, gs = slice(h * hd, (h + 1) * hd), slice(g * hd, (g + 1) * hd)
        s_loc = jnp.where(valid, _dot_nt(qr[:, hs], kk[:, gs]) * scale, NEG)
        s_cx = _dot_nt(qr[:, hs], kcb[:, gs]) * scale
        o[:, hs] = _softmax_pv([(s_loc, vv[:, gs]), (s_cx, vcb[:, gs])], sink=sink_ref[h])


def _sw_attention(q, k, v, kc, vc, sink, tabs_q, tabs_k, nb, t):
    nj = t // SW_TQ
    lc = kc.shape[1]
    kvw = SW_KV_HEADS * HEAD_DIM
    prev = lambda j: jnp.maximum(j - 1, 0)
    nxt = lambda j: jnp.minimum(j + 1, nj - 1)
    cur = lambda j: j
    qblk = pl.BlockSpec((SW_TQ, BRANCH_W), lambda b, j: (b * nj + j, 0))
    kblk = lambda f: pl.BlockSpec((SW_TQ, kvw), lambda b, j: (b * nj + f(j), 0))
    ctx = pl.BlockSpec((None, lc, kvw), lambda b, j: (b, 0, 0))
    tq = pl.BlockSpec((SW_TQ, BRANCH_W), lambda b, j: (j, 0))
    tk = lambda f: pl.BlockSpec((SW_TQ, kvw), lambda b, j: (f(j), 0))
    in_specs = [qblk, kblk(prev), kblk(cur), kblk(nxt), kblk(prev), kblk(cur), kblk(nxt), ctx, ctx, tq, tq, tq]
    args = [q, k, k, k, v, v, v, kc, vc, *tabs_q]
    for f in (prev, cur, nxt):
        in_specs += [tk(f)] * 3
        args += list(tabs_k)
    in_specs.append(pl.BlockSpec(memory_space=pltpu.SMEM))
    args.append(sink)
    return pl.pallas_call(
        functools.partial(_sw_kernel, t=t), grid=(nb, nj), in_specs=in_specs, out_specs=qblk,
        out_shape=jax.ShapeDtypeStruct((nb * t, BRANCH_W), F32),
        compiler_params=_cp(("parallel", "arbitrary"), 48), name="sw_attention",
    )(*args)


MLA_TQ = 512
MLA_TK = 512


def _mla_kernel(qm, kn, kr, vt, knc, krc, vtc, cq, saq, sbq, ck, sak, sbk, o, qt_ref, m_ref, l_ref, acc_ref, *,
                n_lat):
    kj = pl.program_id(2)
    half = MLA_ROPE // 4

    @pl.when(kj == 0)
    def _():
        qt_ref[...] = _rope(qm[...], cq[...], saq[...], sbq[...], half).T.astype(BF16)
        m_ref[...] = jnp.full(m_ref.shape, NEG, F32)
        l_ref[...] = jnp.zeros(l_ref.shape, F32)
        acc_ref[...] = jnp.zeros(acc_ref.shape, F32)

    c2 = MLA_SCALE * 1.4426950408889634

    def step(kcat, vt_b):
        ones = jnp.ones((8, vt_b.shape[1]), BF16)
        for h in range(MLA_HEADS):
            ks = slice(h * LANE, (h + 1) * LANE)
            vs = slice(h * MLA_V, (h + 1) * MLA_V)
            s = _dot(kcat[:, ks], qt_ref[ks, :])
            m_old = m_ref[h]
            m_new = jnp.maximum(m_old, jnp.max(s, axis=0, keepdims=True))
            alpha = jnp.exp2(c2 * (m_old - m_new))
            p = jnp.exp2(c2 * s - c2 * m_new).astype(BF16)
            pv = _dot(jnp.concatenate([vt_b[vs, :], ones], axis=0), p)
            l_ref[h] = alpha * l_ref[h] + pv[MLA_V:MLA_V + 1, :]
            acc_ref[vs, :] = alpha * acc_ref[vs, :] + pv[:MLA_V, :]
            m_ref[h] = m_new

    @pl.when(kj < n_lat)
    def _():
        krr = _rope(kr[...], ck[...], sak[...], sbk[...], half)
        step((kn[...] + jnp.concatenate([krr] * MLA_HEADS, axis=1)).astype(BF16), vt[...].astype(BF16))

    @pl.when(kj == n_lat)
    def _():
        step((knc[...] + jnp.concatenate([krc[...]] * MLA_HEADS, axis=1)).astype(BF16), vtc[...].astype(BF16))
        for h in range(MLA_HEADS):
            vs = slice(h * MLA_V, (h + 1) * MLA_V)
            acc_ref[vs, :] = acc_ref[vs, :] / l_ref[h]
        o[...] = acc_ref[...].T


def _mla_attention(qm, kn, kr, vt, knc, krc, vtc, tabs_q, tabs_k, nb, t):
    nq, n_lat = t // MLA_TQ, t // MLA_TK
    lc = krc.shape[1]
    assert lc == MLA_TK
    kb = lambda b, i, kj: b * n_lat + jnp.minimum(kj, n_lat - 1)
    kblk = lambda wd: pl.BlockSpec((MLA_TK, wd), lambda b, i, kj: (kb(b, i, kj), 0))
    tkb = pl.BlockSpec((MLA_TK, LANE), lambda b, i, kj: (jnp.minimum(kj, n_lat - 1), 0))
    tqb = pl.BlockSpec((MLA_TQ, MLA_QK_W), lambda b, i, kj: (i, 0))
    return pl.pallas_call(
        functools.partial(_mla_kernel, n_lat=n_lat), grid=(nb, nq, n_lat + 1),
        in_specs=[pl.BlockSpec((MLA_TQ, MLA_QK_W), lambda b, i, kj: (b * nq + i, 0)),
                  kblk(MLA_QK_W), kblk(LANE),
                  pl.BlockSpec((BRANCH_W, MLA_TK), lambda b, i, kj: (0, kb(b, i, kj))),
                  pl.BlockSpec((lc, MLA_QK_W), lambda b, i, kj: (b, 0)),
                  pl.BlockSpec((None, lc, LANE), lambda b, i, kj: (b, 0, 0)),
                  pl.BlockSpec((BRANCH_W, lc), lambda b, i, kj: (0, b)),
                  tqb, tqb, tqb, tkb, tkb, tkb],
        out_specs=pl.BlockSpec((MLA_TQ, BRANCH_W), lambda b, i, kj: (b * nq + i, 0)),
        out_shape=jax.ShapeDtypeStruct((nb * t, BRANCH_W), F32),
        scratch_shapes=[pltpu.VMEM((MLA_QK_W, MLA_TQ), BF16), pltpu.VMEM((MLA_HEADS, 1, MLA_TQ), F32),
                        pltpu.VMEM((MLA_HEADS, 1, MLA_TQ), F32), pltpu.VMEM((BRANCH_W, MLA_TQ), F32)],
        compiler_params=_cp(("parallel", "parallel", "arbitrary"), 48), name="mla_attention",
    )(qm, kn, kr, vt, knc, krc, vtc, *tabs_q, *tabs_k)


def _log_sigmoid(x):
    return jnp.minimum(x, 0.0) - jnp.log(1.0 + jnp.exp(-jnp.abs(x)))


def _split3(x):
    hi = x.astype(BF16)
    r1 = x - hi.astype(F32)
    mid = r1.astype(BF16)
    lo = (r1 - mid.astype(F32)).astype(BF16)
    return jnp.concatenate([hi, mid, lo], axis=1)


def _mlstm_dir(d, q_ref, k_ref, v_ref, if_ref, o_ref, c_aug, m_prev, negsel_ref):
    ln = ML_CHUNK
    scale = HEAD_DIM ** -0.5
    t_in = if_ref[...]
    logf = _log_sigmoid(t_in)
    ri = lax.broadcasted_iota(jnp.int32, (ln, ln), 0)
    ci = lax.broadcasted_iota(jnp.int32, (ln, ln), 1)
    mask = (ci <= ri) if d == 0 else (ci >= ri)
    tri = jnp.where(mask, 1.0, 0.0).astype(BF16)
    fs = _split3(logf)
    bcol = _dot(tri, fs[:, :LANE]) + _dot(tri, fs[:, LANE:2 * LANE]) + _dot(tri, fs[:, 2 * LANE:])
    b = pltpu.roll(bcol, LANE - 2 * ML_HEADS, 1)
    a = t_in - b
    row = lax.broadcasted_iota(jnp.int32, (ln, LANE), 0)
    cm, sh = a, 1
    while sh < ln:
        if d == 0:
            cm = jnp.maximum(cm, jnp.where(row >= sh, pltpu.roll(cm, sh, 0), NEG))
        else:
            cm = jnp.maximum(cm, jnp.where(row < ln - sh, pltpu.roll(cm, ln - sh, 0), NEG))
        sh *= 2
    big_m = jnp.maximum(m_prev, cm)
    xs = _split3(big_m - m_prev)
    ys = _split3(b + big_m)
    at = (a - m_prev).T
    last = ln - 1 if d == 0 else 0
    m_last = big_m[last:last + 1, :]
    m_new = b[last:last + 1, :] + m_last
    decay = jnp.exp(m_prev - m_last)
    wt = jnp.exp(a - m_last).T
    q, k, v = q_ref[...], k_ref[...], v_ref[...]
    new_c = []
    for h in range(ML_HEADS):
        c = d * ML_HEADS + h
        hsl = slice(h * LANE, (h + 1) * LANE)
        neg_x = _dot(xs, negsel_ref[c])
        neg_y = _dot(ys, negsel_ref[c])
        w_intra = jnp.exp(jnp.where(mask, at[c:c + 1, :] + neg_x, NEG))
        kh = k[:, hsl] * scale
        qb, kb, vb = q[:, hsl].astype(BF16), kh.astype(BF16), v[:, hsl].astype(BF16)
        s = _dot_nt(qb, kb) * w_intra
        c_prev = c_aug[c]
        nd = _dot(s.astype(BF16), vb) + jnp.exp(neg_x) * _dot(qb, c_prev.astype(BF16))
        den = pltpu.roll(nd, HEAD_DIM, 1)
        hout = nd / jnp.maximum(jnp.abs(den), jnp.exp(neg_y))
        o_ref[:, h * HEAD_DIM:(h + 1) * HEAD_DIM] = hout[:, :HEAD_DIM]
        new_c.append(decay[:, c:c + 1] * c_prev + _dot((kh.T * wt[c:c + 1, :]).astype(BF16), vb))
    return new_c, m_new


def _mlstm_kernel(qf, kf, vf, iff, qb, kb, vb, ifb, c0, m0, negsel, hf, hb, c_out, m_out, c_s, m_s, *, nc):
    c = pl.program_id(1)
    nst = 2 * ML_HEADS

    @pl.when(c == 0)
    def _():
        c_s[...] = c0[...]
        m_s[...] = m0[...]

    c_aug = [c_s[i] for i in range(nst)]
    m_prev = m_s[...]
    cf, mf = _mlstm_dir(0, qf, kf, vf, iff, hf, c_aug, m_prev, negsel)
    cb, mb = _mlstm_dir(1, qb, kb, vb, ifb, hb, c_aug, m_prev, negsel)
    for i, cn in enumerate(cf + cb):
        c_s[i] = cn
    lane = lax.broadcasted_iota(jnp.int32, (1, LANE), 1)
    m_s[...] = jnp.where(lane < ML_HEADS, mf, mb)

    @pl.when(c == nc - 1)
    def _():
        c_out[...] = c_s[...]
        m_out[...] = m_s[...]


def _mlstm(q, k, v, gif, c0, n0, m0, nb, t):
    nc = t // ML_CHUNK
    nst = 2 * ML_HEADS
    hd = HEAD_DIM
    qkw = ML_HEADS * LANE
    c_aug0 = jnp.concatenate([c0.reshape(nb, nst, hd, hd),
                              jnp.broadcast_to(n0.reshape(nb, nst, hd, 1), (nb, nst, hd, hd))], axis=-1)
    c_aug0 = jnp.pad(c_aug0, ((0, 0), (0, 0), (0, LANE - hd), (0, 0)))
    m_row0 = jnp.pad(m0.reshape(nb, 1, nst), ((0, 0), (0, 0), (0, LANE - nst)))
    sel = np.zeros((nst, 3 * LANE, LANE), np.float32)
    for ch in range(nst):
        sel[ch, [ch, LANE + ch, 2 * LANE + ch], :] = -1.0
    fw = lambda wd: pl.BlockSpec((ML_CHUNK, wd), lambda b, c: (b * nc + c, 0))
    bw = lambda wd: pl.BlockSpec((ML_CHUNK, wd), lambda b, c: (b * nc + nc - 1 - c, 0))
    st = lambda shp: pl.BlockSpec((None,) + shp, lambda b, c: (b,) + (0,) * len(shp))
    shapes = ((nst, LANE, LANE), (1, LANE))
    hf, hb, c_fin, m_fin = pl.pallas_call(
        functools.partial(_mlstm_kernel, nc=nc), grid=(nb, nc),
        in_specs=[fw(qkw), fw(qkw), fw(qkw), fw(LANE), bw(qkw), bw(qkw), bw(qkw), bw(LANE)]
                 + [st(s) for s in shapes] + [pl.BlockSpec(sel.shape, lambda b, c: (0, 0, 0))],
        out_specs=[fw(BRANCH_W), bw(BRANCH_W)] + [st(s) for s in shapes],
        out_shape=[jax.ShapeDtypeStruct((nb * t, BRANCH_W), F32)] * 2
                  + [jax.ShapeDtypeStruct((nb,) + s, F32) for s in shapes],
        scratch_shapes=[pltpu.VMEM(s, F32) for s in shapes],
        compiler_params=_cp(("parallel", "arbitrary")), name="mlstm_scan",
    )(q, k, v, gif, q, k, v, gif, c_aug0, m_row0, jnp.asarray(sel, BF16))
    c_new = c_fin[:, :, :hd, :hd].reshape(nb, 2, ML_HEADS, hd, hd)
    n_new = c_fin[:, :, :hd, hd].reshape(nb, 2, ML_HEADS, hd)
    m_new = m_fin[:, 0, :nst].reshape(nb, 2, ML_HEADS)
    return hf, hb, c_new, n_new, m_new


def _prep_weights(w_in, b_in, mla_q_norm, w_uq, w_uk, w_uv, w_branch, w_out, peer_wq, peer_keys, peer_u, peer_v):
    offs = np.cumsum((0,) + IN_SPLITS)
    seg = lambda a, i: a[..., offs[i]:offs[i + 1]]
    padr = lambda a, wd, lead=0: jnp.pad(a, [(0, 0)] * (a.ndim - 1) + [(lead, wd - a.shape[-1] - lead)])

    def headpad(a, fill):
        a4 = a.reshape(a.shape[:-1] + (ML_HEADS, HEAD_DIM))
        a4 = jnp.pad(a4, [(0, 0)] * (a4.ndim - 1) + [(0, LANE - HEAD_DIM)], constant_values=fill)
        return a4.reshape(a.shape[:-1] + (ML_HEADS * LANE,))

    def layout(a, is_bias):
        ml_if = padr(jnp.concatenate([seg(a, 7), seg(a, 8)], axis=-1), LANE)
        ml = [headpad(seg(a, 3), 0.0), headpad(seg(a, 4), 0.0), headpad(seg(a, 5), 1.0 if is_bias else 0.0)]
        parts = [seg(a, i) for i in range(3)] + ml + [seg(a, 6), ml_if, seg(a, 9), seg(a, 10), seg(a, 11),
                                                       padr(seg(a, 12), 256), seg(a, 13),
                                                       padr(seg(a, 14), LANE, KROPE_SLICE.start)]
        return jnp.concatenate(parts, axis=-1)

    nl = w_in.shape[0]
    w_uq_p = padr(w_uq.reshape(nl, MLA_Q_RANK, MLA_HEADS, MLA_NOPE + MLA_ROPE), LANE).reshape(nl, MLA_Q_RANK, -1)
    w_uq_p = jnp.pad(w_uq_p, ((0, 0), (0, 256 - MLA_Q_RANK), (0, 0)))
    w_uk_p = padr(w_uk.reshape(nl, MLA_KV_RANK, MLA_HEADS, MLA_NOPE), LANE).reshape(nl, MLA_KV_RANK, -1)
    return dict(
        w_ab=layout(w_in, False).astype(BF16), b_ab=layout(b_in, True),
        w_c=seg(w_in, 15).astype(BF16), b_c=seg(b_in, 15),
        q_norm=padr(mla_q_norm, 256), w_uq=w_uq_p.astype(BF16),
        w_ukv=jnp.concatenate([w_uk_p, w_uv], axis=-1).astype(BF16),
        w_branch=w_branch.astype(BF16), w_out=w_out.astype(BF16), peer_wq=peer_wq.astype(BF16),
        peer_keys=peer_keys.astype(BF16), peer_u=peer_u.astype(BF16),
        peer_vt=jnp.swapaxes(peer_v.reshape(nl, -1, PEER_EC, peer_v.shape[-1]), 2, 3).astype(BF16),
    )


def _project(x, wl, l, norm_g, sc, sh, rows_per_mod):
    outs = _linear(x, wl["w_ab"][l], [wd for _, wd in SEGS_AB], bias=wl["b_ab"][l], norm_g=norm_g,
                   mod=(sc, sh), rows_per_mod=rows_per_mod, tm=256, vmem_mb=48, name="in_proj_ab")
    p = {name: o for (name, _), o in zip(SEGS_AB, outs)}
    (p["gate"],) = _linear(x, wl["w_c"][l], [N_BRANCH * D_MODEL], bias=wl["b_c"][l], norm_g=norm_g,
                           mod=(sc, sh), rows_per_mod=rows_per_mod, tm=256, vmem_mb=48, name="in_proj_gate")
    (qm,) = _linear(p["cq"], wl["w_uq"][l], [MLA_QK_W], norm_g=wl["q_norm"][l], k_valid=MLA_Q_RANK, tm=512,
                    name="mla_q_up")
    return p, qm


def kernel(x_prompt, x_sample, cache_na_k, cache_na_v, state_mlstm_C, state_mlstm_n, state_mlstm_m,
           cache_swa_k, cache_swa_v, cache_mla_ckv, cache_mla_krope, c, c_ctx, w_mod, b_mod, norm1_g, norm2_g,
           w_in, b_in, na_rpb, sw_sink, mla_q_norm, w_uq, mla_kv_norm, w_uk, w_uv, w_branch, w_out, peer_wq,
           peer_keys, peer_u, peer_v, final_norm_g):
    nbp, tp, d = x_prompt.shape
    nbs, ts, _ = x_sample.shape
    nl = w_in.shape[0]
    lc = cache_na_k.shape[2]
    nst = 2 * ML_HEADS
    wl = _prep_weights(w_in, b_in, mla_q_norm, w_uq, w_uk, w_uv, w_branch, w_out, peer_wq, peer_keys,
                       peer_u, peer_v)

    n_cond = 1 + nbs
    cond = jnp.zeros((16, d), F32).at[0].set(c_ctx).at[1:n_cond].set(c)
    mods = _modulation(cond, w_mod, b_mod).reshape(nl, 16, 6, 1, d)

    tabs_swq = _rope_tables(ts, SW_HEADS, HEAD_DIM)
    tabs_swk = _rope_tables(ts, SW_KV_HEADS, HEAD_DIM)
    tabs_mlaq = _rope_tables(ts, MLA_HEADS, MLA_ROPE, lead=MLA_NOPE, width=LANE)
    tabs_mlak = _rope_tables(ts, 1, MLA_ROPE, lead=MLA_NOPE, width=LANE)
    na_bias = _na_bias_tables(na_rpb, ts // GRID_W)
    krope_c = jnp.pad(cache_mla_krope, ((0, 0), (0, 0), (0, 0), (KROPE_SLICE.start, LANE - KROPE_SLICE.stop)))

    xp = x_prompt.reshape(nbp * tp, d)
    xs = x_sample.reshape(nbs * ts, d)
    zeros_c = jnp.zeros((nbp, 2, ML_HEADS, HEAD_DIM, HEAD_DIM), F32)
    zeros_n = jnp.zeros((nbp, 2, ML_HEADS, HEAD_DIM), F32)
    zeros_m = jnp.zeros((nbp, 2, ML_HEADS), F32)
    per_layer = tuple([] for _ in range(9))

    for l in range(nl):
        mod_p = [mods[l, 0:1, i] for i in range(6)]
        mod_s = [mods[l, 1:n_cond, i] for i in range(6)]
        sink = sw_sink[l]

        p, qm = _project(xp, wl, l, norm1_g[l], mod_p[1], mod_p[0], nbp * tp)
        ckv_n, kn, vm = _linear(p["ckv"], wl["w_ukv"][l], [MLA_QK_W, 256], norm_g=mla_kv_norm[l],
                                emit_normed=True, tm=512, name="mla_kv_up")
        o_na, o_sw, o_mla = _ctx_attention(p, qm, kn, vm, sink, nbp, tp)
        hf, hb, c_new, n_new, m_new = _mlstm(p["ml_q"], p["ml_k"], p["ml_v"], p["ml_if"], zeros_c, zeros_n,
                                             zeros_m, nbp, tp)
        xp = _merge(o_na, hf, hb, p["ml_o"], o_sw, o_mla, p["gate"], wl["w_branch"][l], wl["w_out"][l], xp,
                    mod_p[2], nbp * tp)
        xp = _peer(xp, norm2_g[l], mod_p[4], mod_p[3], mod_p[5], wl["peer_wq"][l], wl["peer_keys"][l],
                   wl["peer_u"][l], wl["peer_vt"][l], nbp * tp)
        ctx_t = (p["na_k"].reshape(nbp, tp, NA_HEADS, HEAD_DIM), p["na_v"].reshape(nbp, tp, NA_HEADS, HEAD_DIM),
                 c_new, n_new, m_new,
                 p["sw_k"].reshape(nbp, tp, SW_KV_HEADS, HEAD_DIM), p["sw_v"].reshape(nbp, tp, SW_KV_HEADS, HEAD_DIM),
                 ckv_n.reshape(nbp, tp, MLA_KV_RANK), p["krope"][:, KROPE_SLICE].reshape(nbp, tp, MLA_ROPE))
        for i, a in enumerate(ctx_t):
            per_layer[i].append(a)

        p, qm = _project(xs, wl, l, norm1_g[l], mod_s[1], mod_s[0], ts)
        _, kn, vt = _linear(p["ckv"], wl["w_ukv"][l], [MLA_QK_W, 256], norm_g=mla_kv_norm[l], emit_normed=True,
                            transpose_last=True, tm=512, name="mla_kv_up_t")
        knc, vtc = _linear(cache_mla_ckv[:, l].reshape(nbs * lc, MLA_KV_RANK), wl["w_ukv"][l], [MLA_QK_W, 256],
                           transpose_last=True, tm=512, name="mla_kv_up_cache")
        o_na = _na_attention(p["na_q"], p["na_k"], p["na_v"], cache_na_k[:, l].reshape(nbs, lc, BRANCH_W),
                             cache_na_v[:, l].reshape(nbs, lc, BRANCH_W), na_bias[l], nbs, ts)
        o_sw = _sw_attention(p["sw_q"], p["sw_k"], p["sw_v"],
                             cache_swa_k[:, l].reshape(nbs, lc, SW_KV_HEADS * HEAD_DIM),
                             cache_swa_v[:, l].reshape(nbs, lc, SW_KV_HEADS * HEAD_DIM), sink, tabs_swq, tabs_swk,
                             nbs, ts)
        o_mla = _mla_attention(qm, kn, p["krope"], vt, knc, krope_c[:, l], vtc, tabs_mlaq, tabs_mlak, nbs, ts)
        hf, hb, _, _, _ = _mlstm(p["ml_q"], p["ml_k"], p["ml_v"], p["ml_if"], state_mlstm_C[:, l],
                                 state_mlstm_n[:, l], state_mlstm_m[:, l], nbs, ts)
        xs = _merge(o_na, hf, hb, p["ml_o"], o_sw, o_mla, p["gate"], wl["w_branch"][l], wl["w_out"][l], xs,
                    mod_s[2], ts)
        xs = _peer(xs, norm2_g[l], mod_s[4], mod_s[3], mod_s[5], wl["peer_wq"][l], wl["peer_keys"][l],
                   wl["peer_u"][l], wl["peer_vt"][l], ts)

    y_prompt = _final_norm(xp, final_norm_g).reshape(nbp, tp, d)
    y_sample = _final_norm(xs, final_norm_g).reshape(nbs, ts, d)
    return (y_prompt, y_sample) + tuple(jnp.stack(s, axis=1) for s in per_layer)
```

```python
import functools

import numpy as np
import jax
import jax.numpy as jnp
from jax import lax
from jax.experimental import pallas as pl
from jax.experimental.pallas import tpu as pltpu

F32 = jnp.float32
BF16 = jnp.bfloat16

D_MODEL = 1024
GRID_W = 64
HEAD_DIM = 64
N_BRANCH = 4
BRANCH_W = 256
NA_HEADS = 4
NA_KR_MAX = 8
NA_KC = 16
ML_HEADS = 4
ML_CHUNK = 128
SW_HEADS = 4
SW_KV_HEADS = 2
SW_GROUP = SW_HEADS // SW_KV_HEADS
SW_WINDOW = 128
MLA_HEADS = 4
MLA_Q_RANK = 192
MLA_KV_RANK = 128
MLA_NOPE = 64
MLA_ROPE = 32
MLA_V = 64
MLA_SCALE = (MLA_NOPE + MLA_ROPE) ** -0.5
PEER_HEADS = 8
PEER_KEY_DIM = 64
PEER_N_KEYS = 128
PEER_TOPK = 16
PEER_EC = 512
ROPE_BASE = 10000.0
RMS_EPS = 1e-6
NEG = -1e30
IN_SPLITS = (256, 256, 256, 256, 256, 256, 256, 8, 8, 256, 128, 128, 192, 128, 32, 4096)

LANE = 128
MLA_QK_W = MLA_HEADS * LANE

SEGS_AB = (("na_q", 256), ("na_k", 256), ("na_v", 256), ("ml_q", 512), ("ml_k", 512), ("ml_v", 512),
           ("ml_o", 256), ("ml_if", 128), ("sw_q", 256), ("sw_k", 128), ("sw_v", 128), ("cq", 256),
           ("ckv", 128), ("krope", 128))
KROPE_SLICE = slice(MLA_NOPE, MLA_NOPE + MLA_ROPE)


def _cp(sem, vmem_mb=None):
    kw = dict(dimension_semantics=sem)
    if vmem_mb is not None:
        kw["vmem_limit_bytes"] = vmem_mb * 1024 * 1024
    return pltpu.CompilerParams(**kw)


def _dot(a, b):
    return jnp.dot(a, b, preferred_element_type=F32)


def _dot_nt(a, b):
    return lax.dot_general(a, b, (((1,), (1,)), ((), ())), preferred_element_type=F32)


def _rms(x, inv_n):
    return x * lax.rsqrt(jnp.sum(x * x, axis=-1, keepdims=True) * inv_n + RMS_EPS)


def _sigmoid(x):
    return 1.0 / (1.0 + jnp.exp(-x))


def _mod_kernel(c_ref, w_ref, b_ref, o_ref):
    c = c_ref[...]
    s = c * _sigmoid(c)
    o_ref[...] = _dot(s.astype(BF16), w_ref[...].astype(BF16)) + b_ref[...]


def _modulation(cond, w_mod, b_mod):
    nl, d, n = w_mod.shape
    r = cond.shape[0]
    tn = 1024
    return pl.pallas_call(
        _mod_kernel,
        grid=(nl, n // tn),
        in_specs=[pl.BlockSpec((r, d), lambda l, j: (0, 0)),
                  pl.BlockSpec((None, d, tn), lambda l, j: (l, 0, j)),
                  pl.BlockSpec((None, 1, tn), lambda l, j: (l, 0, j))],
        out_specs=pl.BlockSpec((None, r, tn), lambda l, j: (l, 0, j)),
        out_shape=jax.ShapeDtypeStruct((nl, r, n), F32),
        compiler_params=_cp(("parallel", "parallel")),
        name="adaln_modulation",
    )(cond, w_mod, b_mod.reshape(nl, 1, n))


def _linear_kernel(*refs, segs, has_norm, has_mod, has_bias, emit_normed, transpose_last, inv_k):
    it = iter(refs)
    x_ref = next(it)
    g_ref = next(it) if has_norm else None
    sc_ref = next(it) if has_mod else None
    sh_ref = next(it) if has_mod else None
    w_ref = next(it)
    b_ref = next(it) if has_bias else None
    outs = list(it)
    x = x_ref[...]
    if has_norm:
        x = _rms(x, inv_k) * g_ref[...]
    if has_mod:
        x = x * (1.0 + sc_ref[0]) + sh_ref[0]
    if emit_normed:
        outs[0][...] = x
        outs = outs[1:]
    xb = x.astype(BF16)
    for idx, ((s, wd), o) in enumerate(zip(segs, outs)):
        y = _dot(xb, w_ref[:, s:s + wd])
        if has_bias:
            y = y + b_ref[:, s:s + wd]
        o[...] = y.T if (transpose_last and idx == len(segs) - 1) else y


def _linear(x, w, widths, *, bias=None, norm_g=None, k_valid=None, mod=None, rows_per_mod=None,
            emit_normed=False, transpose_last=False, tm=256, vmem_mb=None, name="linear"):
    n, k = x.shape
    tm = min(tm, n)
    segs, s = [], 0
    for wd in widths:
        segs.append((s, wd))
        s += wd
    assert s == w.shape[1] and n % tm == 0
    args = [x]
    in_specs = [pl.BlockSpec((tm, k), lambda i: (i, 0))]
    if norm_g is not None:
        args.append(norm_g.reshape(1, k))
        in_specs.append(pl.BlockSpec((1, k), lambda i: (0, 0)))
    if mod is not None:
        assert rows_per_mod % tm == 0
        for m in mod:
            args.append(m)
            in_specs.append(pl.BlockSpec((1, 1, k), lambda i: ((i * tm) // rows_per_mod, 0, 0)))
    args.append(w)
    in_specs.append(pl.BlockSpec(w.shape, lambda i: (0, 0)))
    if bias is not None:
        args.append(bias.reshape(1, -1))
        in_specs.append(pl.BlockSpec((1, w.shape[1]), lambda i: (0, 0)))
    out_shapes, out_specs = [], []
    if emit_normed:
        out_shapes.append(jax.ShapeDtypeStruct((n, k), F32))
        out_specs.append(pl.BlockSpec((tm, k), lambda i: (i, 0)))
    for idx, wd in enumerate(widths):
        if transpose_last and idx == len(widths) - 1:
            out_shapes.append(jax.ShapeDtypeStruct((wd, n), F32))
            out_specs.append(pl.BlockSpec((wd, tm), lambda i: (0, i)))
        else:
            out_shapes.append(jax.ShapeDtypeStruct((n, wd), F32))
            out_specs.append(pl.BlockSpec((tm, wd), lambda i: (i, 0)))
    kern = functools.partial(_linear_kernel, segs=tuple(segs), has_norm=norm_g is not None,
                             has_mod=mod is not None, has_bias=bias is not None, emit_normed=emit_normed,
                             transpose_last=transpose_last, inv_k=1.0 / (k_valid or k))
    return pl.pallas_call(kern, grid=(n // tm,), in_specs=in_specs, out_specs=out_specs,
                          out_shape=out_shapes, compiler_params=_cp(("parallel",), vmem_mb), name=name)(*args)


def _merge_kernel(ona, hf, hb, mlo, osw, omla, gate, wb, wo, x, g1, bd, o):
    h = hf[...] + hb[...]
    hsq = h * h
    hi = hsq.astype(BF16)
    lo = (hsq - hi.astype(F32)).astype(BF16)
    ms = (_dot(hi, bd[...]) + _dot(lo, bd[...])) * (1.0 / HEAD_DIM)
    oml = h * lax.rsqrt(ms + RMS_EPS) * _sigmoid(mlo[...])
    branches = (ona[...], oml, osw[...], omla[...])
    acc = None
    for n in range(N_BRANCH):
        y = _sigmoid(gate[:, n * D_MODEL:(n + 1) * D_MODEL]) * _dot(branches[n].astype(BF16), wb[n])
        acc = y if acc is None else acc + y
    o[...] = x[...] + g1[0] * _dot(acc.astype(BF16), wo[...])


def _merge(ona, hf, hb, mlo, osw, omla, gate, wb, wo, x, g1, rows_per_mod, tm=256):
    n = x.shape[0]
    tm = min(tm, n)
    bd = jnp.asarray(np.kron(np.eye(BRANCH_W // HEAD_DIM), np.ones((HEAD_DIM, HEAD_DIM))), BF16)
    row = lambda wd: pl.BlockSpec((tm, wd), lambda i: (i, 0))
    return pl.pallas_call(
        _merge_kernel,
        grid=(n // tm,),
        in_specs=[row(BRANCH_W)] * 6 + [row(N_BRANCH * D_MODEL),
                  pl.BlockSpec(wb.shape, lambda i: (0, 0, 0)),
                  pl.BlockSpec(wo.shape, lambda i: (0, 0)),
                  row(D_MODEL),
                  pl.BlockSpec((1, 1, D_MODEL), lambda i: ((i * tm) // rows_per_mod, 0, 0)),
                  pl.BlockSpec(bd.shape, lambda i: (0, 0))],
        out_specs=row(D_MODEL),
        out_shape=jax.ShapeDtypeStruct((n, D_MODEL), F32),
        compiler_params=_cp(("parallel",), 48),
        name="branch_merge",
    )(ona, hf, hb, mlo, osw, omla, gate, wb, wo, x, g1, bd)


def _top_rows(s, k):
    cur, rows = s, []
    for _ in range(k):
        mx = jnp.max(cur, axis=0, keepdims=True)
        rows.append(mx)
        cur = jnp.where(cur == mx, NEG, cur)
    return rows


def _peer_select(s1, s2):
    rows8 = lax.broadcasted_iota(jnp.int32, (8, 1), 0)
    a1, a2 = _top_rows(s1, PEER_TOPK), _top_rows(s2, PEER_TOPK)
    a1_16 = jnp.concatenate(a1, axis=0)
    a1_8 = a1_16[:8]
    cands = [a1_16 + a2[0], a1_8 + a2[1]]
    for k2 in range(2, 8):
        cands.append(jnp.where(rows8 < PEER_TOPK // (k2 + 1), a1_8 + a2[k2], NEG))
    cands.append(a1[0] + jnp.concatenate(a2[8:], axis=0))
    cand = jnp.concatenate(cands, axis=0)
    top = _top_rows(cand, PEER_TOPK + 1)
    tau = 0.5 * (top[PEER_TOPK - 1] + top[PEER_TOPK])
    z = jnp.sum(jnp.where(cand >= tau, jnp.exp(cand - top[0]), 0.0), axis=0, keepdims=True)
    return tau - s1, jnp.exp(s1 - a1[0]), jnp.exp(s2 - a2[0]) * (0.5 / z)


def _peer_kernel(x_ref, ng_ref, sc_ref, sh_ref, g2_ref, wq_ref, keys_ref, u_ref, vt_ref, o_ref,
                 h2t_ref, qt_ref, acc_ref, t1_ref, e1_ref, s2_ref, e2_ref, ht_ref, z_ref, *,
                 ec, n_chunks, tm):
    s = pl.program_id(1)
    n_i1 = ec // PEER_N_KEYS
    n_tc = tm // LANE
    half = n_tc // 2
    hw = tm // 2

    @pl.when(s == 0)
    def _():
        x = x_ref[...]
        h2 = _rms(x, 1.0 / D_MODEL) * ng_ref[...] * (1.0 + sc_ref[0]) + sh_ref[0]
        h2t = h2.T.astype(BF16)
        for nh in range(2):
            h2t_ref[nh] = h2t[:, nh * hw:(nh + 1) * hw]
        qt_ref[...] = _dot(h2.astype(BF16), wq_ref[...]).T.astype(BF16)

        def head_body(h, carry):
            r0 = pl.multiple_of(h * 2 * PEER_KEY_DIM, 2 * PEER_KEY_DIM)
            r1 = pl.multiple_of(r0 + PEER_KEY_DIM, PEER_KEY_DIM)
            s1 = _dot(keys_ref[h, 0], qt_ref[pl.ds(r0, PEER_KEY_DIM), :])
            s2 = _dot(keys_ref[h, 1], qt_ref[pl.ds(r1, PEER_KEY_DIM), :])
            for tc in range(n_tc):
                cols = slice(tc * LANE, (tc + 1) * LANE)
                t1, e1, e2 = _peer_select(s1[:, cols], s2[:, cols])
                t1_ref[h, tc] = t1
                e1_ref[h, tc] = e1
                s2_ref[h, tc] = s2[:, cols]
                e2_ref[h, tc] = e2
            return carry

        lax.fori_loop(0, PEER_HEADS, head_body, 0)
        for r in (ht_ref, z_ref, acc_ref):
            r[...] = jnp.zeros_like(r)

    cb = jnp.clip(s - 1, 0, n_chunks - 1)
    orows = D_MODEL // n_i1
    n_rh = 4
    rh_rows = PEER_N_KEYS // n_rh
    n_iter = 2 * n_i1
    upi = n_tc * n_rh // n_iter

    sw = s % 2
    sr = 1 - sw
    for i in range(n_iter):
        j, nh = divmod(i, 2)
        rows = slice(j * PEER_N_KEYS, (j + 1) * PEER_N_KEYS)
        hj = _dot(u_ref[rows, :], h2t_ref[nh])
        for k in range(half):
            ht_ref[sw, nh * half + k, rows, :] = hj[:, k * LANE:(k + 1) * LANE]
        osl = slice(j * orows, (j + 1) * orows)
        zh = jnp.concatenate([z_ref[sw, nh * half + k] for k in range(half)], axis=1)
        acc_ref[nh, osl, :] += _dot(vt_ref[osl, :], zh)
        for k in range(upi):
            tc, rh = divmod(i * upi + k, n_rh)
            r2 = slice(rh * rh_rows, (rh + 1) * rh_rows)
            ws = [None] * n_i1
            for h in range(PEER_HEADS):
                s2t, e2t = s2_ref[h, tc, r2, :], e2_ref[h, tc, r2, :]
                for j4 in range(n_i1):
                    th = t1_ref[h, tc, pl.ds(cb * n_i1 + j4, 1), :]
                    e1r = e1_ref[h, tc, pl.ds(cb * n_i1 + j4, 1), :]
                    term = jnp.where(s2t >= th, e2t, 0.0) * e1r
                    ws[j4] = term if ws[j4] is None else ws[j4] + term
            for j4 in range(n_i1):
                rr = slice(j4 * PEER_N_KEYS + rh * rh_rows, j4 * PEER_N_KEYS + (rh + 1) * rh_rows)
                xh = ht_ref[sr, tc, rr, :]
                t = jnp.tanh(xh * (0.7978845608028654 + 0.035677408136300125 * (xh * xh)))
                z_ref[sr, tc, rr, :] = (ws[j4] * (xh + xh * t)).astype(BF16)

    @pl.when(s == n_chunks + 1)
    def _():
        for nh in range(2):
            rs = slice(nh * hw, (nh + 1) * hw)
            o_ref[rs, :] = x_ref[rs, :] + g2_ref[0] * acc_ref[nh].T


def _peer(x, ng, sc, sh, g2, wq, keys, u, vt, rows_per_mod, tm=512):
    n = x.shape[0]
    tm = min(tm, n)
    n_chunks, _, ec = vt.shape
    last = n_chunks - 1
    n_tc = tm // LANE
    once = pl.Buffered(1)
    modspec = pl.BlockSpec((1, 1, D_MODEL), lambda i, s: ((i * tm) // rows_per_mod, 0, 0))
    kern = functools.partial(_peer_kernel, ec=ec, n_chunks=n_chunks, tm=tm)
    tile_buf = pltpu.VMEM((PEER_HEADS, n_tc, PEER_N_KEYS, LANE), F32)
    return pl.pallas_call(
        kern,
        grid=(n // tm, n_chunks + 2),
        in_specs=[pl.BlockSpec((tm, D_MODEL), lambda i, s: (i, 0), pipeline_mode=once),
                  pl.BlockSpec((1, D_MODEL), lambda i, s: (0, 0)),
                  modspec, modspec, modspec,
                  pl.BlockSpec(wq.shape, lambda i, s: (0, 0), pipeline_mode=once),
                  pl.BlockSpec(keys.shape, lambda i, s: (0, 0, 0, 0), pipeline_mode=once),
                  pl.BlockSpec((ec, D_MODEL), lambda i, s: (jnp.minimum(s, last), 0)),
                  pl.BlockSpec((None, D_MODEL, ec), lambda i, s: (jnp.clip(s - 2, 0, last), 0, 0))],
        out_specs=pl.BlockSpec((tm, D_MODEL), lambda i, s: (i, 0)),
        out_shape=jax.ShapeDtypeStruct((n, D_MODEL), F32),
        scratch_shapes=[pltpu.VMEM((2, D_MODEL, tm // 2), BF16), pltpu.VMEM((D_MODEL, tm), BF16),
                        pltpu.VMEM((2, D_MODEL, tm // 2), F32),
                        tile_buf, tile_buf, tile_buf, tile_buf,
                        pltpu.VMEM((2, n_tc, ec, LANE), F32), pltpu.VMEM((2, n_tc, ec, LANE), BF16)],
        compiler_params=_cp(("parallel", "arbitrary"), 56),
        name="peer_ffn",
    )(x, ng.reshape(1, D_MODEL), sc, sh, g2, wq, keys, u, vt)


def _final_norm_kernel(x_ref, g_ref, o_ref):
    o_ref[...] = _rms(x_ref[...], 1.0 / D_MODEL) * g_ref[...]


def _final_norm(x, g, tm=512):
    n = x.shape[0]
    tm = min(tm, n)
    return pl.pallas_call(
        _final_norm_kernel, grid=(n // tm,),
        in_specs=[pl.BlockSpec((tm, D_MODEL), lambda i: (i, 0)), pl.BlockSpec((1, D_MODEL), lambda i: (0, 0))],
        out_specs=pl.BlockSpec((tm, D_MODEL), lambda i: (i, 0)),
        out_shape=jax.ShapeDtypeStruct((n, D_MODEL), F32),
        compiler_params=_cp(("parallel",)), name="final_norm")(x, g.reshape(1, D_MODEL))


def _rope_tables(t, n_heads, dh, lead=0, width=None):
    width = width or dh
    pos = jnp.arange(t)
    sec, half = dh // 2, dh // 4
    freqs = ROPE_BASE ** (-jnp.arange(half, dtype=F32) / half)
    first = jnp.asarray((np.arange(dh) % sec) < half)
    ang_r = (pos // GRID_W).astype(F32)[:, None] * freqs[None, :]
    ang_c = (pos % GRID_W).astype(F32)[:, None] * freqs[None, :]
    ang = jnp.concatenate([ang_r, ang_r, ang_c, ang_c], axis=1)
    cos, sin = jnp.cos(ang), jnp.sin(ang)
    sa = jnp.where(first[None, :], -sin, 0.0)
    sb = jnp.where(first[None, :], 0.0, sin)
    padw = ((0, 0), (lead, width - dh - lead))
    place = lambda a, fill: jnp.tile(jnp.pad(a, padw, constant_values=fill), (1, n_heads))
    return place(cos, 1.0), place(sa, 0.0), place(sb, 0.0)


def _rope(x, cos, sa, sb, half):
    w = x.shape[-1]
    return x * cos + pltpu.roll(x, w - half, 1) * sa + pltpu.roll(x, half, 1) * sb


def _softmax_pv(blocks, sink=None):
    m = None
    for s, _ in blocks:
        bm = jnp.max(s, axis=1, keepdims=True)
        m = bm if m is None else jnp.maximum(m, bm)
    if sink is not None:
        m = jnp.maximum(m, sink)
    l, acc = None, None
    for s, v in blocks:
        p = jnp.exp(s - m)
        bl = jnp.sum(p, axis=1, keepdims=True)
        pv = _dot(p.astype(BF16), v)
        l = bl if l is None else l + bl
        acc = pv if acc is None else acc + pv
    if sink is not None:
        l = l + jnp.exp(sink - m)
    return acc / l


def _ctx_attn_kernel(naq, nak, nav, swq, swk, swv, qm, kn, kr, vm, sink_ref, o_na, o_sw, o_mla):
    hd = HEAD_DIM
    scale = hd ** -0.5
    q, k, v = naq[...].astype(BF16), nak[...].astype(BF16), nav[...].astype(BF16)
    for h in range(NA_HEADS):
        hs = slice(h * hd, (h + 1) * hd)
        o_na[:, hs] = _softmax_pv([(_dot_nt(q[:, hs], k[:, hs]) * scale, v[:, hs])])
    q, k, v = swq[...].astype(BF16), swk[...].astype(BF16), swv[...].astype(BF16)
    for h in range(SW_HEADS):
        g = h // SW_GROUP
        hs, gs = slice(h * hd, (h + 1) * hd), slice(g * hd, (g + 1) * hd)
        o_sw[:, hs] = _softmax_pv([(_dot_nt(q[:, hs], k[:, gs]) * scale, v[:, gs])], sink=sink_ref[h])
    q, v = qm[...].astype(BF16), vm[...].astype(BF16)
    krv = kr[...]
    for h in range(MLA_HEADS):
        ks = slice(h * LANE, (h + 1) * LANE)
        kcat = (kn[:, ks] + krv).astype(BF16)
        s = _dot_nt(q[:, ks], kcat) * MLA_SCALE
        o_mla[:, h * MLA_V:(h + 1) * MLA_V] = _softmax_pv([(s, v[:, h * MLA_V:(h + 1) * MLA_V])])


def _ctx_attention(p, qm, kn, vm, sink, nb, t):
    row = lambda wd: pl.BlockSpec((t, wd), lambda b: (b, 0))
    outs = pl.pallas_call(
        _ctx_attn_kernel, grid=(nb,),
        in_specs=[row(256), row(256), row(256), row(256), row(128), row(128), row(MLA_QK_W), row(MLA_QK_W),
                  row(LANE), row(256), pl.BlockSpec(memory_space=pltpu.SMEM)],
        out_specs=[row(256)] * 3,
        out_shape=[jax.ShapeDtypeStruct((nb * t, BRANCH_W), F32)] * 3,
        compiler_params=_cp(("parallel",)), name="ctx_attention",
    )(p["na_q"], p["na_k"], p["na_v"], p["sw_q"], p["sw_k"], p["sw_v"], qm, kn, p["krope"], vm, sink)
    return outs


NA_TQ = 256


def _na_bias_tables(rpb, rows_n):
    nr, nc = 2 * NA_KR_MAX - 1, 2 * NA_KC - 1
    p_sel = np.zeros((3, 4, 12, nr), np.float32)
    ok_r = np.zeros((3, 4, 12), bool)
    for ti, j in enumerate((0, 1, rows_n // 4 - 1)):
        for a in range(4):
            r = 4 * j + a
            r0 = min(max(r - NA_KR_MAX // 2, 0), rows_n - NA_KR_MAX)
            for e in range(12):
                krow = 4 * (j - 1) + e
                if r0 <= krow < r0 + NA_KR_MAX:
                    ok_r[ti, a, e] = True
                    p_sel[ti, a, e, krow - r + NA_KR_MAX - 1] = 1.0
    c = np.arange(GRID_W)[:, None]
    w = np.arange(GRID_W)[None, :]
    cs = np.clip(c - NA_KC // 2, 0, GRID_W - NA_KC)
    ok_c = (w >= cs) & (w < cs + NA_KC)
    q_sel = np.eye(nc, dtype=np.float32)[np.clip(w - c + NA_KC - 1, 0, nc - 1)]
    hp = lax.Precision.HIGHEST
    m1 = jnp.einsum("taei,lhij->lthaej", jnp.asarray(p_sel), rpb, precision=hp)
    b = jnp.einsum("lthaej,cwj->lthacew", m1, jnp.asarray(q_sel), precision=hp)
    ok = ok_r[:, :, None, :, None] & ok_c[None, None, :, None, :]
    b = jnp.where(jnp.asarray(ok)[None, :, None], b, NEG)
    return b.reshape(rpb.shape[0], 3, NA_HEADS, 4 * GRID_W, 12 * GRID_W)


def _na_kernel(q, k0, k1, k2, v0, v1, v2, kc, vc, bias, o):
    hd = HEAD_DIM
    scale = hd ** -0.5
    qb = q[...].astype(BF16)
    kk = jnp.concatenate([k0[...], k1[...], k2[...]], axis=0).astype(BF16)
    vv = jnp.concatenate([v0[...], v1[...], v2[...]], axis=0).astype(BF16)
    kcb, vcb = kc[...].astype(BF16), vc[...].astype(BF16)
    for h in range(NA_HEADS):
        hs = slice(h * hd, (h + 1) * hd)
        s_nb = _dot_nt(qb[:, hs], kk[:, hs]) * scale + bias[h]
        s_cx = _dot_nt(qb[:, hs], kcb[:, hs]) * scale
        o[:, hs] = _softmax_pv([(s_nb, vv[:, hs]), (s_cx, vcb[:, hs])])


def _na_attention(q, k, v, kc, vc, bias, nb, t):
    nj = t // NA_TQ
    lc = kc.shape[1]
    blk = lambda f: pl.BlockSpec((NA_TQ, BRANCH_W), lambda b, j: (b * nj + f(j), 0))
    prev = lambda j: jnp.maximum(j - 1, 0)
    nxt = lambda j: jnp.minimum(j + 1, nj - 1)
    cur = lambda j: j
    ctx = pl.BlockSpec((None, lc, BRANCH_W), lambda b, j: (b, 0, 0))
    tsel = lambda b, j: (jnp.where(j == 0, 0, jnp.where(j == nj - 1, 2, 1)), 0, 0, 0)
    return pl.pallas_call(
        _na_kernel, grid=(nb, nj),
        in_specs=[blk(cur), blk(prev), blk(cur), blk(nxt), blk(prev), blk(cur), blk(nxt), ctx, ctx,
                  pl.BlockSpec((None,) + bias.shape[1:], tsel)],
        out_specs=blk(cur),
        out_shape=jax.ShapeDtypeStruct((nb * t, BRANCH_W), F32),
        compiler_params=_cp(("parallel", "arbitrary"), 48), name="na_attention",
    )(q, k, k, k, v, v, v, kc, vc, bias)


SW_TQ = 256


def _sw_kernel(q, k0, k1, k2, v0, v1, v2, kc, vc, cq, saq, sbq, ck0, sak0, sbk0, ck1, sak1, sbk1,
               ck2, sak2, sbk2, sink_ref, o, *, t):
    hd = HEAD_DIM
    scale = hd ** -0.5
    half = hd // 4
    j = pl.program_id(1)
    qr = _rope(q[...], cq[...], saq[...], sbq[...], half).astype(BF16)
    kk = jnp.concatenate([_rope(k0[...], ck0[...], sak0[...], sbk0[...], half),
                          _rope(k1[...], ck1[...], sak1[...], sbk1[...], half),
                          _rope(k2[...], ck2[...], sak2[...], sbk2[...], half)], axis=0).astype(BF16)
    vv = jnp.concatenate([v0[...], v1[...], v2[...]], axis=0).astype(BF16)
    kcb, vcb = kc[...].astype(BF16), vc[...].astype(BF16)
    qpos = j * SW_TQ + lax.broadcasted_iota(jnp.int32, (SW_TQ, 3 * SW_TQ), 0)
    kpos = (j - 1) * SW_TQ + lax.broadcasted_iota(jnp.int32, (SW_TQ, 3 * SW_TQ), 1)
    valid = (jnp.abs(qpos - kpos) <= SW_WINDOW) & (kpos >= 0) & (kpos < t)
    for h in range(SW_HEADS):
        g = h // SW_GROUP
        hs, gs = slice(h * hd, (h + 1) * hd), slice(g * hd, (g + 1) * hd)
        s_loc = jnp.where(valid, _dot_nt(qr[:, hs], kk[:, gs]) * scale, NEG)
        s_cx = _dot_nt(qr[:, hs], kcb[:, gs]) * scale
        o[:, hs] = _softmax_pv([(s_loc, vv[:, gs]), (s_cx, vcb[:, gs])], sink=sink_ref[h])


def _sw_attention(q, k, v, kc, vc, sink, tabs_q, tabs_k, nb, t):
    nj = t // SW_TQ
    lc = kc.shape[1]
    kvw = SW_KV_HEADS * HEAD_DIM
    prev = lambda j: jnp.maximum(j - 1, 0)
    nxt = lambda j: jnp.minimum(j + 1, nj - 1)
    cur = lambda j: j
    qblk = pl.BlockSpec((SW_TQ, BRANCH_W), lambda b, j: (b * nj + j, 0))
    kblk = lambda f: pl.BlockSpec((SW_TQ, kvw), lambda b, j: (b * nj + f(j), 0))
    ctx = pl.BlockSpec((None, lc, kvw), lambda b, j: (b, 0, 0))
    tq = pl.BlockSpec((SW_TQ, BRANCH_W), lambda b, j: (j, 0))
    tk = lambda f: pl.BlockSpec((SW_TQ, kvw), lambda b, j: (f(j), 0))
    in_specs = [qblk, kblk(prev), kblk(cur), kblk(nxt), kblk(prev), kblk(cur), kblk(nxt), ctx, ctx, tq, tq, tq]
    args = [q, k, k, k, v, v, v, kc, vc, *tabs_q]
    for f in (prev, cur, nxt):
        in_specs += [tk(f)] * 3
        args += list(tabs_k)
    in_specs.append(pl.BlockSpec(memory_space=pltpu.SMEM))
    args.append(sink)
    return pl.pallas_call(
        functools.partial(_sw_kernel, t=t), grid=(nb, nj), in_specs=in_specs, out_specs=qblk,
        out_shape=jax.ShapeDtypeStruct((nb * t, BRANCH_W), F32),
        compiler_params=_cp(("parallel", "arbitrary"), 48), name="sw_attention",
    )(*args)


MLA_TQ = 512
MLA_TK = 512


def _mla_kernel(qm, kn, kr, vt, knc, krc, vtc, cq, saq, sbq, ck, sak, sbk, o, qt_ref, m_ref, l_ref, acc_ref, *,
                n_lat):
    kj = pl.program_id(2)
    half = MLA_ROPE // 4

    @pl.when(kj == 0)
    def _():
        qt_ref[...] = _rope(qm[...], cq[...], saq[...], sbq[...], half).T.astype(BF16)
        m_ref[...] = jnp.full(m_ref.shape, NEG, F32)
        l_ref[...] = jnp.zeros(l_ref.shape, F32)
        acc_ref[...] = jnp.zeros(acc_ref.shape, F32)

    c2 = MLA_SCALE * 1.4426950408889634

    def step(kcat, vt_b):
        ones = jnp.ones((8, vt_b.shape[1]), BF16)
        for h in range(MLA_HEADS):
            ks = slice(h * LANE, (h + 1) * LANE)
            vs = slice(h * MLA_V, (h + 1) * MLA_V)
            s = _dot(kcat[:, ks], qt_ref[ks, :])
            m_old = m_ref[h]
            m_new = jnp.maximum(m_old, jnp.max(s, axis=0, keepdims=True))
            alpha = jnp.exp2(c2 * (m_old - m_new))
            p = jnp.exp2(c2 * s - c2 * m_new).astype(BF16)
            pv = _dot(jnp.concatenate([vt_b[vs, :], ones], axis=0), p)
            l_ref[h] = alpha * l_ref[h] + pv[MLA_V:MLA_V + 1, :]
            acc_ref[vs, :] = alpha * acc_ref[vs, :] + pv[:MLA_V, :]
            m_ref[h] = m_new

    @pl.when(kj < n_lat)
    def _():
        krr = _rope(kr[...], ck[...], sak[...], sbk[...], half)
        step((kn[...] + jnp.concatenate([krr] * MLA_HEADS, axis=1)).astype(BF16), vt[...].astype(BF16))

    @pl.when(kj == n_lat)
    def _():
        step((knc[...] + jnp.concatenate([krc[...]] * MLA_HEADS, axis=1)).astype(BF16), vtc[...].astype(BF16))
        for h in range(MLA_HEADS):
            vs = slice(h * MLA_V, (h + 1) * MLA_V)
            acc_ref[vs, :] = acc_ref[vs, :] / l_ref[h]
        o[...] = acc_ref[...].T


def _mla_attention(qm, kn, kr, vt, knc, krc, vtc, tabs_q, tabs_k, nb, t):
    nq, n_lat = t // MLA_TQ, t // MLA_TK
    lc = krc.shape[1]
    assert lc == MLA_TK
    kb = lambda b, i, kj: b * n_lat + jnp.minimum(kj, n_lat - 1)
    kblk = lambda wd: pl.BlockSpec((MLA_TK, wd), lambda b, i, kj: (kb(b, i, kj), 0))
    tkb = pl.BlockSpec((MLA_TK, LANE), lambda b, i, kj: (jnp.minimum(kj, n_lat - 1), 0))
    tqb = pl.BlockSpec((MLA_TQ, MLA_QK_W), lambda b, i, kj: (i, 0))
    return pl.pallas_call(
        functools.partial(_mla_kernel, n_lat=n_lat), grid=(nb, nq, n_lat + 1),
        in_specs=[pl.BlockSpec((MLA_TQ, MLA_QK_W), lambda b, i, kj: (b * nq + i, 0)),
                  kblk(MLA_QK_W), kblk(LANE),
                  pl.BlockSpec((BRANCH_W, MLA_TK), lambda b, i, kj: (0, kb(b, i, kj))),
                  pl.BlockSpec((lc, MLA_QK_W), lambda b, i, kj: (b, 0)),
                  pl.BlockSpec((None, lc, LANE), lambda b, i, kj: (b, 0, 0)),
                  pl.BlockSpec((BRANCH_W, lc), lambda b, i, kj: (0, b)),
                  tqb, tqb, tqb, tkb, tkb, tkb],
        out_specs=pl.BlockSpec((MLA_TQ, BRANCH_W), lambda b, i, kj: (b * nq + i, 0)),
        out_shape=jax.ShapeDtypeStruct((nb * t, BRANCH_W), F32),
        scratch_shapes=[pltpu.VMEM((MLA_QK_W, MLA_TQ), BF16), pltpu.VMEM((MLA_HEADS, 1, MLA_TQ), F32),
                        pltpu.VMEM((MLA_HEADS, 1, MLA_TQ), F32), pltpu.VMEM((BRANCH_W, MLA_TQ), F32)],
        compiler_params=_cp(("parallel", "parallel", "arbitrary"), 48), name="mla_attention",
    )(qm, kn, kr, vt, knc, krc, vtc, *tabs_q, *tabs_k)


def _log_sigmoid(x):
    return jnp.minimum(x, 0.0) - jnp.log(1.0 + jnp.exp(-jnp.abs(x)))


def _split3(x):
    hi = x.astype(BF16)
    r1 = x - hi.astype(F32)
    mid = r1.astype(BF16)
    lo = (r1 - mid.astype(F32)).astype(BF16)
    return jnp.concatenate([hi, mid, lo], axis=1)


def _mlstm_dir(d, q_ref, k_ref, v_ref, if_ref, o_ref, c_aug, m_prev, negsel_ref):
    ln = ML_CHUNK
    scale = HEAD_DIM ** -0.5
    t_in = if_ref[...]
    logf = _log_sigmoid(t_in)
    ri = lax.broadcasted_iota(jnp.int32, (ln, ln), 0)
    ci = lax.broadcasted_iota(jnp.int32, (ln, ln), 1)
    mask = (ci <= ri) if d == 0 else (ci >= ri)
    tri = jnp.where(mask, 1.0, 0.0).astype(BF16)
    fs = _split3(logf)
    bcol = _dot(tri, fs[:, :LANE]) + _dot(tri, fs[:, LANE:2 * LANE]) + _dot(tri, fs[:, 2 * LANE:])
    b = pltpu.roll(bcol, LANE - 2 * ML_HEADS, 1)
    a = t_in - b
    row = lax.broadcasted_iota(jnp.int32, (ln, LANE), 0)
    cm, sh = a, 1
    while sh < ln:
        if d == 0:
            cm = jnp.maximum(cm, jnp.where(row >= sh, pltpu.roll(cm, sh, 0), NEG))
        else:
            cm = jnp.maximum(cm, jnp.where(row < ln - sh, pltpu.roll(cm, ln - sh, 0), NEG))
        sh *= 2
    big_m = jnp.maximum(m_prev, cm)
    xs = _split3(big_m - m_prev)
    ys = _split3(b + big_m)
    at = (a - m_prev).T
    last = ln - 1 if d == 0 else 0
    m_last = big_m[last:last + 1, :]
    m_new = b[last:last + 1, :] + m_last
    decay = jnp.exp(m_prev - m_last)
    wt = jnp.exp(a - m_last).T
    q, k, v = q_ref[...], k_ref[...], v_ref[...]
    new_c = []
    for h in range(ML_HEADS):
        c = d * ML_HEADS + h
        hsl = slice(h * LANE, (h + 1) * LANE)
        neg_x = _dot(xs, negsel_ref[c])
        neg_y = _dot(ys, negsel_ref[c])
        w_intra = jnp.exp(jnp.where(mask, at[c:c + 1, :] + neg_x, NEG))
        kh = k[:, hsl] * scale
        qb, kb, vb = q[:, hsl].astype(BF16), kh.astype(BF16), v[:, hsl].astype(BF16)
        s = _dot_nt(qb, kb) * w_intra
        c_prev = c_aug[c]
        nd = _dot(s.astype(BF16), vb) + jnp.exp(neg_x) * _dot(qb, c_prev.astype(BF16))
        den = pltpu.roll(nd, HEAD_DIM, 1)
        hout = nd / jnp.maximum(jnp.abs(den), jnp.exp(neg_y))
        o_ref[:, h * HEAD_DIM:(h + 1) * HEAD_DIM] = hout[:, :HEAD_DIM]
        new_c.append(decay[:, c:c + 1] * c_prev + _dot((kh.T * wt[c:c + 1, :]).astype(BF16), vb))
    return new_c, m_new


def _mlstm_kernel(qf, kf, vf, iff, qb, kb, vb, ifb, c0, m0, negsel, hf, hb, c_out, m_out, c_s, m_s, *, nc):
    c = pl.program_id(1)
    nst = 2 * ML_HEADS

    @pl.when(c == 0)
    def _():
        c_s[...] = c0[...]
        m_s[...] = m0[...]

    c_aug = [c_s[i] for i in range(nst)]
    m_prev = m_s[...]
    cf, mf = _mlstm_dir(0, qf, kf, vf, iff, hf, c_aug, m_prev, negsel)
    cb, mb = _mlstm_dir(1, qb, kb, vb, ifb, hb, c_aug, m_prev, negsel)
    for i, cn in enumerate(cf + cb):
        c_s[i] = cn
    lane = lax.broadcasted_iota(jnp.int32, (1, LANE), 1)
    m_s[...] = jnp.where(lane < ML_HEADS, mf, mb)

    @pl.when(c == nc - 1)
    def _():
        c_out[...] = c_s[...]
        m_out[...] = m_s[...]


def _mlstm(q, k, v, gif, c0, n0, m0, nb, t):
    nc = t // ML_CHUNK
    nst = 2 * ML_HEADS
    hd = HEAD_DIM
    qkw = ML_HEADS * LANE
    c_aug0 = jnp.concatenate([c0.reshape(nb, nst, hd, hd),
                              jnp.broadcast_to(n0.reshape(nb, nst, hd, 1), (nb, nst, hd, hd))], axis=-1)
    c_aug0 = jnp.pad(c_aug0, ((0, 0), (0, 0), (0, LANE - hd), (0, 0)))
    m_row0 = jnp.pad(m0.reshape(nb, 1, nst), ((0, 0), (0, 0), (0, LANE - nst)))
    sel = np.zeros((nst, 3 * LANE, LANE), np.float32)
    for ch in range(nst):
        sel[ch, [ch, LANE + ch, 2 * LANE + ch], :] = -1.0
    fw = lambda wd: pl.BlockSpec((ML_CHUNK, wd), lambda b, c: (b * nc + c, 0))
    bw = lambda wd: pl.BlockSpec((ML_CHUNK, wd), lambda b, c: (b * nc + nc - 1 - c, 0))
    st = lambda shp: pl.BlockSpec((None,) + shp, lambda b, c: (b,) + (0,) * len(shp))
    shapes = ((nst, LANE, LANE), (1, LANE))
    hf, hb, c_fin, m_fin = pl.pallas_call(
        functools.partial(_mlstm_kernel, nc=nc), grid=(nb, nc),
        in_specs=[fw(qkw), fw(qkw), fw(qkw), fw(LANE), bw(qkw), bw(qkw), bw(qkw), bw(LANE)]
                 + [st(s) for s in shapes] + [pl.BlockSpec(sel.shape, lambda b, c: (0, 0, 0))],
        out_specs=[fw(BRANCH_W), bw(BRANCH_W)] + [st(s) for s in shapes],
        out_shape=[jax.ShapeDtypeStruct((nb * t, BRANCH_W), F32)] * 2
                  + [jax.ShapeDtypeStruct((nb,) + s, F32) for s in shapes],
        scratch_shapes=[pltpu.VMEM(s, F32) for s in shapes],
        compiler_params=_cp(("parallel", "arbitrary")), name="mlstm_scan",
    )(q, k, v, gif, q, k, v, gif, c_aug0, m_row0, jnp.asarray(sel, BF16))
    c_new = c_fin[:, :, :hd, :hd].reshape(nb, 2, ML_HEADS, hd, hd)
    n_new = c_fin[:, :, :hd, hd].reshape(nb, 2, ML_HEADS, hd)
    m_new = m_fin[:, 0, :nst].reshape(nb, 2, ML_HEADS)
    return hf, hb, c_new, n_new, m_new


def _prep_weights(w_in, b_in, mla_q_norm, w_uq, w_uk, w_uv, w_branch, w_out, peer_wq, peer_keys, peer_u, peer_v):
    offs = np.cumsum((0,) + IN_SPLITS)
    seg = lambda a, i: a[..., offs[i]:offs[i + 1]]
    padr = lambda a, wd, lead=0: jnp.pad(a, [(0, 0)] * (a.ndim - 1) + [(lead, wd - a.shape[-1] - lead)])

    def headpad(a, fill):
        a4 = a.reshape(a.shape[:-1] + (ML_HEADS, HEAD_DIM))
        a4 = jnp.pad(a4, [(0, 0)] * (a4.ndim - 1) + [(0, LANE - HEAD_DIM)], constant_values=fill)
        return a4.reshape(a.shape[:-1] + (ML_HEADS * LANE,))

    def layout(a, is_bias):
        ml_if = padr(jnp.concatenate([seg(a, 7), seg(a, 8)], axis=-1), LANE)
        ml = [headpad(seg(a, 3), 0.0), headpad(seg(a, 4), 0.0), headpad(seg(a, 5), 1.0 if is_bias else 0.0)]
        parts = [seg(a, i) for i in range(3)] + ml + [seg(a, 6), ml_if, seg(a, 9), seg(a, 10), seg(a, 11),
                                                       padr(seg(a, 12), 256), seg(a, 13),
                                                       padr(seg(a, 14), LANE, KROPE_SLICE.start)]
        return jnp.concatenate(parts, axis=-1)

    nl = w_in.shape[0]
    w_uq_p = padr(w_uq.reshape(nl, MLA_Q_RANK, MLA_HEADS, MLA_NOPE + MLA_ROPE), LANE).reshape(nl, MLA_Q_RANK, -1)
    w_uq_p = jnp.pad(w_uq_p, ((0, 0), (0, 256 - MLA_Q_RANK), (0, 0)))
    w_uk_p = padr(w_uk.reshape(nl, MLA_KV_RANK, MLA_HEADS, MLA_NOPE), LANE).reshape(nl, MLA_KV_RANK, -1)
    return dict(
        w_ab=layout(w_in, False).astype(BF16), b_ab=layout(b_in, True),
        w_c=seg(w_in, 15).astype(BF16), b_c=seg(b_in, 15),
        q_norm=padr(mla_q_norm, 256), w_uq=w_uq_p.astype(BF16),
        w_ukv=jnp.concatenate([w_uk_p, w_uv], axis=-1).astype(BF16),
        w_branch=w_branch.astype(BF16), w_out=w_out.astype(BF16), peer_wq=peer_wq.astype(BF16),
        peer_keys=peer_keys.astype(BF16), peer_u=peer_u.astype(BF16),
        peer_vt=jnp.swapaxes(peer_v.reshape(nl, -1, PEER_EC, peer_v.shape[-1]), 2, 3).astype(BF16),
    )


def _project(x, wl, l, norm_g, sc, sh, rows_per_mod):
    outs = _linear(x, wl["w_ab"][l], [wd for _, wd in SEGS_AB], bias=wl["b_ab"][l], norm_g=norm_g,
                   mod=(sc, sh), rows_per_mod=rows_per_mod, tm=256, vmem_mb=48, name="in_proj_ab")
    p = {name: o for (name, _), o in zip(SEGS_AB, outs)}
    (p["gate"],) = _linear(x, wl["w_c"][l], [N_BRANCH * D_MODEL], bias=wl["b_c"][l], norm_g=norm_g,
                           mod=(sc, sh), rows_per_mod=rows_per_mod, tm=256, vmem_mb=48, name="in_proj_gate")
    (qm,) = _linear(p["cq"], wl["w_uq"][l], [MLA_QK_W], norm_g=wl["q_norm"][l], k_valid=MLA_Q_RANK, tm=512,
                    name="mla_q_up")
    return p, qm


def kernel(x_prompt, x_sample, cache_na_k, cache_na_v, state_mlstm_C, state_mlstm_n, state_mlstm_m,
           cache_swa_k, cache_swa_v, cache_mla_ckv, cache_mla_krope, c, c_ctx, w_mod, b_mod, norm1_g, norm2_g,
           w_in, b_in, na_rpb, sw_sink, mla_q_norm, w_uq, mla_kv_norm, w_uk, w_uv, w_branch, w_out, peer_wq,
           peer_keys, peer_u, peer_v, final_norm_g):
    nbp, tp, d = x_prompt.shape
    nbs, ts, _ = x_sample.shape
    nl = w_in.shape[0]
    lc = cache_na_k.shape[2]
    nst = 2 * ML_HEADS
    wl = _prep_weights(w_in, b_in, mla_q_norm, w_uq, w_uk, w_uv, w_branch, w_out, peer_wq, peer_keys,
                       peer_u, peer_v)

    n_cond = 1 + nbs
    cond = jnp.zeros((16, d), F32).at[0].set(c_ctx).at[1:n_cond].set(c)
    mods = _modulation(cond, w_mod, b_mod).reshape(nl, 16, 6, 1, d)

    tabs_swq = _rope_tables(ts, SW_HEADS, HEAD_DIM)
    tabs_swk = _rope_tables(ts, SW_KV_HEADS, HEAD_DIM)
    tabs_mlaq = _rope_tables(ts, MLA_HEADS, MLA_ROPE, lead=MLA_NOPE, width=LANE)
    tabs_mlak = _rope_tables(ts, 1, MLA_ROPE, lead=MLA_NOPE, width=LANE)
    na_bias = _na_bias_tables(na_rpb, ts // GRID_W)
    krope_c = jnp.pad(cache_mla_krope, ((0, 0), (0, 0), (0, 0), (KROPE_SLICE.start, LANE - KROPE_SLICE.stop)))

    xp = x_prompt.reshape(nbp * tp, d)
    xs = x_sample.reshape(nbs * ts, d)
    zeros_c = jnp.zeros((nbp, 2, ML_HEADS, HEAD_DIM, HEAD_DIM), F32)
    zeros_n = jnp.zeros((nbp, 2, ML_HEADS, HEAD_DIM), F32)
    zeros_m = jnp.zeros((nbp, 2, ML_HEADS), F32)
    per_layer = tuple([] for _ in range(9))

    for l in range(nl):
        mod_p = [mods[l, 0:1, i] for i in range(6)]
        mod_s = [mods[l, 1:n_cond, i] for i in range(6)]
        sink = sw_sink[l]

        p, qm = _project(xp, wl, l, norm1_g[l], mod_p[1], mod_p[0], nbp * tp)
        ckv_n, kn, vm = _linear(p["ckv"], wl["w_ukv"][l], [MLA_QK_W, 256], norm_g=mla_kv_norm[l],
                                emit_normed=True, tm=512, name="mla_kv_up")
        o_na, o_sw, o_mla = _ctx_attention(p, qm, kn, vm, sink, nbp, tp)
        hf, hb, c_new, n_new, m_new = _mlstm(p["ml_q"], p["ml_k"], p["ml_v"], p["ml_if"], zeros_c, zeros_n,
                                             zeros_m, nbp, tp)
        xp = _merge(o_na, hf, hb, p["ml_o"], o_sw, o_mla, p["gate"], wl["w_branch"][l], wl["w_out"][l], xp,
                    mod_p[2], nbp * tp)
        xp = _peer(xp, norm2_g[l], mod_p[4], mod_p[3], mod_p[5], wl["peer_wq"][l], wl["peer_keys"][l],
                   wl["peer_u"][l], wl["peer_vt"][l], nbp * tp)
        ctx_t = (p["na_k"].reshape(nbp, tp, NA_HEADS, HEAD_DIM), p["na_v"].reshape(nbp, tp, NA_HEADS, HEAD_DIM),
                 c_new, n_new, m_new,
                 p["sw_k"].reshape(nbp, tp, SW_KV_HEADS, HEAD_DIM), p["sw_v"].reshape(nbp, tp, SW_KV_HEADS, HEAD_DIM),
                 ckv_n.reshape(nbp, tp, MLA_KV_RANK), p["krope"][:, KROPE_SLICE].reshape(nbp, tp, MLA_ROPE))
        for i, a in enumerate(ctx_t):
            per_layer[i].append(a)

        p, qm = _project(xs, wl, l, norm1_g[l], mod_s[1], mod_s[0], ts)
        _, kn, vt = _linear(p["ckv"], wl["w_ukv"][l], [MLA_QK_W, 256], norm_g=mla_kv_norm[l], emit_normed=True,
                            transpose_last=True, tm=512, name="mla_kv_up_t")
        knc, vtc = _linear(cache_mla_ckv[:, l].reshape(nbs * lc, MLA_KV_RANK), wl["w_ukv"][l], [MLA_QK_W, 256],
                           transpose_last=True, tm=512, name="mla_kv_up_cache")
        o_na = _na_attention(p["na_q"], p["na_k"], p["na_v"], cache_na_k[:, l].reshape(nbs, lc, BRANCH_W),
                             cache_na_v[:, l].reshape(nbs, lc, BRANCH_W), na_bias[l], nbs, ts)
        o_sw = _sw_attention(p["sw_q"], p["sw_k"], p["sw_v"],
                             cache_swa_k[:, l].reshape(nbs, lc, SW_KV_HEADS * HEAD_DIM),
                             cache_swa_v[:, l].reshape(nbs, lc, SW_KV_HEADS * HEAD_DIM), sink, tabs_swq, tabs_swk,
                             nbs, ts)
        o_mla = _mla_attention(qm, kn, p["krope"], vt, knc, krope_c[:, l], vtc, tabs_mlaq, tabs_mlak, nbs, ts)
        hf, hb, _, _, _ = _mlstm(p["ml_q"], p["ml_k"], p["ml_v"], p["ml_if"], state_mlstm_C[:, l],
                                 state_mlstm_n[:, l], state_mlstm_m[:, l], nbs, ts)
        xs = _merge(o_na, hf, hb, p["ml_o"], o_sw, o_mla, p["gate"], wl["w_branch"][l], wl["w_out"][l], xs,
                    mod_s[2], ts)
        xs = _peer(xs, norm2_g[l], mod_s[4], mod_s[3], mod_s[5], wl["peer_wq"][l], wl["peer_keys"][l],
                   wl["peer_u"][l], wl["peer_vt"][l], ts)

    y_prompt = _final_norm(xp, final_norm_g).reshape(nbp, tp, d)
    y_sample = _final_norm(xs, final_norm_g).reshape(nbs, ts, d)
    return (y_prompt, y_sample) + tuple(jnp.stack(s, axis=1) for s in per_layer)
```

```python
import functools

import numpy as np
import jax
import jax.numpy as jnp
from jax import lax
from jax.experimental import pallas as pl
from jax.experimental.pallas import tpu as pltpu

F32 = jnp.float32
BF16 = jnp.bfloat16

D_MODEL = 1024
GRID_W = 64
HEAD_DIM = 64
N_BRANCH = 4
BRANCH_W = 256
NA_HEADS = 4
NA_KR_MAX = 8
NA_KC = 16
ML_HEADS = 4
ML_CHUNK = 128
SW_HEADS = 4
SW_KV_HEADS = 2
SW_GROUP = SW_HEADS // SW_KV_HEADS
SW_WINDOW = 128
MLA_HEADS = 4
MLA_Q_RANK = 192
MLA_KV_RANK = 128
MLA_NOPE = 64
MLA_ROPE = 32
MLA_V = 64
MLA_SCALE = (MLA_NOPE + MLA_ROPE) ** -0.5
PEER_HEADS = 8
PEER_KEY_DIM = 64
PEER_N_KEYS = 128
PEER_TOPK = 16
PEER_EC = 512
ROPE_BASE = 10000.0
RMS_EPS = 1e-6
NEG = -1e30
IN_SPLITS = (256, 256, 256, 256, 256, 256, 256, 8, 8, 256, 128, 128, 192, 128, 32, 4096)

LANE = 128
MLA_QK_W = MLA_HEADS * LANE

SEGS_AB = (("na_q", 256), ("na_k", 256), ("na_v", 256), ("ml_q", 512), ("ml_k", 512), ("ml_v", 512),
           ("ml_o", 256), ("ml_if", 128), ("sw_q", 256), ("sw_k", 128), ("sw_v", 128), ("cq", 256),
           ("ckv", 128), ("krope", 128))
KROPE_SLICE = slice(MLA_NOPE, MLA_NOPE + MLA_ROPE)


def _cp(sem, vmem_mb=None):
    kw = dict(dimension_semantics=sem)
    if vmem_mb is not None:
        kw["vmem_limit_bytes"] = vmem_mb * 1024 * 1024
    return pltpu.CompilerParams(**kw)


def _dot(a, b):
    return jnp.dot(a, b, preferred_element_type=F32)


def _dot_nt(a, b):
    return lax.dot_general(a, b, (((1,), (1,)), ((), ())), preferred_element_type=F32)


def _rms(x, inv_n):
    return x * lax.rsqrt(jnp.sum(x * x, axis=-1, keepdims=True) * inv_n + RMS_EPS)


def _sigmoid(x):
    return 1.0 / (1.0 + jnp.exp(-x))


def _mod_kernel(c_ref, w_ref, b_ref, o_ref):
    c = c_ref[...]
    s = c * _sigmoid(c)
    o_ref[...] = _dot(s.astype(BF16), w_ref[...].astype(BF16)) + b_ref[...]


def _modulation(cond, w_mod, b_mod):
    nl, d, n = w_mod.shape
    r = cond.shape[0]
    tn = 1024
    return pl.pallas_call(
        _mod_kernel,
        grid=(nl, n // tn),
        in_specs=[pl.BlockSpec((r, d), lambda l, j: (0, 0)),
                  pl.BlockSpec((None, d, tn), lambda l, j: (l, 0, j)),
                  pl.BlockSpec((None, 1, tn), lambda l, j: (l, 0, j))],
        out_specs=pl.BlockSpec((None, r, tn), lambda l, j: (l, 0, j)),
        out_shape=jax.ShapeDtypeStruct((nl, r, n), F32),
        compiler_params=_cp(("parallel", "parallel")),
        name="adaln_modulation",
    )(cond, w_mod, b_mod.reshape(nl, 1, n))


def _linear_kernel(*refs, segs, has_norm, has_mod, has_bias, emit_normed, transpose_last, inv_k):
    it = iter(refs)
    x_ref = next(it)
    g_ref = next(it) if has_norm else None
    sc_ref = next(it) if has_mod else None
    sh_ref = next(it) if has_mod else None
    w_ref = next(it)
    b_ref = next(it) if has_bias else None
    outs = list(it)
    x = x_ref[...]
    if has_norm:
        x = _rms(x, inv_k) * g_ref[...]
    if has_mod:
        x = x * (1.0 + sc_ref[0]) + sh_ref[0]
    if emit_normed:
        outs[0][...] = x
        outs = outs[1:]
    xb = x.astype(BF16)
    for idx, ((s, wd), o) in enumerate(zip(segs, outs)):
        y = _dot(xb, w_ref[:, s:s + wd])
        if has_bias:
            y = y + b_ref[:, s:s + wd]
        o[...] = y.T if (transpose_last and idx == len(segs) - 1) else y


def _linear(x, w, widths, *, bias=None, norm_g=None, k_valid=None, mod=None, rows_per_mod=None,
            emit_normed=False, transpose_last=False, tm=256, vmem_mb=None, name="linear"):
    n, k = x.shape
    tm = min(tm, n)
    segs, s = [], 0
    for wd in widths:
        segs.append((s, wd))
        s += wd
    assert s == w.shape[1] and n % tm == 0
    args = [x]
    in_specs = [pl.BlockSpec((tm, k), lambda i: (i, 0))]
    if norm_g is not None:
        args.append(norm_g.reshape(1, k))
        in_specs.append(pl.BlockSpec((1, k), lambda i: (0, 0)))
    if mod is not None:
        assert rows_per_mod % tm == 0
        for m in mod:
            args.append(m)
            in_specs.append(pl.BlockSpec((1, 1, k), lambda i: ((i * tm) // rows_per_mod, 0, 0)))
    args.append(w)
    in_specs.append(pl.BlockSpec(w.shape, lambda i: (0, 0)))
    if bias is not None:
        args.append(bias.reshape(1, -1))
        in_specs.append(pl.BlockSpec((1, w.shape[1]), lambda i: (0, 0)))
    out_shapes, out_specs = [], []
    if emit_normed:
        out_shapes.append(jax.ShapeDtypeStruct((n, k), F32))
        out_specs.append(pl.BlockSpec((tm, k), lambda i: (i, 0)))
    for idx, wd in enumerate(widths):
        if transpose_last and idx == len(widths) - 1:
            out_shapes.append(jax.ShapeDtypeStruct((wd, n), F32))
            out_specs.append(pl.BlockSpec((wd, tm), lambda i: (0, i)))
        else:
            out_shapes.append(jax.ShapeDtypeStruct((n, wd), F32))
            out_specs.append(pl.BlockSpec((tm, wd), lambda i: (i, 0)))
    kern = functools.partial(_linear_kernel, segs=tuple(segs), has_norm=norm_g is not None,
                             has_mod=mod is not None, has_bias=bias is not None, emit_normed=emit_normed,
                             transpose_last=transpose_last, inv_k=1.0 / (k_valid or k))
    return pl.pallas_call(kern, grid=(n // tm,), in_specs=in_specs, out_specs=out_specs,
                          out_shape=out_shapes, compiler_params=_cp(("parallel",), vmem_mb), name=name)(*args)


def _merge_kernel(ona, hf, hb, mlo, osw, omla, gate, wb, wo, x, g1, bd, o):
    h = hf[...] + hb[...]
    hsq = h * h
    hi = hsq.astype(BF16)
    lo = (hsq - hi.astype(F32)).astype(BF16)
    ms = (_dot(hi, bd[...]) + _dot(lo, bd[...])) * (1.0 / HEAD_DIM)
    oml = h * lax.rsqrt(ms + RMS_EPS) * _sigmoid(mlo[...])
    branches = (ona[...], oml, osw[...], omla[...])
    acc = None
    for n in range(N_BRANCH):
        y = _sigmoid(gate[:, n * D_MODEL:(n + 1) * D_MODEL]) * _dot(branches[n].astype(BF16), wb[n])
        acc = y if acc is None else acc + y
    o[...] = x[...] + g1[0] * _dot(acc.astype(BF16), wo[...])


def _merge(ona, hf, hb, mlo, osw, omla, gate, wb, wo, x, g1, rows_per_mod, tm=256):
    n = x.shape[0]
    tm = min(tm, n)
    bd = jnp.asarray(np.kron(np.eye(BRANCH_W // HEAD_DIM), np.ones((HEAD_DIM, HEAD_DIM))), BF16)
    row = lambda wd: pl.BlockSpec((tm, wd), lambda i: (i, 0))
    return pl.pallas_call(
        _merge_kernel,
        grid=(n // tm,),
        in_specs=[row(BRANCH_W)] * 6 + [row(N_BRANCH * D_MODEL),
                  pl.BlockSpec(wb.shape, lambda i: (0, 0, 0)),
                  pl.BlockSpec(wo.shape, lambda i: (0, 0)),
                  row(D_MODEL),
                  pl.BlockSpec((1, 1, D_MODEL), lambda i: ((i * tm) // rows_per_mod, 0, 0)),
                  pl.BlockSpec(bd.shape, lambda i: (0, 0))],
        out_specs=row(D_MODEL),
        out_shape=jax.ShapeDtypeStruct((n, D_MODEL), F32),
        compiler_params=_cp(("parallel",), 48),
        name="branch_merge",
    )(ona, hf, hb, mlo, osw, omla, gate, wb, wo, x, g1, bd)


def _top_rows(s, k):
    cur, rows = s, []
    for _ in range(k):
        mx = jnp.max(cur, axis=0, keepdims=True)
        rows.append(mx)
        cur = jnp.where(cur == mx, NEG, cur)
    return rows


def _peer_select(s1, s2):
    rows8 = lax.broadcasted_iota(jnp.int32, (8, 1), 0)
    a1, a2 = _top_rows(s1, PEER_TOPK), _top_rows(s2, PEER_TOPK)
    a1_16 = jnp.concatenate(a1, axis=0)
    a1_8 = a1_16[:8]
    cands = [a1_16 + a2[0], a1_8 + a2[1]]
    for k2 in range(2, 8):
        cands.append(jnp.where(rows8 < PEER_TOPK // (k2 + 1), a1_8 + a2[k2], NEG))
    cands.append(a1[0] + jnp.concatenate(a2[8:], axis=0))
    cand = jnp.concatenate(cands, axis=0)
    top = _top_rows(cand, PEER_TOPK + 1)
    tau = 0.5 * (top[PEER_TOPK - 1] + top[PEER_TOPK])
    z = jnp.sum(jnp.where(cand >= tau, jnp.exp(cand - top[0]), 0.0), axis=0, keepdims=True)
    return tau - s1, jnp.exp(s1 - a1[0]), jnp.exp(s2 - a2[0]) * (0.5 / z)


def _peer_kernel(x_ref, ng_ref, sc_ref, sh_ref, g2_ref, wq_ref, keys_ref, u_ref, vt_ref, o_ref,
                 h2t_ref, acc_ref, t1_ref, e1_ref, s2_ref, e2_ref, ht_ref, z_ref, *,
                 ec, n_chunks, tm):
    s = pl.program_id(1)
    n_i1 = ec // PEER_N_KEYS
    n_tc = tm // LANE
    half = n_tc // 2
    hw = tm // 2

    @pl.when(s == 0)
    def _():
        x = x_ref[...]
        h2 = _rms(x, 1.0 / D_MODEL) * ng_ref[...] * (1.0 + sc_ref[0]) + sh_ref[0]
        h2t = h2.T.astype(BF16)
        for nh in range(2):
            h2t_ref[nh] = h2t[:, nh * hw:(nh + 1) * hw]
        qt = _dot(h2.astype(BF16), wq_ref[...]).T.astype(BF16)
        for h in range(PEER_HEADS):
            r0 = h * 2 * PEER_KEY_DIM
            s1 = _dot(keys_ref[h, 0], qt[r0:r0 + PEER_KEY_DIM, :])
            s2 = _dot(keys_ref[h, 1], qt[r0 + PEER_KEY_DIM:r0 + 2 * PEER_KEY_DIM, :])
            for tc in range(n_tc):
                cols = slice(tc * LANE, (tc + 1) * LANE)
                t1, e1, e2 = _peer_select(s1[:, cols], s2[:, cols])
                t1_ref[h, tc] = t1
                e1_ref[h, tc] = e1
                s2_ref[h, tc] = s2[:, cols]
                e2_ref[h, tc] = e2
        for r in (ht_ref, z_ref, acc_ref):
            r[...] = jnp.zeros_like(r)

    cb = jnp.clip(s - 1, 0, n_chunks - 1)
    orows = D_MODEL // n_i1
    n_rh = 4
    rh_rows = PEER_N_KEYS // n_rh
    n_iter = 2 * n_i1
    upi = n_tc * n_rh // n_iter

    sw = s % 2
    sr = 1 - sw
    th_rows = [[[t1_ref[h, tc, pl.ds(cb * n_i1 + j4, 1), :] for j4 in range(n_i1)] for h in range(PEER_HEADS)]
               for tc in range(n_tc)]
    e1_rows = [[[e1_ref[h, tc, pl.ds(cb * n_i1 + j4, 1), :] for j4 in range(n_i1)] for h in range(PEER_HEADS)]
               for tc in range(n_tc)]
    for i in range(n_iter):
        j, nh = divmod(i, 2)
        rows = slice(j * PEER_N_KEYS, (j + 1) * PEER_N_KEYS)
        hj = _dot(u_ref[rows, :], h2t_ref[nh])
        for k in range(half):
            ht_ref[sw, nh * half + k, rows, :] = hj[:, k * LANE:(k + 1) * LANE]
        osl = slice(j * orows, (j + 1) * orows)
        zh = jnp.concatenate([z_ref[sw, nh * half + k] for k in range(half)], axis=1)
        acc_ref[nh, osl, :] += _dot(vt_ref[osl, :], zh)
        for k in range(upi):
            tc, rh = divmod(i * upi + k, n_rh)
            r2 = slice(rh * rh_rows, (rh + 1) * rh_rows)
            ws = [None] * n_i1
            for h in range(PEER_HEADS):
                s2t, e2t = s2_ref[h, tc, r2, :], e2_ref[h, tc, r2, :]
                for j4 in range(n_i1):
                    term = jnp.where(s2t >= th_rows[tc][h][j4], e2t, 0.0) * e1_rows[tc][h][j4]
                    ws[j4] = term if ws[j4] is None else ws[j4] + term
            for j4 in range(n_i1):
                rr = slice(j4 * PEER_N_KEYS + rh * rh_rows, j4 * PEER_N_KEYS + (rh + 1) * rh_rows)
                xh = ht_ref[sr, tc, rr, :]
                t = jnp.tanh(xh * (0.7978845608028654 + 0.035677408136300125 * (xh * xh)))
                z_ref[sr, tc, rr, :] = (ws[j4] * (xh + xh * t)).astype(BF16)

    @pl.when(s == n_chunks + 1)
    def _():
        for nh in range(2):
            rs = slice(nh * hw, (nh + 1) * hw)
            o_ref[rs, :] = x_ref[rs, :] + g2_ref[0] * acc_ref[nh].T


def _peer(x, ng, sc, sh, g2, wq, keys, u, vt, rows_per_mod, tm=1024):
    n = x.shape[0]
    tm = min(tm, n)
    n_chunks, _, ec = vt.shape
    last = n_chunks - 1
    n_tc = tm // LANE
    modspec = pl.BlockSpec((1, 1, D_MODEL), lambda i, s: ((i * tm) // rows_per_mod, 0, 0))
    kern = functools.partial(_peer_kernel, ec=ec, n_chunks=n_chunks, tm=tm)
    tile_buf = pltpu.VMEM((PEER_HEADS, n_tc, PEER_N_KEYS, LANE), F32)
    return pl.pallas_call(
        kern,
        grid=(n // tm, n_chunks + 2),
        in_specs=[pl.BlockSpec((tm, D_MODEL), lambda i, s: (i, 0)),
                  pl.BlockSpec((1, D_MODEL), lambda i, s: (0, 0)),
                  modspec, modspec, modspec,
                  pl.BlockSpec(wq.shape, lambda i, s: (0, 0)),
                  pl.BlockSpec(keys.shape, lambda i, s: (0, 0, 0, 0)),
                  pl.BlockSpec((ec, D_MODEL), lambda i, s: (jnp.minimum(s, last), 0)),
                  pl.BlockSpec((None, D_MODEL, ec), lambda i, s: (jnp.clip(s - 2, 0, last), 0, 0))],
        out_specs=pl.BlockSpec((tm, D_MODEL), lambda i, s: (i, 0)),
        out_shape=jax.ShapeDtypeStruct((n, D_MODEL), F32),
        scratch_shapes=[pltpu.VMEM((2, D_MODEL, tm // 2), BF16),
                        pltpu.VMEM((2, D_MODEL, tm // 2), F32),
                        tile_buf, tile_buf, tile_buf, tile_buf,
                        pltpu.VMEM((2, n_tc, ec, LANE), F32), pltpu.VMEM((2, n_tc, ec, LANE), BF16)],
        compiler_params=_cp(("parallel", "arbitrary"), 61),
        name="peer_ffn",
    )(x, ng.reshape(1, D_MODEL), sc, sh, g2, wq, keys, u, vt)


def _final_norm_kernel(x_ref, g_ref, o_ref):
    o_ref[...] = _rms(x_ref[...], 1.0 / D_MODEL) * g_ref[...]


def _final_norm(x, g, tm=512):
    n = x.shape[0]
    tm = min(tm, n)
    return pl.pallas_call(
        _final_norm_kernel, grid=(n // tm,),
        in_specs=[pl.BlockSpec((tm, D_MODEL), lambda i: (i, 0)), pl.BlockSpec((1, D_MODEL), lambda i: (0, 0))],
        out_specs=pl.BlockSpec((tm, D_MODEL), lambda i: (i, 0)),
        out_shape=jax.ShapeDtypeStruct((n, D_MODEL), F32),
        compiler_params=_cp(("parallel",)), name="final_norm")(x, g.reshape(1, D_MODEL))


def _rope_tables(t, n_heads, dh, lead=0, width=None):
    width = width or dh
    pos = jnp.arange(t)
    sec, half = dh // 2, dh // 4
    freqs = ROPE_BASE ** (-jnp.arange(half, dtype=F32) / half)
    first = jnp.asarray((np.arange(dh) % sec) < half)
    ang_r = (pos // GRID_W).astype(F32)[:, None] * freqs[None, :]
    ang_c = (pos % GRID_W).astype(F32)[:, None] * freqs[None, :]
    ang = jnp.concatenate([ang_r, ang_r, ang_c, ang_c], axis=1)
    cos, sin = jnp.cos(ang), jnp.sin(ang)
    sa = jnp.where(first[None, :], -sin, 0.0)
    sb = jnp.where(first[None, :], 0.0, sin)
    padw = ((0, 0), (lead, width - dh - lead))
    place = lambda a, fill: jnp.tile(jnp.pad(a, padw, constant_values=fill), (1, n_heads))
    return place(cos, 1.0), place(sa, 0.0), place(sb, 0.0)


def _rope(x, cos, sa, sb, half):
    w = x.shape[-1]
    return x * cos + pltpu.roll(x, w - half, 1) * sa + pltpu.roll(x, half, 1) * sb


def _softmax_pv(blocks, sink=None):
    m = None
    for s, _ in blocks:
        bm = jnp.max(s, axis=1, keepdims=True)
        m = bm if m is None else jnp.maximum(m, bm)
    if sink is not None:
        m = jnp.maximum(m, sink)
    l, acc = None, None
    for s, v in blocks:
        p = jnp.exp(s - m)
        bl = jnp.sum(p, axis=1, keepdims=True)
        pv = _dot(p.astype(BF16), v)
        l = bl if l is None else l + bl
        acc = pv if acc is None else acc + pv
    if sink is not None:
        l = l + jnp.exp(sink - m)
    return acc / l


def _ctx_attn_kernel(naq, nak, nav, swq, swk, swv, qm, kn, kr, vm, sink_ref, o_na, o_sw, o_mla):
    hd = HEAD_DIM
    scale = hd ** -0.5
    q, k, v = naq[...].astype(BF16), nak[...].astype(BF16), nav[...].astype(BF16)
    for h in range(NA_HEADS):
        hs = slice(h * hd, (h + 1) * hd)
        o_na[:, hs] = _softmax_pv([(_dot_nt(q[:, hs], k[:, hs]) * scale, v[:, hs])])
    q, k, v = swq[...].astype(BF16), swk[...].astype(BF16), swv[...].astype(BF16)
    for h in range(SW_HEADS):
        g = h // SW_GROUP
        hs, gs = slice(h * hd, (h + 1) * hd), slice(g * hd, (g + 1) * hd)
        o_sw[:, hs] = _softmax_pv([(_dot_nt(q[:, hs], k[:, gs]) * scale, v[:, gs])], sink=sink_ref[h])
    q, v = qm[...].astype(BF16), vm[...].astype(BF16)
    krv = kr[...]
    for h in range(MLA_HEADS):
        ks = slice(h * LANE, (h + 1) * LANE)
        kcat = (kn[:, ks] + krv).astype(BF16)
        s = _dot_nt(q[:, ks], kcat) * MLA_SCALE
        o_mla[:, h * MLA_V:(h + 1) * MLA_V] = _softmax_pv([(s, v[:, h * MLA_V:(h + 1) * MLA_V])])


def _ctx_attention(p, qm, kn, vm, sink, nb, t):
    row = lambda wd: pl.BlockSpec((t, wd), lambda b: (b, 0))
    outs = pl.pallas_call(
        _ctx_attn_kernel, grid=(nb,),
        in_specs=[row(256), row(256), row(256), row(256), row(128), row(128), row(MLA_QK_W), row(MLA_QK_W),
                  row(LANE), row(256), pl.BlockSpec(memory_space=pltpu.SMEM)],
        out_specs=[row(256)] * 3,
        out_shape=[jax.ShapeDtypeStruct((nb * t, BRANCH_W), F32)] * 3,
        compiler_params=_cp(("parallel",)), name="ctx_attention",
    )(p["na_q"], p["na_k"], p["na_v"], p["sw_q"], p["sw_k"], p["sw_v"], qm, kn, p["krope"], vm, sink)
    return outs


NA_TQ = 256


def _na_bias_tables(rpb, rows_n):
    nr, nc = 2 * NA_KR_MAX - 1, 2 * NA_KC - 1
    p_sel = np.zeros((3, 4, 12, nr), np.float32)
    ok_r = np.zeros((3, 4, 12), bool)
    for ti, j in enumerate((0, 1, rows_n // 4 - 1)):
        for a in range(4):
            r = 4 * j + a
            r0 = min(max(r - NA_KR_MAX // 2, 0), rows_n - NA_KR_MAX)
            for e in range(12):
                krow = 4 * (j - 1) + e
                if r0 <= krow < r0 + NA_KR_MAX:
                    ok_r[ti, a, e] = True
                    p_sel[ti, a, e, krow - r + NA_KR_MAX - 1] = 1.0
    c = np.arange(GRID_W)[:, None]
    w = np.arange(GRID_W)[None, :]
    cs = np.clip(c - NA_KC // 2, 0, GRID_W - NA_KC)
    ok_c = (w >= cs) & (w < cs + NA_KC)
    q_sel = np.eye(nc, dtype=np.float32)[np.clip(w - c + NA_KC - 1, 0, nc - 1)]
    hp = lax.Precision.HIGHEST
    m1 = jnp.einsum("taei,lhij->lthaej", jnp.asarray(p_sel), rpb, precision=hp)
    b = jnp.einsum("lthaej,cwj->lthacew", m1, jnp.asarray(q_sel), precision=hp)
    ok = ok_r[:, :, None, :, None] & ok_c[None, None, :, None, :]
    b = jnp.where(jnp.asarray(ok)[None, :, None], b, NEG)
    return b.reshape(rpb.shape[0], 3, NA_HEADS, 4 * GRID_W, 12 * GRID_W)


def _na_kernel(q, k0, k1, k2, v0, v1, v2, kc, vc, bias, o):
    hd = HEAD_DIM
    scale = hd ** -0.5
    qb = q[...].astype(BF16)
    kk = jnp.concatenate([k0[...], k1[...], k2[...]], axis=0).astype(BF16)
    vv = jnp.concatenate([v0[...], v1[...], v2[...]], axis=0).astype(BF16)
    kcb, vcb = kc[...].astype(BF16), vc[...].astype(BF16)
    for h in range(NA_HEADS):
        hs = slice(h * hd, (h + 1) * hd)
        s_nb = _dot_nt(qb[:, hs], kk[:, hs]) * scale + bias[h]
        s_cx = _dot_nt(qb[:, hs], kcb[:, hs]) * scale
        o[:, hs] = _softmax_pv([(s_nb, vv[:, hs]), (s_cx, vcb[:, hs])])


def _na_attention(q, k, v, kc, vc, bias, nb, t):
    nj = t // NA_TQ
    lc = kc.shape[1]
    blk = lambda f: pl.BlockSpec((NA_TQ, BRANCH_W), lambda b, j: (b * nj + f(j), 0))
    prev = lambda j: jnp.maximum(j - 1, 0)
    nxt = lambda j: jnp.minimum(j + 1, nj - 1)
    cur = lambda j: j
    ctx = pl.BlockSpec((None, lc, BRANCH_W), lambda b, j: (b, 0, 0))
    tsel = lambda b, j: (jnp.where(j == 0, 0, jnp.where(j == nj - 1, 2, 1)), 0, 0, 0)
    return pl.pallas_call(
        _na_kernel, grid=(nb, nj),
        in_specs=[blk(cur), blk(prev), blk(cur), blk(nxt), blk(prev), blk(cur), blk(nxt), ctx, ctx,
                  pl.BlockSpec((None,) + bias.shape[1:], tsel)],
        out_specs=blk(cur),
        out_shape=jax.ShapeDtypeStruct((nb * t, BRANCH_W), F32),
        compiler_params=_cp(("parallel", "arbitrary"), 48), name="na_attention",
    )(q, k, k, k, v, v, v, kc, vc, bias)


SW_TQ = 256


def _sw_kernel(q, k0, k1, k2, v0, v1, v2, kc, vc, cq, saq, sbq, ck0, sak0, sbk0, ck1, sak1, sbk1,
               ck2, sak2, sbk2, sink_ref, o, *, t):
    hd = HEAD_DIM
    scale = hd ** -0.5
    half = hd // 4
    j = pl.program_id(1)
    qr = _rope(q[...], cq[...], saq[...], sbq[...], half).astype(BF16)
    kk = jnp.concatenate([_rope(k0[...], ck0[...], sak0[...], sbk0[...], half),
                          _rope(k1[...], ck1[...], sak1[...], sbk1[...], half),
                          _rope(k2[...], ck2[...], sak2[...], sbk2[...], half)], axis=0).astype(BF16)
    vv = jnp.concatenate([v0[...], v1[...], v2[...]], axis=0).astype(BF16)
    kcb, vcb = kc[...].astype(BF16), vc[...].astype(BF16)
    qpos = j * SW_TQ + lax.broadcasted_iota(jnp.int32, (SW_TQ, 3 * SW_TQ), 0)
    kpos = (j - 1) * SW_TQ + lax.broadcasted_iota(jnp.int32, (SW_TQ, 3 * SW_TQ), 1)
    valid = (jnp.abs(qpos - kpos) <= SW_WINDOW) & (kpos >= 0) & (kpos < t)
    for h in range(SW_HEADS):
        g = h // SW_GROUP
        hs, gs = slice(h * hd, (h + 1) * hd), slice(g * hd, (g + 1) * hd)
        s_loc = jnp.where(valid, _dot_nt(qr[:, hs], kk[:, gs]) * scale, NEG)
        s_cx = _dot_nt(qr[:, hs], kcb[:, gs]) * scale
        o[:, hs] = _softmax_pv([(s_loc, vv[:, gs]), (s_cx, vcb[:, gs])], sink=sink_ref[h])


def _sw_attention(q, k, v, kc, vc, sink, tabs_q, tabs_k, nb, t):
    nj = t // SW_TQ
    lc = kc.shape[1]
    kvw = SW_KV_HEADS * HEAD_DIM
    prev = lambda j: jnp.maximum(j - 1, 0)
    nxt = lambda j: jnp.minimum(j + 1, nj - 1)
    cur = lambda j: j
    qblk = pl.BlockSpec((SW_TQ, BRANCH_W), lambda b, j: (b * nj + j, 0))
    kblk = lambda f: pl.BlockSpec((SW_TQ, kvw), lambda b, j: (b * nj + f(j), 0))
    ctx = pl.BlockSpec((None, lc, kvw), lambda b, j: (b, 0, 0))
    tq = pl.BlockSpec((SW_TQ, BRANCH_W), lambda b, j: (j, 0))
    tk = lambda f: pl.BlockSpec((SW_TQ, kvw), lambda b, j: (f(j), 0))
    in_specs = [qblk, kblk(prev), kblk(cur), kblk(nxt), kblk(prev), kblk(cur), kblk(nxt), ctx, ctx, tq, tq, tq]
    args = [q, k, k, k, v, v, v, kc, vc, *tabs_q]
    for f in (prev, cur, nxt):
        in_specs += [tk(f)] * 3
        args += list(tabs_k)
    in_specs.append(pl.BlockSpec(memory_space=pltpu.SMEM))
    args.append(sink)
    return pl.pallas_call(
        functools.partial(_sw_kernel, t=t), grid=(nb, nj), in_specs=in_specs, out_specs=qblk,
        out_shape=jax.ShapeDtypeStruct((nb * t, BRANCH_W), F32),
        compiler_params=_cp(("parallel", "arbitrary"), 48), name="sw_attention",
    )(*args)


MLA_TQ = 512
MLA_TK = 512


def _mla_kernel(qm, kn, kr, vt, knc, krc, vtc, cq, saq, sbq, ck, sak, sbk, o, qt_ref, m_ref, l_ref, acc_ref, *,
                n_lat):
    kj = pl.program_id(2)
    half = MLA_ROPE // 4

    @pl.when(kj == 0)
    def _():
        qt_ref[...] = _rope(qm[...], cq[...], saq[...], sbq[...], half).T.astype(BF16)
        m_ref[...] = jnp.full(m_ref.shape, NEG, F32)
        l_ref[...] = jnp.zeros(l_ref.shape, F32)
        acc_ref[...] = jnp.zeros(acc_ref.shape, F32)

    c2 = MLA_SCALE * 1.4426950408889634

    def step(kcat, vt_b):
        ones = jnp.ones((8, vt_b.shape[1]), BF16)
        for h in range(MLA_HEADS):
            ks = slice(h * LANE, (h + 1) * LANE)
            vs = slice(h * MLA_V, (h + 1) * MLA_V)
            s = _dot(kcat[:, ks], qt_ref[ks, :])
            m_old = m_ref[h]
            m_new = jnp.maximum(m_old, jnp.max(s, axis=0, keepdims=True))
            alpha = jnp.exp2(c2 * (m_old - m_new))
            p = jnp.exp2(c2 * s - c2 * m_new).astype(BF16)
            pv = _dot(jnp.concatenate([vt_b[vs, :], ones], axis=0), p)
            l_ref[h] = alpha * l_ref[h] + pv[MLA_V:MLA_V + 1, :]
            acc_ref[vs, :] = alpha * acc_ref[vs, :] + pv[:MLA_V, :]
            m_ref[h] = m_new

    @pl.when(kj < n_lat)
    def _():
        krr = _rope(kr[...], ck[...], sak[...], sbk[...], half)
        step((kn[...] + jnp.concatenate([krr] * MLA_HEADS, axis=1)).astype(BF16), vt[...].astype(BF16))

    @pl.when(kj == n_lat)
    def _():
        step((knc[...] + jnp.concatenate([krc[...]] * MLA_HEADS, axis=1)).astype(BF16), vtc[...].astype(BF16))
        for h in range(MLA_HEADS):
            vs = slice(h * MLA_V, (h + 1) * MLA_V)
            acc_ref[vs, :] = acc_ref[vs, :] / l_ref[h]
        o[...] = acc_ref[...].T


def _mla_attention(qm, kn, kr, vt, knc, krc, vtc, tabs_q, tabs_k, nb, t):
    nq, n_lat = t // MLA_TQ, t // MLA_TK
    lc = krc.shape[1]
    assert lc == MLA_TK
    kb = lambda b, i, kj: b * n_lat + jnp.minimum(kj, n_lat - 1)
    kblk = lambda wd: pl.BlockSpec((MLA_TK, wd), lambda b, i, kj: (kb(b, i, kj), 0))
    tkb = pl.BlockSpec((MLA_TK, LANE), lambda b, i, kj: (jnp.minimum(kj, n_lat - 1), 0))
    tqb = pl.BlockSpec((MLA_TQ, MLA_QK_W), lambda b, i, kj: (i, 0))
    return pl.pallas_call(
        functools.partial(_mla_kernel, n_lat=n_lat), grid=(nb, nq, n_lat + 1),
        in_specs=[pl.BlockSpec((MLA_TQ, MLA_QK_W), lambda b, i, kj: (b * nq + i, 0)),
                  kblk(MLA_QK_W), kblk(LANE),
                  pl.BlockSpec((BRANCH_W, MLA_TK), lambda b, i, kj: (0, kb(b, i, kj))),
                  pl.BlockSpec((lc, MLA_QK_W), lambda b, i, kj: (b, 0)),
                  pl.BlockSpec((None, lc, LANE), lambda b, i, kj: (b, 0, 0)),
                  pl.BlockSpec((BRANCH_W, lc), lambda b, i, kj: (0, b)),
                  tqb, tqb, tqb, tkb, tkb, tkb],
        out_specs=pl.BlockSpec((MLA_TQ, BRANCH_W), lambda b, i, kj: (b * nq + i, 0)),
        out_shape=jax.ShapeDtypeStruct((nb * t, BRANCH_W), F32),
        scratch_shapes=[pltpu.VMEM((MLA_QK_W, MLA_TQ), BF16), pltpu.VMEM((MLA_HEADS, 1, MLA_TQ), F32),
                        pltpu.VMEM((MLA_HEADS, 1, MLA_TQ), F32), pltpu.VMEM((BRANCH_W, MLA_TQ), F32)],
        compiler_params=_cp(("parallel", "parallel", "arbitrary"), 48), name="mla_attention",
    )(qm, kn, kr, vt, knc, krc, vtc, *tabs_q, *tabs_k)


def _log_sigmoid(x):
    return jnp.minimum(x, 0.0) - jnp.log(1.0 + jnp.exp(-jnp.abs(x)))


def _split3(x):
    hi = x.astype(BF16)
    r1 = x - hi.astype(F32)
    mid = r1.astype(BF16)
    lo = (r1 - mid.astype(F32)).astype(BF16)
    return jnp.concatenate([hi, mid, lo], axis=1)


def _mlstm_dir(d, q_ref, k_ref, v_ref, if_ref, o_ref, c_aug, m_prev, negsel_ref):
    ln = ML_CHUNK
    scale = HEAD_DIM ** -0.5
    t_in = if_ref[...]
    logf = _log_sigmoid(t_in)
    ri = lax.broadcasted_iota(jnp.int32, (ln, ln), 0)
    ci = lax.broadcasted_iota(jnp.int32, (ln, ln), 1)
    mask = (ci <= ri) if d == 0 else (ci >= ri)
    tri = jnp.where(mask, 1.0, 0.0).astype(BF16)
    fs = _split3(logf)
    bcol = _dot(tri, fs[:, :LANE]) + _dot(tri, fs[:, LANE:2 * LANE]) + _dot(tri, fs[:, 2 * LANE:])
    b = pltpu.roll(bcol, LANE - 2 * ML_HEADS, 1)
    a = t_in - b
    row = lax.broadcasted_iota(jnp.int32, (ln, LANE), 0)
    cm, sh = a, 1
    while sh < ln:
        if d == 0:
            cm = jnp.maximum(cm, jnp.where(row >= sh, pltpu.roll(cm, sh, 0), NEG))
        else:
            cm = jnp.maximum(cm, jnp.where(row < ln - sh, pltpu.roll(cm, ln - sh, 0), NEG))
        sh *= 2
    big_m = jnp.maximum(m_prev, cm)
    xs = _split3(big_m - m_prev)
    ys = _split3(b + big_m)
    at = (a - m_prev).T
    last = ln - 1 if d == 0 else 0
    m_last = big_m[last:last + 1, :]
    m_new = b[last:last + 1, :] + m_last
    decay = jnp.exp(m_prev - m_last)
    wt = jnp.exp(a - m_last).T
    q, k, v = q_ref[...], k_ref[...], v_ref[...]
    new_c = []
    for h in range(ML_HEADS):
        c = d * ML_HEADS + h
        hsl = slice(h * LANE, (h + 1) * LANE)
        neg_x = _dot(xs, negsel_ref[c])
        neg_y = _dot(ys, negsel_ref[c])
        w_intra = jnp.exp(jnp.where(mask, at[c:c + 1, :] + neg_x, NEG))
        kh = k[:, hsl] * scale
        qb, kb, vb = q[:, hsl].astype(BF16), kh.astype(BF16), v[:, hsl].astype(BF16)
        s = _dot_nt(qb, kb) * w_intra
        c_prev = c_aug[c]
        nd = _dot(s.astype(BF16), vb) + jnp.exp(neg_x) * _dot(qb, c_prev.astype(BF16))
        den = pltpu.roll(nd, HEAD_DIM, 1)
        hout = nd / jnp.maximum(jnp.abs(den), jnp.exp(neg_y))
        o_ref[:, h * HEAD_DIM:(h + 1) * HEAD_DIM] = hout[:, :HEAD_DIM]
        new_c.append(decay[:, c:c + 1] * c_prev + _dot((kh.T * wt[c:c + 1, :]).astype(BF16), vb))
    return new_c, m_new


def _mlstm_kernel(qf, kf, vf, iff, qb, kb, vb, ifb, c0, m0, negsel, hf, hb, c_out, m_out, c_s, m_s, *, nc):
    c = pl.program_id(1)
    nst = 2 * ML_HEADS

    @pl.when(c == 0)
    def _():
        c_s[...] = c0[...]
        m_s[...] = m0[...]

    c_aug = [c_s[i] for i in range(nst)]
    m_prev = m_s[...]
    cf, mf = _mlstm_dir(0, qf, kf, vf, iff, hf, c_aug, m_prev, negsel)
    cb, mb = _mlstm_dir(1, qb, kb, vb, ifb, hb, c_aug, m_prev, negsel)
    for i, cn in enumerate(cf + cb):
        c_s[i] = cn
    lane = lax.broadcasted_iota(jnp.int32, (1, LANE), 1)
    m_s[...] = jnp.where(lane < ML_HEADS, mf, mb)

    @pl.when(c == nc - 1)
    def _():
        c_out[...] = c_s[...]
        m_out[...] = m_s[...]


def _mlstm(q, k, v, gif, c0, n0, m0, nb, t):
    nc = t // ML_CHUNK
    nst = 2 * ML_HEADS
    hd = HEAD_DIM
    qkw = ML_HEADS * LANE
    c_aug0 = jnp.concatenate([c0.reshape(nb, nst, hd, hd),
                              jnp.broadcast_to(n0.reshape(nb, nst, hd, 1), (nb, nst, hd, hd))], axis=-1)
    c_aug0 = jnp.pad(c_aug0, ((0, 0), (0, 0), (0, LANE - hd), (0, 0)))
    m_row0 = jnp.pad(m0.reshape(nb, 1, nst), ((0, 0), (0, 0), (0, LANE - nst)))
    sel = np.zeros((nst, 3 * LANE, LANE), np.float32)
    for ch in range(nst):
        sel[ch, [ch, LANE + ch, 2 * LANE + ch], :] = -1.0
    fw = lambda wd: pl.BlockSpec((ML_CHUNK, wd), lambda b, c: (b * nc + c, 0))
    bw = lambda wd: pl.BlockSpec((ML_CHUNK, wd), lambda b, c: (b * nc + nc - 1 - c, 0))
    st = lambda shp: pl.BlockSpec((None,) + shp, lambda b, c: (b,) + (0,) * len(shp))
    shapes = ((nst, LANE, LANE), (1, LANE))
    hf, hb, c_fin, m_fin = pl.pallas_call(
        functools.partial(_mlstm_kernel, nc=nc), grid=(nb, nc),
        in_specs=[fw(qkw), fw(qkw), fw(qkw), fw(LANE), bw(qkw), bw(qkw), bw(qkw), bw(LANE)]
                 + [st(s) for s in shapes] + [pl.BlockSpec(sel.shape, lambda b, c: (0, 0, 0))],
        out_specs=[fw(BRANCH_W), bw(BRANCH_W)] + [st(s) for s in shapes],
        out_shape=[jax.ShapeDtypeStruct((nb * t, BRANCH_W), F32)] * 2
                  + [jax.ShapeDtypeStruct((nb,) + s, F32) for s in shapes],
        scratch_shapes=[pltpu.VMEM(s, F32) for s in shapes],
        compiler_params=_cp(("parallel", "arbitrary")), name="mlstm_scan",
    )(q, k, v, gif, q, k, v, gif, c_aug0, m_row0, jnp.asarray(sel, BF16))
    c_new = c_fin[:, :, :hd, :hd].reshape(nb, 2, ML_HEADS, hd, hd)
    n_new = c_fin[:, :, :hd, hd].reshape(nb, 2, ML_HEADS, hd)
    m_new = m_fin[:, 0, :nst].reshape(nb, 2, ML_HEADS)
    return hf, hb, c_new, n_new, m_new


def _prep_weights(w_in, b_in, mla_q_norm, w_uq, w_uk, w_uv, w_branch, w_out, peer_wq, peer_keys, peer_u, peer_v):
    offs = np.cumsum((0,) + IN_SPLITS)
    seg = lambda a, i: a[..., offs[i]:offs[i + 1]]
    padr = lambda a, wd, lead=0: jnp.pad(a, [(0, 0)] * (a.ndim - 1) + [(lead, wd - a.shape[-1] - lead)])

    def headpad(a, fill):
        a4 = a.reshape(a.shape[:-1] + (ML_HEADS, HEAD_DIM))
        a4 = jnp.pad(a4, [(0, 0)] * (a4.ndim - 1) + [(0, LANE - HEAD_DIM)], constant_values=fill)
        return a4.reshape(a.shape[:-1] + (ML_HEADS * LANE,))

    def layout(a, is_bias):
        ml_if = padr(jnp.concatenate([seg(a, 7), seg(a, 8)], axis=-1), LANE)
        ml = [headpad(seg(a, 3), 0.0), headpad(seg(a, 4), 0.0), headpad(seg(a, 5), 1.0 if is_bias else 0.0)]
        parts = [seg(a, i) for i in range(3)] + ml + [seg(a, 6), ml_if, seg(a, 9), seg(a, 10), seg(a, 11),
                                                       padr(seg(a, 12), 256), seg(a, 13),
                                                       padr(seg(a, 14), LANE, KROPE_SLICE.start)]
        return jnp.concatenate(parts, axis=-1)

    nl = w_in.shape[0]
    w_uq_p = padr(w_uq.reshape(nl, MLA_Q_RANK, MLA_HEADS, MLA_NOPE + MLA_ROPE), LANE).reshape(nl, MLA_Q_RANK, -1)
    w_uq_p = jnp.pad(w_uq_p, ((0, 0), (0, 256 - MLA_Q_RANK), (0, 0)))
    w_uk_p = padr(w_uk.reshape(nl, MLA_KV_RANK, MLA_HEADS, MLA_NOPE), LANE).reshape(nl, MLA_KV_RANK, -1)
    return dict(
        w_ab=layout(w_in, False).astype(BF16), b_ab=layout(b_in, True),
        w_c=seg(w_in, 15).astype(BF16), b_c=seg(b_in, 15),
        q_norm=padr(mla_q_norm, 256), w_uq=w_uq_p.astype(BF16),
        w_ukv=jnp.concatenate([w_uk_p, w_uv], axis=-1).astype(BF16),
        w_branch=w_branch.astype(BF16), w_out=w_out.astype(BF16), peer_wq=peer_wq.astype(BF16),
        peer_keys=peer_keys.astype(BF16), peer_u=peer_u.astype(BF16),
        peer_vt=jnp.swapaxes(peer_v.reshape(nl, -1, PEER_EC, peer_v.shape[-1]), 2, 3).astype(BF16),
    )


def _project(x, wl, l, norm_g, sc, sh, rows_per_mod):
    outs = _linear(x, wl["w_ab"][l], [wd for _, wd in SEGS_AB], bias=wl["b_ab"][l], norm_g=norm_g,
                   mod=(sc, sh), rows_per_mod=rows_per_mod, tm=256, vmem_mb=48, name="in_proj_ab")
    p = {name: o for (name, _), o in zip(SEGS_AB, outs)}
    (p["gate"],) = _linear(x, wl["w_c"][l], [N_BRANCH * D_MODEL], bias=wl["b_c"][l], norm_g=norm_g,
                           mod=(sc, sh), rows_per_mod=rows_per_mod, tm=256, vmem_mb=48, name="in_proj_gate")
    (qm,) = _linear(p["cq"], wl["w_uq"][l], [MLA_QK_W], norm_g=wl["q_norm"][l], k_valid=MLA_Q_RANK, tm=512,
                    name="mla_q_up")
    return p, qm


def kernel(x_prompt, x_sample, cache_na_k, cache_na_v, state_mlstm_C, state_mlstm_n, state_mlstm_m,
           cache_swa_k, cache_swa_v, cache_mla_ckv, cache_mla_krope, c, c_ctx, w_mod, b_mod, norm1_g, norm2_g,
           w_in, b_in, na_rpb, sw_sink, mla_q_norm, w_uq, mla_kv_norm, w_uk, w_uv, w_branch, w_out, peer_wq,
           peer_keys, peer_u, peer_v, final_norm_g):
    nbp, tp, d = x_prompt.shape
    nbs, ts, _ = x_sample.shape
    nl = w_in.shape[0]
    lc = cache_na_k.shape[2]
    nst = 2 * ML_HEADS
    wl = _prep_weights(w_in, b_in, mla_q_norm, w_uq, w_uk, w_uv, w_branch, w_out, peer_wq, peer_keys,
                       peer_u, peer_v)

    n_cond = 1 + nbs
    cond = jnp.zeros((16, d), F32).at[0].set(c_ctx).at[1:n_cond].set(c)
    mods = _modulation(cond, w_mod, b_mod).reshape(nl, 16, 6, 1, d)

    tabs_swq = _rope_tables(ts, SW_HEADS, HEAD_DIM)
    tabs_swk = _rope_tables(ts, SW_KV_HEADS, HEAD_DIM)
    tabs_mlaq = _rope_tables(ts, MLA_HEADS, MLA_ROPE, lead=MLA_NOPE, width=LANE)
    tabs_mlak = _rope_tables(ts, 1, MLA_ROPE, lead=MLA_NOPE, width=LANE)
    na_bias = _na_bias_tables(na_rpb, ts // GRID_W)
    krope_c = jnp.pad(cache_mla_krope, ((0, 0), (0, 0), (0, 0), (KROPE_SLICE.start, LANE - KROPE_SLICE.stop)))

    xp = x_prompt.reshape(nbp * tp, d)
    xs = x_sample.reshape(nbs * ts, d)
    zeros_c = jnp.zeros((nbp, 2, ML_HEADS, HEAD_DIM, HEAD_DIM), F32)
    zeros_n = jnp.zeros((nbp, 2, ML_HEADS, HEAD_DIM), F32)
    zeros_m = jnp.zeros((nbp, 2, ML_HEADS), F32)
    per_layer = tuple([] for _ in range(9))

    for l in range(nl):
        mod_p = [mods[l, 0:1, i] for i in range(6)]
        mod_s = [mods[l, 1:n_cond, i] for i in range(6)]
        sink = sw_sink[l]

        p, qm = _project(xp, wl, l, norm1_g[l], mod_p[1], mod_p[0], nbp * tp)
        ckv_n, kn, vm = _linear(p["ckv"], wl["w_ukv"][l], [MLA_QK_W, 256], norm_g=mla_kv_norm[l],
                                emit_normed=True, tm=512, name="mla_kv_up")
        o_na, o_sw, o_mla = _ctx_attention(p, qm, kn, vm, sink, nbp, tp)
        hf, hb, c_new, n_new, m_new = _mlstm(p["ml_q"], p["ml_k"], p["ml_v"], p["ml_if"], zeros_c, zeros_n,
                                             zeros_m, nbp, tp)
        xp = _merge(o_na, hf, hb, p["ml_o"], o_sw, o_mla, p["gate"], wl["w_branch"][l], wl["w_out"][l], xp,
                    mod_p[2], nbp * tp)
        xp = _peer(xp, norm2_g[l], mod_p[4], mod_p[3], mod_p[5], wl["peer_wq"][l], wl["peer_keys"][l],
                   wl["peer_u"][l], wl["peer_vt"][l], nbp * tp)
        ctx_t = (p["na_k"].reshape(nbp, tp, NA_HEADS, HEAD_DIM), p["na_v"].reshape(nbp, tp, NA_HEADS, HEAD_DIM),
                 c_new, n_new, m_new,
                 p["sw_k"].reshape(nbp, tp, SW_KV_HEADS, HEAD_DIM), p["sw_v"].reshape(nbp, tp, SW_KV_HEADS, HEAD_DIM),
                 ckv_n.reshape(nbp, tp, MLA_KV_RANK), p["krope"][:, KROPE_SLICE].reshape(nbp, tp, MLA_ROPE))
        for i, a in enumerate(ctx_t):
            per_layer[i].append(a)

        p, qm = _project(xs, wl, l, norm1_g[l], mod_s[1], mod_s[0], ts)
        _, kn, vt = _linear(p["ckv"], wl["w_ukv"][l], [MLA_QK_W, 256], norm_g=mla_kv_norm[l], emit_normed=True,
                            transpose_last=True, tm=512, name="mla_kv_up_t")
        knc, vtc = _linear(cache_mla_ckv[:, l].reshape(nbs * lc, MLA_KV_RANK), wl["w_ukv"][l], [MLA_QK_W, 256],
                           transpose_last=True, tm=512, name="mla_kv_up_cache")
        o_na = _na_attention(p["na_q"], p["na_k"], p["na_v"], cache_na_k[:, l].reshape(nbs, lc, BRANCH_W),
                             cache_na_v[:, l].reshape(nbs, lc, BRANCH_W), na_bias[l], nbs, ts)
        o_sw = _sw_attention(p["sw_q"], p["sw_k"], p["sw_v"],
                             cache_swa_k[:, l].reshape(nbs, lc, SW_KV_HEADS * HEAD_DIM),
                             cache_swa_v[:, l].reshape(nbs, lc, SW_KV_HEADS * HEAD_DIM), sink, tabs_swq, tabs_swk,
                             nbs, ts)
        o_mla = _mla_attention(qm, kn, p["krope"], vt, knc, krope_c[:, l], vtc, tabs_mlaq, tabs_mlak, nbs, ts)
        hf, hb, _, _, _ = _mlstm(p["ml_q"], p["ml_k"], p["ml_v"], p["ml_if"], state_mlstm_C[:, l],
                                 state_mlstm_n[:, l], state_mlstm_m[:, l], nbs, ts)
        xs = _merge(o_na, hf, hb, p["ml_o"], o_sw, o_mla, p["gate"], wl["w_branch"][l], wl["w_out"][l], xs,
                    mod_s[2], ts)
        xs = _peer(xs, norm2_g[l], mod_s[4], mod_s[3], mod_s[5], wl["peer_wq"][l], wl["peer_keys"][l],
                   wl["peer_u"][l], wl["peer_vt"][l], ts)

    y_prompt = _final_norm(xp, final_norm_g).reshape(nbp, tp, d)
    y_sample = _final_norm(xs, final_norm_g).reshape(nbs, ts, d)
    return (y_prompt, y_sample) + tuple(jnp.stack(s, axis=1) for s in per_layer)
```

```python
import functools

import numpy as np
import jax
import jax.numpy as jnp
from jax import lax
from jax.experimental import pallas as pl
from jax.experimental.pallas import tpu as pltpu

F32 = jnp.float32
BF16 = jnp.bfloat16

D_MODEL = 1024
GRID_W = 64
HEAD_DIM = 64
N_BRANCH = 4
BRANCH_W = 256
NA_HEADS = 4
NA_KR_MAX = 8
NA_KC = 16
ML_HEADS = 4
ML_CHUNK = 128
SW_HEADS = 4
SW_KV_HEADS = 2
SW_GROUP = SW_HEADS // SW_KV_HEADS
SW_WINDOW = 128
MLA_HEADS = 4
MLA_Q_RANK = 192
MLA_KV_RANK = 128
MLA_NOPE = 64
MLA_ROPE = 32
MLA_V = 64
MLA_SCALE = (MLA_NOPE + MLA_ROPE) ** -0.5
PEER_HEADS = 8
PEER_KEY_DIM = 64
PEER_N_KEYS = 128
PEER_TOPK = 16
PEER_EC = 512
ROPE_BASE = 10000.0
RMS_EPS = 1e-6
NEG = -1e30
IN_SPLITS = (256, 256, 256, 256, 256, 256, 256, 8, 8, 256, 128, 128, 192, 128, 32, 4096)

LANE = 128
MLA_QK_W = MLA_HEADS * LANE

SEGS_AB = (("na_q", 256), ("na_k", 256), ("na_v", 256), ("ml_q", 512), ("ml_k", 512), ("ml_v", 512),
           ("ml_o", 256), ("ml_if", 128), ("sw_q", 256), ("sw_k", 128), ("sw_v", 128), ("cq", 256),
           ("ckv", 128), ("krope", 128))
KROPE_SLICE = slice(MLA_NOPE, MLA_NOPE + MLA_ROPE)


def _cp(sem, vmem_mb=None):
    kw = dict(dimension_semantics=sem)
    if vmem_mb is not None:
        kw["vmem_limit_bytes"] = vmem_mb * 1024 * 1024
    return pltpu.CompilerParams(**kw)


def _dot(a, b):
    return jnp.dot(a, b, preferred_element_type=F32)


def _dot_nt(a, b):
    return lax.dot_general(a, b, (((1,), (1,)), ((), ())), preferred_element_type=F32)


def _rms(x, inv_n):
    return x * lax.rsqrt(jnp.sum(x * x, axis=-1, keepdims=True) * inv_n + RMS_EPS)


def _sigmoid(x):
    return 1.0 / (1.0 + jnp.exp(-x))


def _mod_kernel(c_ref, w_ref, b_ref, o_ref):
    c = c_ref[...]
    s = c * _sigmoid(c)
    o_ref[...] = _dot(s.astype(BF16), w_ref[...].astype(BF16)) + b_ref[...]


def _modulation(cond, w_mod, b_mod):
    nl, d, n = w_mod.shape
    r = cond.shape[0]
    tn = 1024
    return pl.pallas_call(
        _mod_kernel,
        grid=(nl, n // tn),
        in_specs=[pl.BlockSpec((r, d), lambda l, j: (0, 0)),
                  pl.BlockSpec((None, d, tn), lambda l, j: (l, 0, j)),
                  pl.BlockSpec((None, 1, tn), lambda l, j: (l, 0, j))],
        out_specs=pl.BlockSpec((None, r, tn), lambda l, j: (l, 0, j)),
        out_shape=jax.ShapeDtypeStruct((nl, r, n), F32),
        compiler_params=_cp(("parallel", "parallel")),
        name="adaln_modulation",
    )(cond, w_mod, b_mod.reshape(nl, 1, n))


def _linear_kernel(*refs, segs, has_norm, has_mod, has_bias, emit_normed, transpose_last, inv_k):
    it = iter(refs)
    x_ref = next(it)
    g_ref = next(it) if has_norm else None
    sc_ref = next(it) if has_mod else None
    sh_ref = next(it) if has_mod else None
    w_ref = next(it)
    b_ref = next(it) if has_bias else None
    outs = list(it)
    x = x_ref[...]
    if has_norm:
        x = _rms(x, inv_k) * g_ref[...]
    if has_mod:
        x = x * (1.0 + sc_ref[0]) + sh_ref[0]
    if emit_normed:
        outs[0][...] = x
        outs = outs[1:]
    xb = x.astype(BF16)
    for idx, ((s, wd), o) in enumerate(zip(segs, outs)):
        y = _dot(xb, w_ref[:, s:s + wd])
        if has_bias:
            y = y + b_ref[:, s:s + wd]
        o[...] = y.T if (transpose_last and idx == len(segs) - 1) else y


def _linear(x, w, widths, *, bias=None, norm_g=None, k_valid=None, mod=None, rows_per_mod=None,
            emit_normed=False, transpose_last=False, tm=256, vmem_mb=None, name="linear"):
    n, k = x.shape
    tm = min(tm, n)
    segs, s = [], 0
    for wd in widths:
        segs.append((s, wd))
        s += wd
    assert s == w.shape[1] and n % tm == 0
    args = [x]
    in_specs = [pl.BlockSpec((tm, k), lambda i: (i, 0))]
    if norm_g is not None:
        args.append(norm_g.reshape(1, k))
        in_specs.append(pl.BlockSpec((1, k), lambda i: (0, 0)))
    if mod is not None:
        assert rows_per_mod % tm == 0
        for m in mod:
            args.append(m)
            in_specs.append(pl.BlockSpec((1, 1, k), lambda i: ((i * tm) // rows_per_mod, 0, 0)))
    args.append(w)
    in_specs.append(pl.BlockSpec(w.shape, lambda i: (0, 0)))
    if bias is not None:
        args.append(bias.reshape(1, -1))
        in_specs.append(pl.BlockSpec((1, w.shape[1]), lambda i: (0, 0)))
    out_shapes, out_specs = [], []
    if emit_normed:
        out_shapes.append(jax.ShapeDtypeStruct((n, k), F32))
        out_specs.append(pl.BlockSpec((tm, k), lambda i: (i, 0)))
    for idx, wd in enumerate(widths):
        if transpose_last and idx == len(widths) - 1:
            out_shapes.append(jax.ShapeDtypeStruct((wd, n), F32))
            out_specs.append(pl.BlockSpec((wd, tm), lambda i: (0, i)))
        else:
            out_shapes.append(jax.ShapeDtypeStruct((n, wd), F32))
            out_specs.append(pl.BlockSpec((tm, wd), lambda i: (i, 0)))
    kern = functools.partial(_linear_kernel, segs=tuple(segs), has_norm=norm_g is not None,
                             has_mod=mod is not None, has_bias=bias is not None, emit_normed=emit_normed,
                             transpose_last=transpose_last, inv_k=1.0 / (k_valid or k))
    return pl.pallas_call(kern, grid=(n // tm,), in_specs=in_specs, out_specs=out_specs,
                          out_shape=out_shapes, compiler_params=_cp(("parallel",), vmem_mb), name=name)(*args)


def _merge_kernel(ona, hf, hb, mlo, osw, omla, gate, wb, wo, x, g1, bd, o):
    h = hf[...] + hb[...]
    hsq = h * h
    hi = hsq.astype(BF16)
    lo = (hsq - hi.astype(F32)).astype(BF16)
    ms = (_dot(hi, bd[...]) + _dot(lo, bd[...])) * (1.0 / HEAD_DIM)
    oml = h * lax.rsqrt(ms + RMS_EPS) * _sigmoid(mlo[...])
    branches = (ona[...], oml, osw[...], omla[...])
    acc = None
    for n in range(N_BRANCH):
        y = _sigmoid(gate[:, n * D_MODEL:(n + 1) * D_MODEL]) * _dot(branches[n].astype(BF16), wb[n])
        acc = y if acc is None else acc + y
    o[...] = x[...] + g1[0] * _dot(acc.astype(BF16), wo[...])


def _merge(ona, hf, hb, mlo, osw, omla, gate, wb, wo, x, g1, rows_per_mod, tm=256):
    n = x.shape[0]
    tm = min(tm, n)
    bd = jnp.asarray(np.kron(np.eye(BRANCH_W // HEAD_DIM), np.ones((HEAD_DIM, HEAD_DIM))), BF16)
    row = lambda wd: pl.BlockSpec((tm, wd), lambda i: (i, 0))
    return pl.pallas_call(
        _merge_kernel,
        grid=(n // tm,),
        in_specs=[row(BRANCH_W)] * 6 + [row(N_BRANCH * D_MODEL),
                  pl.BlockSpec(wb.shape, lambda i: (0, 0, 0)),
                  pl.BlockSpec(wo.shape, lambda i: (0, 0)),
                  row(D_MODEL),
                  pl.BlockSpec((1, 1, D_MODEL), lambda i: ((i * tm) // rows_per_mod, 0, 0)),
                  pl.BlockSpec(bd.shape, lambda i: (0, 0))],
        out_specs=row(D_MODEL),
        out_shape=jax.ShapeDtypeStruct((n, D_MODEL), F32),
        compiler_params=_cp(("parallel",), 48),
        name="branch_merge",
    )(ona, hf, hb, mlo, osw, omla, gate, wb, wo, x, g1, bd)


def _top_rows(s, k):
    cur, rows = s, []
    for _ in range(k):
        mx = jnp.max(cur, axis=0, keepdims=True)
        rows.append(mx)
        cur = jnp.where(cur == mx, NEG, cur)
    return rows


def _peer_select(s1, s2):
    rows8 = lax.broadcasted_iota(jnp.int32, (8, 1), 0)
    a1, a2 = _top_rows(s1, PEER_TOPK), _top_rows(s2, PEER_TOPK)
    a1_16 = jnp.concatenate(a1, axis=0)
    a1_8 = a1_16[:8]
    cands = [a1_16 + a2[0], a1_8 + a2[1]]
    for k2 in range(2, 8):
        cands.append(jnp.where(rows8 < PEER_TOPK // (k2 + 1), a1_8 + a2[k2], NEG))
    cands.append(a1[0] + jnp.concatenate(a2[8:], axis=0))
    cand = jnp.concatenate(cands, axis=0)
    top = _top_rows(cand, PEER_TOPK + 1)
    tau = 0.5 * (top[PEER_TOPK - 1] + top[PEER_TOPK])
    z = jnp.sum(jnp.where(cand >= tau, jnp.exp(cand - top[0]), 0.0), axis=0, keepdims=True)
    return tau - s1, jnp.exp(s1 - a1[0]), jnp.exp(s2 - a2[0]) * (0.5 / z)


def _peer_kernel(x_ref, ng_ref, sc_ref, sh_ref, g2_ref, wq_ref, keys_ref, u_ref, vt_ref, o_ref,
                 h2t_ref, acc_ref, t1_ref, e1_ref, s2_ref, e2_ref, ht_ref, z_ref, *,
                 ec, n_chunks, tm):
    s = pl.program_id(1)
    n_i1 = ec // PEER_N_KEYS
    n_tc = tm // LANE
    half = n_tc // 2
    hw = tm // 2

    @pl.when(s == 0)
    def _():
        x = x_ref[...]
        h2 = _rms(x, 1.0 / D_MODEL) * ng_ref[...] * (1.0 + sc_ref[0]) + sh_ref[0]
        h2t = h2.T.astype(BF16)
        for nh in range(2):
            h2t_ref[nh] = h2t[:, nh * hw:(nh + 1) * hw]
        qt = _dot(h2.astype(BF16), wq_ref[...]).T.astype(BF16)
        for h in range(PEER_HEADS):
            r0 = h * 2 * PEER_KEY_DIM
            s1 = _dot(keys_ref[h, 0], qt[r0:r0 + PEER_KEY_DIM, :])
            s2 = _dot(keys_ref[h, 1], qt[r0 + PEER_KEY_DIM:r0 + 2 * PEER_KEY_DIM, :])
            for tc in range(n_tc):
                cols = slice(tc * LANE, (tc + 1) * LANE)
                t1, e1, e2 = _peer_select(s1[:, cols], s2[:, cols])
                t1_ref[h, tc] = t1
                e1_ref[h, tc] = e1
                s2_ref[h, tc] = s2[:, cols]
                e2_ref[h, tc] = e2
        for r in (ht_ref, z_ref, acc_ref):
            r[...] = jnp.zeros_like(r)

    cb = jnp.clip(s - 1, 0, n_chunks - 1)
    orows = D_MODEL // n_i1
    n_rh = 4
    rh_rows = PEER_N_KEYS // n_rh
    n_iter = 2 * n_i1
    upi = n_tc * n_rh // n_iter

    sw = s % 2
    sr = 1 - sw
    th_rows = [[[t1_ref[h, tc, pl.ds(cb * n_i1 + j4, 1), :] for j4 in range(n_i1)] for h in range(PEER_HEADS)]
               for tc in range(n_tc)]
    e1_rows = [[[e1_ref[h, tc, pl.ds(cb * n_i1 + j4, 1), :] for j4 in range(n_i1)] for h in range(PEER_HEADS)]
               for tc in range(n_tc)]
    for i in range(n_iter):
        j, nh = divmod(i, 2)
        rows = slice(j * PEER_N_KEYS, (j + 1) * PEER_N_KEYS)
        hj = _dot(u_ref[rows, :], h2t_ref[nh])
        for k in range(half):
            ht_ref[sw, nh * half + k, rows, :] = hj[:, k * LANE:(k + 1) * LANE]
        osl = slice(j * orows, (j + 1) * orows)
        zh = jnp.concatenate([z_ref[sw, nh * half + k] for k in range(half)], axis=1)
        acc_ref[nh, osl, :] += _dot(vt_ref[osl, :], zh)
        for k in range(upi):
            tc, rh = divmod(i * upi + k, n_rh)
            r2 = slice(rh * rh_rows, (rh + 1) * rh_rows)
            ws = [None] * n_i1
            for h in range(PEER_HEADS):
                s2t, e2t = s2_ref[h, tc, r2, :], e2_ref[h, tc, r2, :]
                for j4 in range(n_i1):
                    term = jnp.where(s2t >= th_rows[tc][h][j4], e2t, 0.0) * e1_rows[tc][h][j4]
                    ws[j4] = term if ws[j4] is None else ws[j4] + term
            for j4 in range(n_i1):
                rr = slice(j4 * PEER_N_KEYS + rh * rh_rows, j4 * PEER_N_KEYS + (rh + 1) * rh_rows)
                xh = ht_ref[sr, tc, rr, :]
                t = jnp.tanh(xh * (0.7978845608028654 + 0.035677408136300125 * (xh * xh)))
                z_ref[sr, tc, rr, :] = (ws[j4] * (xh + xh * t)).astype(BF16)

    @pl.when(s == n_chunks + 1)
    def _():
        for nh in range(2):
            rs = slice(nh * hw, (nh + 1) * hw)
            o_ref[rs, :] = x_ref[rs, :] + g2_ref[0] * acc_ref[nh].T


def _peer(x, ng, sc, sh, g2, wq, keys, u, vt, rows_per_mod, tm=512):
    n = x.shape[0]
    tm = min(tm, n)
    n_chunks, _, ec = vt.shape
    last = n_chunks - 1
    n_tc = tm // LANE
    modspec = pl.BlockSpec((1, 1, D_MODEL), lambda i, s: ((i * tm) // rows_per_mod, 0, 0))
    kern = functools.partial(_peer_kernel, ec=ec, n_chunks=n_chunks, tm=tm)
    tile_buf = pltpu.VMEM((PEER_HEADS, n_tc, PEER_N_KEYS, LANE), F32)
    return pl.pallas_call(
        kern,
        grid=(n // tm, n_chunks + 2),
        in_specs=[pl.BlockSpec((tm, D_MODEL), lambda i, s: (i, 0)),
                  pl.BlockSpec((1, D_MODEL), lambda i, s: (0, 0)),
                  modspec, modspec, modspec,
                  pl.BlockSpec(wq.shape, lambda i, s: (0, 0)),
                  pl.BlockSpec(keys.shape, lambda i, s: (0, 0, 0, 0)),
                  pl.BlockSpec((ec, D_MODEL), lambda i, s: (jnp.minimum(s, last), 0)),
                  pl.BlockSpec((None, D_MODEL, ec), lambda i, s: (jnp.clip(s - 2, 0, last), 0, 0))],
        out_specs=pl.BlockSpec((tm, D_MODEL), lambda i, s: (i, 0)),
        out_shape=jax.ShapeDtypeStruct((n, D_MODEL), F32),
        scratch_shapes=[pltpu.VMEM((2, D_MODEL, tm // 2), BF16),
                        pltpu.VMEM((2, D_MODEL, tm // 2), F32),
                        tile_buf, tile_buf, tile_buf, tile_buf,
                        pltpu.VMEM((2, n_tc, ec, LANE), F32), pltpu.VMEM((2, n_tc, ec, LANE), BF16)],
        compiler_params=_cp(("parallel", "arbitrary"), 56),
        name="peer_ffn",
    )(x, ng.reshape(1, D_MODEL), sc, sh, g2, wq, keys, u, vt)


def _final_norm_kernel(x_ref, g_ref, o_ref):
    o_ref[...] = _rms(x_ref[...], 1.0 / D_MODEL) * g_ref[...]


def _final_norm(x, g, tm=512):
    n = x.shape[0]
    tm = min(tm, n)
    return pl.pallas_call(
        _final_norm_kernel, grid=(n // tm,),
        in_specs=[pl.BlockSpec((tm, D_MODEL), lambda i: (i, 0)), pl.BlockSpec((1, D_MODEL), lambda i: (0, 0))],
        out_specs=pl.BlockSpec((tm, D_MODEL), lambda i: (i, 0)),
        out_shape=jax.ShapeDtypeStruct((n, D_MODEL), F32),
        compiler_params=_cp(("parallel",)), name="final_norm")(x, g.reshape(1, D_MODEL))


def _rope_tables(t, n_heads, dh, lead=0, width=None):
    width = width or dh
    pos = jnp.arange(t)
    sec, half = dh // 2, dh // 4
    freqs = ROPE_BASE ** (-jnp.arange(half, dtype=F32) / half)
    first = jnp.asarray((np.arange(dh) % sec) < half)
    ang_r = (pos // GRID_W).astype(F32)[:, None] * freqs[None, :]
    ang_c = (pos % GRID_W).astype(F32)[:, None] * freqs[None, :]
    ang = jnp.concatenate([ang_r, ang_r, ang_c, ang_c], axis=1)
    cos, sin = jnp.cos(ang), jnp.sin(ang)
    sa = jnp.where(first[None, :], -sin, 0.0)
    sb = jnp.where(first[None, :], 0.0, sin)
    padw = ((0, 0), (lead, width - dh - lead))
    place = lambda a, fill: jnp.tile(jnp.pad(a, padw, constant_values=fill), (1, n_heads))
    return place(cos, 1.0), place(sa, 0.0), place(sb, 0.0)


def _rope(x, cos, sa, sb, half):
    w = x.shape[-1]
    return x * cos + pltpu.roll(x, w - half, 1) * sa + pltpu.roll(x, half, 1) * sb


def _softmax_pv(blocks, sink=None):
    m = None
    for s, _ in blocks:
        bm = jnp.max(s, axis=1, keepdims=True)
        m = bm if m is None else jnp.maximum(m, bm)
    if sink is not None:
        m = jnp.maximum(m, sink)
    l, acc = None, None
    for s, v in blocks:
        p = jnp.exp(s - m)
        bl = jnp.sum(p, axis=1, keepdims=True)
        pv = _dot(p.astype(BF16), v)
        l = bl if l is None else l + bl
        acc = pv if acc is None else acc + pv
    if sink is not None:
        l = l + jnp.exp(sink - m)
    return acc / l


def _ctx_attn_kernel(naq, nak, nav, swq, swk, swv, qm, kn, kr, vm, sink_ref, o_na, o_sw, o_mla):
    hd = HEAD_DIM
    scale = hd ** -0.5
    q, k, v = naq[...].astype(BF16), nak[...].astype(BF16), nav[...].astype(BF16)
    for h in range(NA_HEADS):
        hs = slice(h * hd, (h + 1) * hd)
        o_na[:, hs] = _softmax_pv([(_dot_nt(q[:, hs], k[:, hs]) * scale, v[:, hs])])
    q, k, v = swq[...].astype(BF16), swk[...].astype(BF16), swv[...].astype(BF16)
    for h in range(SW_HEADS):
        g = h // SW_GROUP
        hs, gs = slice(h * hd, (h + 1) * hd), slice(g * hd, (g + 1) * hd)
        o_sw[:, hs] = _softmax_pv([(_dot_nt(q[:, hs], k[:, gs]) * scale, v[:, gs])], sink=sink_ref[h])
    q, v = qm[...].astype(BF16), vm[...].astype(BF16)
    krv = kr[...]
    for h in range(MLA_HEADS):
        ks = slice(h * LANE, (h + 1) * LANE)
        kcat = (kn[:, ks] + krv).astype(BF16)
        s = _dot_nt(q[:, ks], kcat) * MLA_SCALE
        o_mla[:, h * MLA_V:(h + 1) * MLA_V] = _softmax_pv([(s, v[:, h * MLA_V:(h + 1) * MLA_V])])


def _ctx_attention(p, qm, kn, vm, sink, nb, t):
    row = lambda wd: pl.BlockSpec((t, wd), lambda b: (b, 0))
    outs = pl.pallas_call(
        _ctx_attn_kernel, grid=(nb,),
        in_specs=[row(256), row(256), row(256), row(256), row(128), row(128), row(MLA_QK_W), row(MLA_QK_W),
                  row(LANE), row(256), pl.BlockSpec(memory_space=pltpu.SMEM)],
        out_specs=[row(256)] * 3,
        out_shape=[jax.ShapeDtypeStruct((nb * t, BRANCH_W), F32)] * 3,
        compiler_params=_cp(("parallel",)), name="ctx_attention",
    )(p["na_q"], p["na_k"], p["na_v"], p["sw_q"], p["sw_k"], p["sw_v"], qm, kn, p["krope"], vm, sink)
    return outs


NA_TQ = 256


def _na_bias_tables(rpb, rows_n):
    nr, nc = 2 * NA_KR_MAX - 1, 2 * NA_KC - 1
    p_sel = np.zeros((3, 4, 12, nr), np.float32)
    ok_r = np.zeros((3, 4, 12), bool)
    for ti, j in enumerate((0, 1, rows_n // 4 - 1)):
        for a in range(4):
            r = 4 * j + a
            r0 = min(max(r - NA_KR_MAX // 2, 0), rows_n - NA_KR_MAX)
            for e in range(12):
                krow = 4 * (j - 1) + e
                if r0 <= krow < r0 + NA_KR_MAX:
                    ok_r[ti, a, e] = True
                    p_sel[ti, a, e, krow - r + NA_KR_MAX - 1] = 1.0
    c = np.arange(GRID_W)[:, None]
    w = np.arange(GRID_W)[None, :]
    cs = np.clip(c - NA_KC // 2, 0, GRID_W - NA_KC)
    ok_c = (w >= cs) & (w < cs + NA_KC)
    q_sel = np.eye(nc, dtype=np.float32)[np.clip(w - c + NA_KC - 1, 0, nc - 1)]
    hp = lax.Precision.HIGHEST
    m1 = jnp.einsum("taei,lhij->lthaej", jnp.asarray(p_sel), rpb, precision=hp)
    b = jnp.einsum("lthaej,cwj->lthacew", m1, jnp.asarray(q_sel), precision=hp)
    ok = ok_r[:, :, None, :, None] & ok_c[None, None, :, None, :]
    b = jnp.where(jnp.asarray(ok)[None, :, None], b, NEG)
    return b.reshape(rpb.shape[0], 3, NA_HEADS, 4 * GRID_W, 12 * GRID_W)


def _na_kernel(q, k0, k1, k2, v0, v1, v2, kc, vc, bias, o):
    hd = HEAD_DIM
    scale = hd ** -0.5
    qb = q[...].astype(BF16)
    kk = jnp.concatenate([k0[...], k1[...], k2[...]], axis=0).astype(BF16)
    vv = jnp.concatenate([v0[...], v1[...], v2[...]], axis=0).astype(BF16)
    kcb, vcb = kc[...].astype(BF16), vc[...].astype(BF16)
    for h in range(NA_HEADS):
        hs = slice(h * hd, (h + 1) * hd)
        s_nb = _dot_nt(qb[:, hs], kk[:, hs]) * scale + bias[h]
        s_cx = _dot_nt(qb[:, hs], kcb[:, hs]) * scale
        o[:, hs] = _softmax_pv([(s_nb, vv[:, hs]), (s_cx, vcb[:, hs])])


def _na_attention(q, k, v, kc, vc, bias, nb, t):
    nj = t // NA_TQ
    lc = kc.shape[1]
    blk = lambda f: pl.BlockSpec((NA_TQ, BRANCH_W), lambda b, j: (b * nj + f(j), 0))
    prev = lambda j: jnp.maximum(j - 1, 0)
    nxt = lambda j: jnp.minimum(j + 1, nj - 1)
    cur = lambda j: j
    ctx = pl.BlockSpec((None, lc, BRANCH_W), lambda b, j: (b, 0, 0))
    tsel = lambda b, j: (jnp.where(j == 0, 0, jnp.where(j == nj - 1, 2, 1)), 0, 0, 0)
    return pl.pallas_call(
        _na_kernel, grid=(nb, nj),
        in_specs=[blk(cur), blk(prev), blk(cur), blk(nxt), blk(prev), blk(cur), blk(nxt), ctx, ctx,
                  pl.BlockSpec((None,) + bias.shape[1:], tsel)],
        out_specs=blk(cur),
        out_shape=jax.ShapeDtypeStruct((nb * t, BRANCH_W), F32),
        compiler_params=_cp(("parallel", "arbitrary"), 48), name="na_attention",
    )(q, k, k, k, v, v, v, kc, vc, bias)


SW_TQ = 256


def _sw_kernel(q, k0, k1, k2, v0, v1, v2, kc, vc, cq, saq, sbq, ck0, sak0, sbk0, ck1, sak1, sbk1,
               ck2, sak2, sbk2, sink_ref, o, *, t):
    hd = HEAD_DIM
    scale = hd ** -0.5
    half = hd // 4
    j = pl.program_id(1)
    qr = _rope(q[...], cq[...], saq[...], sbq[...], half).astype(BF16)
    kk = jnp.concatenate([_rope(k0[...], ck0[...], sak0[...], sbk0[...], half),
                          _rope(k1[...], ck1[...], sak1[...], sbk1[...], half),
                          _rope(k2[...], ck2[...], sak2[...], sbk2[...], half)], axis=0).astype(BF16)
    vv = jnp.concatenate([v0[...], v1[...], v2[...]], axis=0).astype(BF16)
    kcb, vcb = kc[...].astype(BF16), vc[...].astype(BF16)
    qpos = j * SW_TQ + lax.broadcasted_iota(jnp.int32, (SW_TQ, 3 * SW_TQ), 0)
    kpos = (j - 1) * SW_TQ + lax.broadcasted_iota(jnp.int32, (SW_TQ, 3 * SW_TQ), 1)
    valid = (jnp.abs(qpos - kpos) <= SW_WINDOW) & (kpos >= 0) & (kpos < t)
    for h in range(SW_HEADS):
        g = h // SW_GROUP
        hs, gs = slice(h * hd, (h + 1) * hd), slice(g * hd, (g + 1) * hd)
        s_loc = jnp.where(valid, _dot_nt(qr[:, hs], kk[:, gs]) * scale, NEG)
        s_cx = _dot_nt(qr[:, hs], kcb[:, gs]) * scale
        o[:, hs] = _softmax_pv([(s_loc, vv[:, gs]), (s_cx, vcb[:, gs])], sink=sink_ref[h])


def _sw_attention(q, k, v, kc, vc, sink, tabs_q, tabs_k, nb, t):
    nj = t // SW_TQ
    lc = kc.shape[1]
    kvw = SW_KV_HEADS * HEAD_DIM
    prev = lambda j: jnp.maximum(j - 1, 0)
    nxt = lambda j: jnp.minimum(j + 1, nj - 1)
    cur = lambda j: j
    qblk = pl.BlockSpec((SW_TQ, BRANCH_W), lambda b, j: (b * nj + j, 0))
    kblk = lambda f: pl.BlockSpec((SW_TQ, kvw), lambda b, j: (b * nj + f(j), 0))
    ctx = pl.BlockSpec((None, lc, kvw), lambda b, j: (b, 0, 0))
    tq = pl.BlockSpec((SW_TQ, BRANCH_W), lambda b, j: (j, 0))
    tk = lambda f: pl.BlockSpec((SW_TQ, kvw), lambda b, j: (f(j), 0))
    in_specs = [qblk, kblk(prev), kblk(cur), kblk(nxt), kblk(prev), kblk(cur), kblk(nxt), ctx, ctx, tq, tq, tq]
    args = [q, k, k, k, v, v, v, kc, vc, *tabs_q]
    for f in (prev, cur, nxt):
        in_specs += [tk(f)] * 3
        args += list(tabs_k)
    in_specs.append(pl.BlockSpec(memory_space=pltpu.SMEM))
    args.append(sink)
    return pl.pallas_call(
        functools.partial(_sw_kernel, t=t), grid=(nb, nj), in_specs=in_specs, out_specs=qblk,
        out_shape=jax.ShapeDtypeStruct((nb * t, BRANCH_W), F32),
        compiler_params=_cp(("parallel", "arbitrary"), 48), name="sw_attention",
    )(*args)


MLA_TQ = 512
MLA_TK = 512


def _mla_kernel(qm, kn, kr, vt, knc, krc, vtc, cq, saq, sbq, ck, sak, sbk, o, qt_ref, m_ref, l_ref, acc_ref, *,
                n_lat):
    kj = pl.program_id(2)
    half = MLA_ROPE // 4

    @pl.when(kj == 0)
    def _():
        qt_ref[...] = _rope(qm[...], cq[...], saq[...], sbq[...], half).T.astype(BF16)
        m_ref[...] = jnp.full(m_ref.shape, NEG, F32)
        l_ref[...] = jnp.zeros(l_ref.shape, F32)
        acc_ref[...] = jnp.zeros(acc_ref.shape, F32)

    c2 = MLA_SCALE * 1.4426950408889634

    def step(kcat, vt_b):
        ones = jnp.ones((8, vt_b.shape[1]), BF16)
        for h in range(MLA_HEADS):
            ks = slice(h * LANE, (h + 1) * LANE)
            vs = slice(h * MLA_V, (h + 1) * MLA_V)
            s = _dot(kcat[:, ks], qt_ref[ks, :])
            m_old = m_ref[h]
            m_new = jnp.maximum(m_old, jnp.max(s, axis=0, keepdims=True))
            alpha = jnp.exp2(c2 * (m_old - m_new))
            p = jnp.exp2(c2 * s - c2 * m_new).astype(BF16)
            pv = _dot(jnp.concatenate([vt_b[vs, :], ones], axis=0), p)
            l_ref[h] = alpha * l_ref[h] + pv[MLA_V:MLA_V + 1, :]
            acc_ref[vs, :] = alpha * acc_ref[vs, :] + pv[:MLA_V, :]
            m_ref[h] = m_new

    @pl.when(kj < n_lat)
    def _():
        krr = _rope(kr[...], ck[...], sak[...], sbk[...], half)
        step((kn[...] + jnp.concatenate([krr] * MLA_HEADS, axis=1)).astype(BF16), vt[...].astype(BF16))

    @pl.when(kj == n_lat)
    def _():
        step((knc[...] + jnp.concatenate([krc[...]] * MLA_HEADS, axis=1)).astype(BF16), vtc[...].astype(BF16))
        for h in range(MLA_HEADS):
            vs = slice(h * MLA_V, (h + 1) * MLA_V)
            acc_ref[vs, :] = acc_ref[vs, :] / l_ref[h]
        o[...] = acc_ref[...].T


def _mla_attention(qm, kn, kr, vt, knc, krc, vtc, tabs_q, tabs_k, nb, t):
    nq, n_lat = t // MLA_TQ, t // MLA_TK
    lc = krc.shape[1]
    assert lc == MLA_TK
    kb = lambda b, i, kj: b * n_lat + jnp.minimum(kj, n_lat - 1)
    kblk = lambda wd: pl.BlockSpec((MLA_TK, wd), lambda b, i, kj: (kb(b, i, kj), 0))
    tkb = pl.BlockSpec((MLA_TK, LANE), lambda b, i, kj: (jnp.minimum(kj, n_lat - 1), 0))
    tqb = pl.BlockSpec((MLA_TQ, MLA_QK_W), lambda b, i, kj: (i, 0))
    return pl.pallas_call(
        functools.partial(_mla_kernel, n_lat=n_lat), grid=(nb, nq, n_lat + 1),
        in_specs=[pl.BlockSpec((MLA_TQ, MLA_QK_W), lambda b, i, kj: (b * nq + i, 0)),
                  kblk(MLA_QK_W), kblk(LANE),
                  pl.BlockSpec((BRANCH_W, MLA_TK), lambda b, i, kj: (0, kb(b, i, kj))),
                  pl.BlockSpec((lc, MLA_QK_W), lambda b, i, kj: (b, 0)),
                  pl.BlockSpec((None, lc, LANE), lambda b, i, kj: (b, 0, 0)),
                  pl.BlockSpec((BRANCH_W, lc), lambda b, i, kj: (0, b)),
                  tqb, tqb, tqb, tkb, tkb, tkb],
        out_specs=pl.BlockSpec((MLA_TQ, BRANCH_W), lambda b, i, kj: (b * nq + i, 0)),
        out_shape=jax.ShapeDtypeStruct((nb * t, BRANCH_W), F32),
        scratch_shapes=[pltpu.VMEM((MLA_QK_W, MLA_TQ), BF16), pltpu.VMEM((MLA_HEADS, 1, MLA_TQ), F32),
                        pltpu.VMEM((MLA_HEADS, 1, MLA_TQ), F32), pltpu.VMEM((BRANCH_W, MLA_TQ), F32)],
        compiler_params=_cp(("parallel", "parallel", "arbitrary"), 48), name="mla_attention",
    )(qm, kn, kr, vt, knc, krc, vtc, *tabs_q, *tabs_k)


def _log_sigmoid(x):
    return jnp.minimum(x, 0.0) - jnp.log(1.0 + jnp.exp(-jnp.abs(x)))


def _split3(x):
    hi = x.astype(BF16)
    r1 = x - hi.astype(F32)
    mid = r1.astype(BF16)
    lo = (r1 - mid.astype(F32)).astype(BF16)
    return jnp.concatenate([hi, mid, lo], axis=1)


def _mlstm_dir(d, q_ref, k_ref, v_ref, if_ref, o_ref, c_aug, m_prev, negsel_ref):
    ln = ML_CHUNK
    scale = HEAD_DIM ** -0.5
    t_in = if_ref[...]
    logf = _log_sigmoid(t_in)
    ri = lax.broadcasted_iota(jnp.int32, (ln, ln), 0)
    ci = lax.broadcasted_iota(jnp.int32, (ln, ln), 1)
    mask = (ci <= ri) if d == 0 else (ci >= ri)
    tri = jnp.where(mask, 1.0, 0.0).astype(BF16)
    fs = _split3(logf)
    bcol = _dot(tri, fs[:, :LANE]) + _dot(tri, fs[:, LANE:2 * LANE]) + _dot(tri, fs[:, 2 * LANE:])
    b = pltpu.roll(bcol, LANE - 2 * ML_HEADS, 1)
    a = t_in - b
    row = lax.broadcasted_iota(jnp.int32, (ln, LANE), 0)
    cm, sh = a, 1
    while sh < ln:
        if d == 0:
            cm = jnp.maximum(cm, jnp.where(row >= sh, pltpu.roll(cm, sh, 0), NEG))
        else:
            cm = jnp.maximum(cm, jnp.where(row < ln - sh, pltpu.roll(cm, ln - sh, 0), NEG))
        sh *= 2
    big_m = jnp.maximum(m_prev, cm)
    xs = _split3(big_m - m_prev)
    ys = _split3(b + big_m)
    at = (a - m_prev).T
    last = ln - 1 if d == 0 else 0
    m_last = big_m[last:last + 1, :]
    m_new = b[last:last + 1, :] + m_last
    decay = jnp.exp(m_prev - m_last)
    wt = jnp.exp(a - m_last).T
    q, k, v = q_ref[...], k_ref[...], v_ref[...]
    new_c = []
    for h in range(ML_HEADS):
        c = d * ML_HEADS + h
        hsl = slice(h * LANE, (h + 1) * LANE)
        neg_x = _dot(xs, negsel_ref[c])
        neg_y = _dot(ys, negsel_ref[c])
        w_intra = jnp.exp(jnp.where(mask, at[c:c + 1, :] + neg_x, NEG))
        kh = k[:, hsl] * scale
        qb, kb, vb = q[:, hsl].astype(BF16), kh.astype(BF16), v[:, hsl].astype(BF16)
        s = _dot_nt(qb, kb) * w_intra
        c_prev = c_aug[c]
        nd = _dot(s.astype(BF16), vb) + jnp.exp(neg_x) * _dot(qb, c_prev.astype(BF16))
        den = pltpu.roll(nd, HEAD_DIM, 1)
        hout = nd / jnp.maximum(jnp.abs(den), jnp.exp(neg_y))
        o_ref[:, h * HEAD_DIM:(h + 1) * HEAD_DIM] = hout[:, :HEAD_DIM]
        new_c.append(decay[:, c:c + 1] * c_prev + _dot((kh.T * wt[c:c + 1, :]).astype(BF16), vb))
    return new_c, m_new


def _mlstm_kernel(qf, kf, vf, iff, qb, kb, vb, ifb, c0, m0, negsel, hf, hb, c_out, m_out, c_s, m_s, *, nc):
    c = pl.program_id(1)
    nst = 2 * ML_HEADS

    @pl.when(c == 0)
    def _():
        c_s[...] = c0[...]
        m_s[...] = m0[...]

    c_aug = [c_s[i] for i in range(nst)]
    m_prev = m_s[...]
    cf, mf = _mlstm_dir(0, qf, kf, vf, iff, hf, c_aug, m_prev, negsel)
    cb, mb = _mlstm_dir(1, qb, kb, vb, ifb, hb, c_aug, m_prev, negsel)
    for i, cn in enumerate(cf + cb):
        c_s[i] = cn
    lane = lax.broadcasted_iota(jnp.int32, (1, LANE), 1)
    m_s[...] = jnp.where(lane < ML_HEADS, mf, mb)

    @pl.when(c == nc - 1)
    def _():
        c_out[...] = c_s[...]
        m_out[...] = m_s[...]


def _mlstm(q, k, v, gif, c0, n0, m0, nb, t):
    nc = t // ML_CHUNK
    nst = 2 * ML_HEADS
    hd = HEAD_DIM
    qkw = ML_HEADS * LANE
    c_aug0 = jnp.concatenate([c0.reshape(nb, nst, hd, hd),
                              jnp.broadcast_to(n0.reshape(nb, nst, hd, 1), (nb, nst, hd, hd))], axis=-1)
    c_aug0 = jnp.pad(c_aug0, ((0, 0), (0, 0), (0, LANE - hd), (0, 0)))
    m_row0 = jnp.pad(m0.reshape(nb, 1, nst), ((0, 0), (0, 0), (0, LANE - nst)))
    sel = np.zeros((nst, 3 * LANE, LANE), np.float32)
    for ch in range(nst):
        sel[ch, [ch, LANE + ch, 2 * LANE + ch], :] = -1.0
    fw = lambda wd: pl.BlockSpec((ML_CHUNK, wd), lambda b, c: (b * nc + c, 0))
    bw = lambda wd: pl.BlockSpec((ML_CHUNK, wd), lambda b, c: (b * nc + nc - 1 - c, 0))
    st = lambda shp: pl.BlockSpec((None,) + shp, lambda b, c: (b,) + (0,) * len(shp))
    shapes = ((nst, LANE, LANE), (1, LANE))
    hf, hb, c_fin, m_fin = pl.pallas_call(
        functools.partial(_mlstm_kernel, nc=nc), grid=(nb, nc),
        in_specs=[fw(qkw), fw(qkw), fw(qkw), fw(LANE), bw(qkw), bw(qkw), bw(qkw), bw(LANE)]
                 + [st(s) for s in shapes] + [pl.BlockSpec(sel.shape, lambda b, c: (0, 0, 0))],
        out_specs=[fw(BRANCH_W), bw(BRANCH_W)] + [st(s) for s in shapes],
        out_shape=[jax.ShapeDtypeStruct((nb * t, BRANCH_W), F32)] * 2
                  + [jax.ShapeDtypeStruct((nb,) + s, F32) for s in shapes],
        scratch_shapes=[pltpu.VMEM(s, F32) for s in shapes],
        compiler_params=_cp(("parallel", "arbitrary")), name="mlstm_scan",
    )(q, k, v, gif, q, k, v, gif, c_aug0, m_row0, jnp.asarray(sel, BF16))
    c_new = c_fin[:, :, :hd, :hd].reshape(nb, 2, ML_HEADS, hd, hd)
    n_new = c_fin[:, :, :hd, hd].reshape(nb, 2, ML_HEADS, hd)
    m_new = m_fin[:, 0, :nst].reshape(nb, 2, ML_HEADS)
    return hf, hb, c_new, n_new, m_new


def _prep_weights(w_in, b_in, mla_q_norm, w_uq, w_uk, w_uv, w_branch, w_out, peer_wq, peer_keys, peer_u, peer_v):
    offs = np.cumsum((0,) + IN_SPLITS)
    seg = lambda a, i: a[..., offs[i]:offs[i + 1]]
    padr = lambda a, wd, lead=0: jnp.pad(a, [(0, 0)] * (a.ndim - 1) + [(lead, wd - a.shape[-1] - lead)])

    def headpad(a, fill):
        a4 = a.reshape(a.shape[:-1] + (ML_HEADS, HEAD_DIM))
        a4 = jnp.pad(a4, [(0, 0)] * (a4.ndim - 1) + [(0, LANE - HEAD_DIM)], constant_values=fill)
        return a4.reshape(a.shape[:-1] + (ML_HEADS * LANE,))

    def layout(a, is_bias):
        ml_if = padr(jnp.concatenate([seg(a, 7), seg(a, 8)], axis=-1), LANE)
        ml = [headpad(seg(a, 3), 0.0), headpad(seg(a, 4), 0.0), headpad(seg(a, 5), 1.0 if is_bias else 0.0)]
        parts = [seg(a, i) for i in range(3)] + ml + [seg(a, 6), ml_if, seg(a, 9), seg(a, 10), seg(a, 11),
                                                       padr(seg(a, 12), 256), seg(a, 13),
                                                       padr(seg(a, 14), LANE, KROPE_SLICE.start)]
        return jnp.concatenate(parts, axis=-1)

    nl = w_in.shape[0]
    w_uq_p = padr(w_uq.reshape(nl, MLA_Q_RANK, MLA_HEADS, MLA_NOPE + MLA_ROPE), LANE).reshape(nl, MLA_Q_RANK, -1)
    w_uq_p = jnp.pad(w_uq_p, ((0, 0), (0, 256 - MLA_Q_RANK), (0, 0)))
    w_uk_p = padr(w_uk.reshape(nl, MLA_KV_RANK, MLA_HEADS, MLA_NOPE), LANE).reshape(nl, MLA_KV_RANK, -1)
    return dict(
        w_ab=layout(w_in, False).astype(BF16), b_ab=layout(b_in, True),
        w_c=seg(w_in, 15).astype(BF16), b_c=seg(b_in, 15),
        q_norm=padr(mla_q_norm, 256), w_uq=w_uq_p.astype(BF16),
        w_ukv=jnp.concatenate([w_uk_p, w_uv], axis=-1).astype(BF16),
        w_branch=w_branch.astype(BF16), w_out=w_out.astype(BF16), peer_wq=peer_wq.astype(BF16),
        peer_keys=peer_keys.astype(BF16), peer_u=peer_u.astype(BF16),
        peer_vt=jnp.swapaxes(peer_v.reshape(nl, -1, PEER_EC, peer_v.shape[-1]), 2, 3).astype(BF16),
    )


def _project(x, wl, l, norm_g, sc, sh, rows_per_mod):
    outs = _linear(x, wl["w_ab"][l], [wd for _, wd in SEGS_AB], bias=wl["b_ab"][l], norm_g=norm_g,
                   mod=(sc, sh), rows_per_mod=rows_per_mod, tm=256, vmem_mb=48, name="in_proj_ab")
    p = {name: o for (name, _), o in zip(SEGS_AB, outs)}
    (p["gate"],) = _linear(x, wl["w_c"][l], [N_BRANCH * D_MODEL], bias=wl["b_c"][l], norm_g=norm_g,
                           mod=(sc, sh), rows_per_mod=rows_per_mod, tm=256, vmem_mb=48, name="in_proj_gate")
    (qm,) = _linear(p["cq"], wl["w_uq"][l], [MLA_QK_W], norm_g=wl["q_norm"][l], k_valid=MLA_Q_RANK, tm=512,
                    name="mla_q_up")
    return p, qm


def kernel(x_prompt, x_sample, cache_na_k, cache_na_v, state_mlstm_C, state_mlstm_n, state_mlstm_m,
           cache_swa_k, cache_swa_v, cache_mla_ckv, cache_mla_krope, c, c_ctx, w_mod, b_mod, norm1_g, norm2_g,
           w_in, b_in, na_rpb, sw_sink, mla_q_norm, w_uq, mla_kv_norm, w_uk, w_uv, w_branch, w_out, peer_wq,
           peer_keys, peer_u, peer_v, final_norm_g):
    nbp, tp, d = x_prompt.shape
    nbs, ts, _ = x_sample.shape
    nl = w_in.shape[0]
    lc = cache_na_k.shape[2]
    nst = 2 * ML_HEADS
    wl = _prep_weights(w_in, b_in, mla_q_norm, w_uq, w_uk, w_uv, w_branch, w_out, peer_wq, peer_keys,
                       peer_u, peer_v)

    n_cond = 1 + nbs
    cond = jnp.zeros((16, d), F32).at[0].set(c_ctx).at[1:n_cond].set(c)
    mods = _modulation(cond, w_mod, b_mod).reshape(nl, 16, 6, 1, d)

    tabs_swq = _rope_tables(ts, SW_HEADS, HEAD_DIM)
    tabs_swk = _rope_tables(ts, SW_KV_HEADS, HEAD_DIM)
    tabs_mlaq = _rope_tables(ts, MLA_HEADS, MLA_ROPE, lead=MLA_NOPE, width=LANE)
    tabs_mlak = _rope_tables(ts, 1, MLA_ROPE, lead=MLA_NOPE, width=LANE)
    na_bias = _na_bias_tables(na_rpb, ts // GRID_W)
    krope_c = jnp.pad(cache_mla_krope, ((0, 0), (0, 0), (0, 0), (KROPE_SLICE.start, LANE - KROPE_SLICE.stop)))

    xp = x_prompt.reshape(nbp * tp, d)
    xs = x_sample.reshape(nbs * ts, d)
    zeros_c = jnp.zeros((nbp, 2, ML_HEADS, HEAD_DIM, HEAD_DIM), F32)
    zeros_n = jnp.zeros((nbp, 2, ML_HEADS, HEAD_DIM), F32)
    zeros_m = jnp.zeros((nbp, 2, ML_HEADS), F32)
    per_layer = tuple([] for _ in range(9))

    for l in range(nl):
        mod_p = [mods[l, 0:1, i] for i in range(6)]
        mod_s = [mods[l, 1:n_cond, i] for i in range(6)]
        sink = sw_sink[l]

        p, qm = _project(xp, wl, l, norm1_g[l], mod_p[1], mod_p[0], nbp * tp)
        ckv_n, kn, vm = _linear(p["ckv"], wl["w_ukv"][l], [MLA_QK_W, 256], norm_g=mla_kv_norm[l],
                                emit_normed=True, tm=512, name="mla_kv_up")
        o_na, o_sw, o_mla = _ctx_attention(p, qm, kn, vm, sink, nbp, tp)
        hf, hb, c_new, n_new, m_new = _mlstm(p["ml_q"], p["ml_k"], p["ml_v"], p["ml_if"], zeros_c, zeros_n,
                                             zeros_m, nbp, tp)
        xp = _merge(o_na, hf, hb, p["ml_o"], o_sw, o_mla, p["gate"], wl["w_branch"][l], wl["w_out"][l], xp,
                    mod_p[2], nbp * tp)
        xp = _peer(xp, norm2_g[l], mod_p[4], mod_p[3], mod_p[5], wl["peer_wq"][l], wl["peer_keys"][l],
                   wl["peer_u"][l], wl["peer_vt"][l], nbp * tp)
        ctx_t = (p["na_k"].reshape(nbp, tp, NA_HEADS, HEAD_DIM), p["na_v"].reshape(nbp, tp, NA_HEADS, HEAD_DIM),
                 c_new, n_new, m_new,
                 p["sw_k"].reshape(nbp, tp, SW_KV_HEADS, HEAD_DIM), p["sw_v"].reshape(nbp, tp, SW_KV_HEADS, HEAD_DIM),
                 ckv_n.reshape(nbp, tp, MLA_KV_RANK), p["krope"][:, KROPE_SLICE].reshape(nbp, tp, MLA_ROPE))
        for i, a in enumerate(ctx_t):
            per_layer[i].append(a)

        p, qm = _project(xs, wl, l, norm1_g[l], mod_s[1], mod_s[0], ts)
        _, kn, vt = _linear(p["ckv"], wl["w_ukv"][l], [MLA_QK_W, 256], norm_g=mla_kv_norm[l], emit_normed=True,
                            transpose_last=True, tm=512, name="mla_kv_up_t")
        knc, vtc = _linear(cache_mla_ckv[:, l].reshape(nbs * lc, MLA_KV_RANK), wl["w_ukv"][l], [MLA_QK_W, 256],
                           transpose_last=True, tm=512, name="mla_kv_up_cache")
        o_na = _na_attention(p["na_q"], p["na_k"], p["na_v"], cache_na_k[:, l].reshape(nbs, lc, BRANCH_W),
                             cache_na_v[:, l].reshape(nbs, lc, BRANCH_W), na_bias[l], nbs, ts)
        o_sw = _sw_attention(p["sw_q"], p["sw_k"], p["sw_v"],
                             cache_swa_k[:, l].reshape(nbs, lc, SW_KV_HEADS * HEAD_DIM),
                             cache_swa_v[:, l].reshape(nbs, lc, SW_KV_HEADS * HEAD_DIM), sink, tabs_swq, tabs_swk,
                             nbs, ts)
        o_mla = _mla_attention(qm, kn, p["krope"], vt, knc, krope_c[:, l], vtc, tabs_mlaq, tabs_mlak, nbs, ts)
        hf, hb, _, _, _ = _mlstm(p["ml_q"], p["ml_k"], p["ml_v"], p["ml_if"], state_mlstm_C[:, l],
                                 state_mlstm_n[:, l], state_mlstm_m[:, l], nbs, ts)
        xs = _merge(o_na, hf, hb, p["ml_o"], o_sw, o_mla, p["gate"], wl["w_branch"][l], wl["w_out"][l], xs,
                    mod_s[2], ts)
        xs = _peer(xs, norm2_g[l], mod_s[4], mod_s[3], mod_s[5], wl["peer_wq"][l], wl["peer_keys"][l],
                   wl["peer_u"][l], wl["peer_vt"][l], ts)

    y_prompt = _final_norm(xp, final_norm_g).reshape(nbp, tp, d)
    y_sample = _final_norm(xs, final_norm_g).reshape(nbs, ts, d)
    return (y_prompt, y_sample) + tuple(jnp.stack(s, axis=1) for s in per_layer)
```

```python
import functools

import numpy as np
import jax
import jax.numpy as jnp
from jax import lax
from jax.experimental import pallas as pl
from jax.experimental.pallas import tpu as pltpu

F32 = jnp.float32
BF16 = jnp.bfloat16

D_MODEL = 1024
GRID_W = 64
HEAD_DIM = 64
N_BRANCH = 4
BRANCH_W = 256
NA_HEADS = 4
NA_KR_MAX = 8
NA_KC = 16
ML_HEADS = 4
ML_CHUNK = 128
SW_HEADS = 4
SW_KV_HEADS = 2
SW_GROUP = SW_HEADS // SW_KV_HEADS
SW_WINDOW = 128
MLA_HEADS = 4
MLA_Q_RANK = 192
MLA_KV_RANK = 128
MLA_NOPE = 64
MLA_ROPE = 32
MLA_V = 64
MLA_SCALE = (MLA_NOPE + MLA_ROPE) ** -0.5
PEER_HEADS = 8
PEER_KEY_DIM = 64
PEER_N_KEYS = 128
PEER_TOPK = 16
PEER_EC = 512
ROPE_BASE = 10000.0
RMS_EPS = 1e-6
NEG = -1e30
IN_SPLITS = (256, 256, 256, 256, 256, 256, 256, 8, 8, 256, 128, 128, 192, 128, 32, 4096)

LANE = 128
MLA_QK_W = MLA_HEADS * LANE

SEGS_AB = (("na_q", 256), ("na_k", 256), ("na_v", 256), ("ml_q", 512), ("ml_k", 512), ("ml_v", 512),
           ("ml_o", 256), ("ml_if", 128), ("sw_q", 256), ("sw_k", 128), ("sw_v", 128), ("cq", 256),
           ("ckv", 128), ("krope", 128))
KROPE_SLICE = slice(MLA_NOPE, MLA_NOPE + MLA_ROPE)


def _cp(sem, vmem_mb=None):
    kw = dict(dimension_semantics=sem)
    if vmem_mb is not None:
        kw["vmem_limit_bytes"] = vmem_mb * 1024 * 1024
    return pltpu.CompilerParams(**kw)


def _dot(a, b):
    return jnp.dot(a, b, preferred_element_type=F32)


def _dot_nt(a, b):
    return lax.dot_general(a, b, (((1,), (1,)), ((), ())), preferred_element_type=F32)


def _rms(x, inv_n):
    return x * lax.rsqrt(jnp.sum(x * x, axis=-1, keepdims=True) * inv_n + RMS_EPS)


def _sigmoid(x):
    return 1.0 / (1.0 + jnp.exp(-x))


def _mod_kernel(c_ref, w_ref, b_ref, o_ref):
    c = c_ref[...]
    s = c * _sigmoid(c)
    o_ref[...] = _dot(s.astype(BF16), w_ref[...].astype(BF16)) + b_ref[...]


def _modulation(cond, w_mod, b_mod):
    nl, d, n = w_mod.shape
    r = cond.shape[0]
    tn = 1024
    return pl.pallas_call(
        _mod_kernel,
        grid=(nl, n // tn),
        in_specs=[pl.BlockSpec((r, d), lambda l, j: (0, 0)),
                  pl.BlockSpec((None, d, tn), lambda l, j: (l, 0, j)),
                  pl.BlockSpec((None, 1, tn), lambda l, j: (l, 0, j))],
        out_specs=pl.BlockSpec((None, r, tn), lambda l, j: (l, 0, j)),
        out_shape=jax.ShapeDtypeStruct((nl, r, n), F32),
        compiler_params=_cp(("parallel", "parallel")),
        name="adaln_modulation",
    )(cond, w_mod, b_mod.reshape(nl, 1, n))


def _linear_kernel(*refs, segs, has_norm, has_mod, has_bias, emit_normed, transpose_last, inv_k):
    it = iter(refs)
    x_ref = next(it)
    g_ref = next(it) if has_norm else None
    sc_ref = next(it) if has_mod else None
    sh_ref = next(it) if has_mod else None
    w_ref = next(it)
    b_ref = next(it) if has_bias else None
    outs = list(it)
    x = x_ref[...]
    if has_norm:
        x = _rms(x, inv_k) * g_ref[...]
    if has_mod:
        x = x * (1.0 + sc_ref[0]) + sh_ref[0]
    if emit_normed:
        outs[0][...] = x
        outs = outs[1:]
    xb = x.astype(BF16)
    for idx, ((s, wd), o) in enumerate(zip(segs, outs)):
        y = _dot(xb, w_ref[:, s:s + wd])
        if has_bias:
            y = y + b_ref[:, s:s + wd]
        o[...] = (y.T if (transpose_last and idx == len(segs) - 1) else y).astype(o.dtype)


def _linear(x, w, widths, *, bias=None, norm_g=None, k_valid=None, mod=None, rows_per_mod=None,
            emit_normed=False, transpose_last=False, bf16_out=(), tm=256, vmem_mb=None, name="linear"):
    n, k = x.shape
    tm = min(tm, n)
    segs, s = [], 0
    for wd in widths:
        segs.append((s, wd))
        s += wd
    assert s == w.shape[1] and n % tm == 0
    args = [x]
    in_specs = [pl.BlockSpec((tm, k), lambda i: (i, 0))]
    if norm_g is not None:
        args.append(norm_g.reshape(1, k))
        in_specs.append(pl.BlockSpec((1, k), lambda i: (0, 0)))
    if mod is not None:
        assert rows_per_mod % tm == 0
        for m in mod:
            args.append(m)
            in_specs.append(pl.BlockSpec((1, 1, k), lambda i: ((i * tm) // rows_per_mod, 0, 0)))
    args.append(w)
    in_specs.append(pl.BlockSpec(w.shape, lambda i: (0, 0)))
    if bias is not None:
        args.append(bias.reshape(1, -1))
        in_specs.append(pl.BlockSpec((1, w.shape[1]), lambda i: (0, 0)))
    out_shapes, out_specs = [], []
    if emit_normed:
        out_shapes.append(jax.ShapeDtypeStruct((n, k), F32))
        out_specs.append(pl.BlockSpec((tm, k), lambda i: (i, 0)))
    for idx, wd in enumerate(widths):
        if transpose_last and idx == len(widths) - 1:
            out_shapes.append(jax.ShapeDtypeStruct((wd, n), F32))
            out_specs.append(pl.BlockSpec((wd, tm), lambda i: (0, i)))
        else:
            out_shapes.append(jax.ShapeDtypeStruct((n, wd), BF16 if idx in bf16_out else F32))
            out_specs.append(pl.BlockSpec((tm, wd), lambda i: (i, 0)))
    kern = functools.partial(_linear_kernel, segs=tuple(segs), has_norm=norm_g is not None,
                             has_mod=mod is not None, has_bias=bias is not None, emit_normed=emit_normed,
                             transpose_last=transpose_last, inv_k=1.0 / (k_valid or k))
    return pl.pallas_call(kern, grid=(n // tm,), in_specs=in_specs, out_specs=out_specs,
                          out_shape=out_shapes, compiler_params=_cp(("parallel",), vmem_mb), name=name)(*args)


def _merge_kernel(ona, hf, hb, mlo, osw, omla, gate, wb, wo, x, g1, bd, o):
    h = hf[...] + hb[...]
    hsq = h * h
    hi = hsq.astype(BF16)
    lo = (hsq - hi.astype(F32)).astype(BF16)
    ms = (_dot(hi, bd[...]) + _dot(lo, bd[...])) * (1.0 / HEAD_DIM)
    oml = h * lax.rsqrt(ms + RMS_EPS) * _sigmoid(mlo[...])
    branches = (ona[...], oml, osw[...], omla[...])
    acc = None
    for n in range(N_BRANCH):
        g = gate[:, n * D_MODEL:(n + 1) * D_MODEL].astype(F32)
        y = _sigmoid(g) * _dot(branches[n].astype(BF16), wb[n])
        acc = y if acc is None else acc + y
    o[...] = x[...] + g1[0] * _dot(acc.astype(BF16), wo[...])


def _merge(ona, hf, hb, mlo, osw, omla, gate, wb, wo, x, g1, rows_per_mod, tm=256):
    n = x.shape[0]
    tm = min(tm, n)
    bd = jnp.asarray(np.kron(np.eye(BRANCH_W // HEAD_DIM), np.ones((HEAD_DIM, HEAD_DIM))), BF16)
    row = lambda wd: pl.BlockSpec((tm, wd), lambda i: (i, 0))
    return pl.pallas_call(
        _merge_kernel,
        grid=(n // tm,),
        in_specs=[row(BRANCH_W)] * 6 + [row(N_BRANCH * D_MODEL),
                  pl.BlockSpec(wb.shape, lambda i: (0, 0, 0)),
                  pl.BlockSpec(wo.shape, lambda i: (0, 0)),
                  row(D_MODEL),
                  pl.BlockSpec((1, 1, D_MODEL), lambda i: ((i * tm) // rows_per_mod, 0, 0)),
                  pl.BlockSpec(bd.shape, lambda i: (0, 0))],
        out_specs=row(D_MODEL),
        out_shape=jax.ShapeDtypeStruct((n, D_MODEL), F32),
        compiler_params=_cp(("parallel",), 48),
        name="branch_merge",
    )(ona, hf, hb, mlo, osw, omla, gate, wb, wo, x, g1, bd)


def _top_rows(s, k):
    cur, rows = s, []
    for _ in range(k):
        mx = jnp.max(cur, axis=0, keepdims=True)
        rows.append(mx)
        cur = jnp.where(cur == mx, NEG, cur)
    return rows


def _peer_select(s1, s2):
    rows8 = lax.broadcasted_iota(jnp.int32, (8, 1), 0)
    a1, a2 = _top_rows(s1, PEER_TOPK), _top_rows(s2, PEER_TOPK)
    a1_16 = jnp.concatenate(a1, axis=0)
    a1_8 = a1_16[:8]
    cands = [a1_16 + a2[0], a1_8 + a2[1]]
    for k2 in range(2, 8):
        cands.append(jnp.where(rows8 < PEER_TOPK // (k2 + 1), a1_8 + a2[k2], NEG))
    cands.append(a1[0] + jnp.concatenate(a2[8:], axis=0))
    cand = jnp.concatenate(cands, axis=0)
    top = _top_rows(cand, PEER_TOPK + 1)
    tau = 0.5 * (top[PEER_TOPK - 1] + top[PEER_TOPK])
    z = jnp.sum(jnp.where(cand >= tau, jnp.exp(cand - top[0]), 0.0), axis=0, keepdims=True)
    return tau - s1, jnp.exp(s1 - a1[0]), jnp.exp(s2 - a2[0]) * (0.5 / z)


def _peer_kernel(x_ref, ng_ref, sc_ref, sh_ref, g2_ref, wq_ref, keys_ref, u_ref, vt_ref, o_ref,
                 h2t_ref, acc_ref, t1_ref, e1_ref, s2_ref, e2_ref, ht0_ref, ht1_ref, z0_ref, z1_ref, *,
                 ec, n_chunks, tm):
    s = pl.program_id(1)
    n_i1 = ec // PEER_N_KEYS
    n_tc = tm // LANE

    @pl.when(s == 0)
    def _():
        x = x_ref[...]
        h2 = _rms(x, 1.0 / D_MODEL) * ng_ref[...] * (1.0 + sc_ref[0]) + sh_ref[0]
        h2t_ref[...] = h2.T.astype(BF16)
        q = _dot(h2.astype(BF16), wq_ref[...])
        qt = q.T.astype(BF16)
        for h in range(PEER_HEADS):
            r0 = h * 2 * PEER_KEY_DIM
            s1 = _dot(keys_ref[h, 0], qt[r0:r0 + PEER_KEY_DIM, :])
            s2 = _dot(keys_ref[h, 1], qt[r0 + PEER_KEY_DIM:r0 + 2 * PEER_KEY_DIM, :])
            for tc in range(n_tc):
                cols = slice(tc * LANE, (tc + 1) * LANE)
                t1, e1, e2 = _peer_select(s1[:, cols], s2[:, cols])
                t1_ref[h, :, cols] = t1
                e1_ref[h, :, cols] = e1
                s2_ref[h, tc] = s2[:, cols]
                e2_ref[h, tc] = e2
        for r in (ht0_ref, ht1_ref, z0_ref, z1_ref, acc_ref):
            r[...] = jnp.zeros_like(r)

    cb = jnp.clip(s - 1, 0, n_chunks - 1)

    def stages(ht_w, ht_r, z_w, z_r):
        orows = D_MODEL // n_i1
        half = n_tc // 2

        def up_piece(j, nh):
            rows = slice(j * PEER_N_KEYS, (j + 1) * PEER_N_KEYS)
            hj = _dot(u_ref[rows, :], h2t_ref[:, nh * half * LANE:(nh + 1) * half * LANE])
            for k in range(half):
                ht_w[nh * half + k, rows, :] = hj[:, k * LANE:(k + 1) * LANE]

        def down_piece(j, nh):
            osl = slice(j * orows, (j + 1) * orows)
            zh = jnp.concatenate([z_r[nh * half + k] for k in range(half)], axis=1)
            acc_ref[osl, nh * half * LANE:(nh + 1) * half * LANE] += _dot(vt_ref[osl, :], zh)

        pieces = [(fn, j, nh) for j in range(n_i1) for nh in (0, 1) for fn in (up_piece, down_piece)]
        th_rows = [[t1_ref[h, pl.ds(cb * n_i1 + j, 1), :] for h in range(PEER_HEADS)] for j in range(n_i1)]
        e1_rows = [[e1_ref[h, pl.ds(cb * n_i1 + j, 1), :] for h in range(PEER_HEADS)] for j in range(n_i1)]
        n_rh = 4
        rh_rows = PEER_N_KEYS // n_rh
        n_units = n_tc * n_rh
        for unit in range(n_units):
            for fn, j, nh in pieces[unit * len(pieces) // n_units:(unit + 1) * len(pieces) // n_units]:
                fn(j, nh)
            tc, rh = divmod(unit, n_rh)
            cols = slice(tc * LANE, (tc + 1) * LANE)
            r2 = slice(rh * rh_rows, (rh + 1) * rh_rows)
            ws = [None] * n_i1
            for h in range(PEER_HEADS):
                s2t, e2t = s2_ref[h, tc, r2, :], e2_ref[h, tc, r2, :]
                for j in range(n_i1):
                    term = jnp.where(s2t >= th_rows[j][h][:, cols], e2t, 0.0) * e1_rows[j][h][:, cols]
                    ws[j] = term if ws[j] is None else ws[j] + term
            for j in range(n_i1):
                rows = slice(j * PEER_N_KEYS + rh * rh_rows, j * PEER_N_KEYS + (rh + 1) * rh_rows)
                xh = ht_r[tc, rows, :]
                t = jnp.tanh(xh * (0.7978845608028654 + 0.035677408136300125 * (xh * xh)))
                z_w[tc, rows, :] = (ws[j] * (xh + xh * t)).astype(BF16)

    @pl.when(s % 2 == 0)
    def _():
        stages(ht0_ref, ht1_ref, z1_ref, z0_ref)

    @pl.when(s % 2 == 1)
    def _():
        stages(ht1_ref, ht0_ref, z0_ref, z1_ref)

    @pl.when(s == n_chunks + 1)
    def _():
        o_ref[...] = x_ref[...] + g2_ref[0] * acc_ref[...].T


def _peer(x, ng, sc, sh, g2, wq, keys, u, vt, rows_per_mod, tm=512):
    n = x.shape[0]
    tm = min(tm, n)
    n_chunks, _, ec = vt.shape
    last = n_chunks - 1
    modspec = pl.BlockSpec((1, 1, D_MODEL), lambda i, s: ((i * tm) // rows_per_mod, 0, 0))
    kern = functools.partial(_peer_kernel, ec=ec, n_chunks=n_chunks, tm=tm)
    head_buf = pltpu.VMEM((PEER_HEADS, PEER_N_KEYS, tm), F32)
    tile_buf = pltpu.VMEM((PEER_HEADS, tm // LANE, PEER_N_KEYS, LANE), F32)
    return pl.pallas_call(
        kern,
        grid=(n // tm, n_chunks + 2),
        in_specs=[pl.BlockSpec((tm, D_MODEL), lambda i, s: (i, 0)),
                  pl.BlockSpec((1, D_MODEL), lambda i, s: (0, 0)),
                  modspec, modspec, modspec,
                  pl.BlockSpec(wq.shape, lambda i, s: (0, 0)),
                  pl.BlockSpec(keys.shape, lambda i, s: (0, 0, 0, 0)),
                  pl.BlockSpec((ec, D_MODEL), lambda i, s: (jnp.minimum(s, last), 0)),
                  pl.BlockSpec((None, D_MODEL, ec), lambda i, s: (jnp.clip(s - 2, 0, last), 0, 0))],
        out_specs=pl.BlockSpec((tm, D_MODEL), lambda i, s: (i, 0)),
        out_shape=jax.ShapeDtypeStruct((n, D_MODEL), F32),
        scratch_shapes=[pltpu.VMEM((D_MODEL, tm), BF16), pltpu.VMEM((D_MODEL, tm), F32),
                        head_buf, head_buf, tile_buf, tile_buf,
                        pltpu.VMEM((tm // LANE, ec, LANE), F32), pltpu.VMEM((tm // LANE, ec, LANE), F32),
                        pltpu.VMEM((tm // LANE, ec, LANE), BF16), pltpu.VMEM((tm // LANE, ec, LANE), BF16)],
        compiler_params=_cp(("parallel", "arbitrary"), 56),
        name="peer_ffn",
    )(x, ng.reshape(1, D_MODEL), sc, sh, g2, wq, keys, u, vt)


def _final_norm_kernel(x_ref, g_ref, o_ref):
    o_ref[...] = _rms(x_ref[...], 1.0 / D_MODEL) * g_ref[...]


def _final_norm(x, g, tm=512):
    n = x.shape[0]
    tm = min(tm, n)
    return pl.pallas_call(
        _final_norm_kernel, grid=(n // tm,),
        in_specs=[pl.BlockSpec((tm, D_MODEL), lambda i: (i, 0)), pl.BlockSpec((1, D_MODEL), lambda i: (0, 0))],
        out_specs=pl.BlockSpec((tm, D_MODEL), lambda i: (i, 0)),
        out_shape=jax.ShapeDtypeStruct((n, D_MODEL), F32),
        compiler_params=_cp(("parallel",)), name="final_norm")(x, g.reshape(1, D_MODEL))


def _rope_tables(t, n_heads, dh, lead=0, width=None):
    width = width or dh
    pos = jnp.arange(t)
    sec, half = dh // 2, dh // 4
    freqs = ROPE_BASE ** (-jnp.arange(half, dtype=F32) / half)
    first = jnp.asarray((np.arange(dh) % sec) < half)
    ang_r = (pos // GRID_W).astype(F32)[:, None] * freqs[None, :]
    ang_c = (pos % GRID_W).astype(F32)[:, None] * freqs[None, :]
    ang = jnp.concatenate([ang_r, ang_r, ang_c, ang_c], axis=1)
    cos, sin = jnp.cos(ang), jnp.sin(ang)
    sa = jnp.where(first[None, :], -sin, 0.0)
    sb = jnp.where(first[None, :], 0.0, sin)
    padw = ((0, 0), (lead, width - dh - lead))
    place = lambda a, fill: jnp.tile(jnp.pad(a, padw, constant_values=fill), (1, n_heads))
    return place(cos, 1.0), place(sa, 0.0), place(sb, 0.0)


def _rope(x, cos, sa, sb, half):
    w = x.shape[-1]
    return x * cos + pltpu.roll(x, w - half, 1) * sa + pltpu.roll(x, half, 1) * sb


def _softmax_pv(blocks, sink=None):
    m = None
    for s, _ in blocks:
        bm = jnp.max(s, axis=1, keepdims=True)
        m = bm if m is None else jnp.maximum(m, bm)
    if sink is not None:
        m = jnp.maximum(m, sink)
    l, acc = None, None
    for s, v in blocks:
        p = jnp.exp(s - m)
        bl = jnp.sum(p, axis=1, keepdims=True)
        pv = _dot(p.astype(BF16), v)
        l = bl if l is None else l + bl
        acc = pv if acc is None else acc + pv
    if sink is not None:
        l = l + jnp.exp(sink - m)
    return acc / l


def _ctx_attn_kernel(naq, nak, nav, swq, swk, swv, qm, kn, kr, vm, sink_ref, o_na, o_sw, o_mla):
    hd = HEAD_DIM
    scale = hd ** -0.5
    q, k, v = naq[...].astype(BF16), nak[...].astype(BF16), nav[...].astype(BF16)
    for h in range(NA_HEADS):
        hs = slice(h * hd, (h + 1) * hd)
        o_na[:, hs] = _softmax_pv([(_dot_nt(q[:, hs], k[:, hs]) * scale, v[:, hs])])
    q, k, v = swq[...].astype(BF16), swk[...].astype(BF16), swv[...].astype(BF16)
    for h in range(SW_HEADS):
        g = h // SW_GROUP
        hs, gs = slice(h * hd, (h + 1) * hd), slice(g * hd, (g + 1) * hd)
        o_sw[:, hs] = _softmax_pv([(_dot_nt(q[:, hs], k[:, gs]) * scale, v[:, gs])], sink=sink_ref[h])
    q, v = qm[...].astype(BF16), vm[...].astype(BF16)
    krv = kr[...]
    for h in range(MLA_HEADS):
        ks = slice(h * LANE, (h + 1) * LANE)
        kcat = (kn[:, ks] + krv).astype(BF16)
        s = _dot_nt(q[:, ks], kcat) * MLA_SCALE
        o_mla[:, h * MLA_V:(h + 1) * MLA_V] = _softmax_pv([(s, v[:, h * MLA_V:(h + 1) * MLA_V])])


def _ctx_attention(p, qm, kn, vm, sink, nb, t):
    row = lambda wd: pl.BlockSpec((t, wd), lambda b: (b, 0))
    outs = pl.pallas_call(
        _ctx_attn_kernel, grid=(nb,),
        in_specs=[row(256), row(256), row(256), row(256), row(128), row(128), row(MLA_QK_W), row(MLA_QK_W),
                  row(LANE), row(256), pl.BlockSpec(memory_space=pltpu.SMEM)],
        out_specs=[row(256)] * 3,
        out_shape=[jax.ShapeDtypeStruct((nb * t, BRANCH_W), F32)] * 3,
        compiler_params=_cp(("parallel",)), name="ctx_attention",
    )(p["na_q"], p["na_k"], p["na_v"], p["sw_q"], p["sw_k"], p["sw_v"], qm, kn, p["krope"], vm, sink)
    return outs


NA_TQ = 256


def _na_bias_tables(rpb, rows_n):
    nr, nc = 2 * NA_KR_MAX - 1, 2 * NA_KC - 1
    p_sel = np.zeros((3, 4, 12, nr), np.float32)
    ok_r = np.zeros((3, 4, 12), bool)
    for ti, j in enumerate((0, 1, rows_n // 4 - 1)):
        for a in range(4):
            r = 4 * j + a
            r0 = min(max(r - NA_KR_MAX // 2, 0), rows_n - NA_KR_MAX)
            for e in range(12):
                krow = 4 * (j - 1) + e
                if r0 <= krow < r0 + NA_KR_MAX:
                    ok_r[ti, a, e] = True
                    p_sel[ti, a, e, krow - r + NA_KR_MAX - 1] = 1.0
    c = np.arange(GRID_W)[:, None]
    w = np.arange(GRID_W)[None, :]
    cs = np.clip(c - NA_KC // 2, 0, GRID_W - NA_KC)
    ok_c = (w >= cs) & (w < cs + NA_KC)
    q_sel = np.eye(nc, dtype=np.float32)[np.clip(w - c + NA_KC - 1, 0, nc - 1)]
    hp = lax.Precision.HIGHEST
    m1 = jnp.einsum("taei,lhij->lthaej", jnp.asarray(p_sel), rpb, precision=hp)
    b = jnp.einsum("lthaej,cwj->lthacew", m1, jnp.asarray(q_sel), precision=hp)
    ok = ok_r[:, :, None, :, None] & ok_c[None, None, :, None, :]
    b = jnp.where(jnp.asarray(ok)[None, :, None], b, NEG)
    return b.reshape(rpb.shape[0], 3, NA_HEADS, 4 * GRID_W, 12 * GRID_W)


def _na_kernel(q, k0, k1, k2, v0, v1, v2, kc, vc, bias, o):
    hd = HEAD_DIM
    scale = hd ** -0.5
    qb = q[...].astype(BF16)
    kk = jnp.concatenate([k0[...], k1[...], k2[...]], axis=0).astype(BF16)
    vv = jnp.concatenate([v0[...], v1[...], v2[...]], axis=0).astype(BF16)
    kcb, vcb = kc[...].astype(BF16), vc[...].astype(BF16)
    for h in range(NA_HEADS):
        hs = slice(h * hd, (h + 1) * hd)
        s_nb = _dot_nt(qb[:, hs], kk[:, hs]) * scale + bias[h]
        s_cx = _dot_nt(qb[:, hs], kcb[:, hs]) * scale
        o[:, hs] = _softmax_pv([(s_nb, vv[:, hs]), (s_cx, vcb[:, hs])])


def _na_attention(q, k, v, kc, vc, bias, nb, t):
    nj = t // NA_TQ
    lc = kc.shape[1]
    blk = lambda f: pl.BlockSpec((NA_TQ, BRANCH_W), lambda b, j: (b * nj + f(j), 0))
    prev = lambda j: jnp.maximum(j - 1, 0)
    nxt = lambda j: jnp.minimum(j + 1, nj - 1)
    cur = lambda j: j
    ctx = pl.BlockSpec((None, lc, BRANCH_W), lambda b, j: (b, 0, 0))
    tsel = lambda b, j: (jnp.where(j == 0, 0, jnp.where(j == nj - 1, 2, 1)), 0, 0, 0)
    return pl.pallas_call(
        _na_kernel, grid=(nb, nj),
        in_specs=[blk(cur), blk(prev), blk(cur), blk(nxt), blk(prev), blk(cur), blk(nxt), ctx, ctx,
                  pl.BlockSpec((None,) + bias.shape[1:], tsel)],
        out_specs=blk(cur),
        out_shape=jax.ShapeDtypeStruct((nb * t, BRANCH_W), F32),
        compiler_params=_cp(("parallel", "arbitrary"), 48), name="na_attention",
    )(q, k, k, k, v, v, v, kc, vc, bias)


SW_TQ = 256


def _sw_kernel(q, k0, k1, k2, v0, v1, v2, kc, vc, cq, saq, sbq, ck0, sak0, sbk0, ck1, sak1, sbk1,
               ck2, sak2, sbk2, sink_ref, o, *, t):
    hd = HEAD_DIM
    scale = hd ** -0.5
    half = hd // 4
    j = pl.program_id(1)
    qr = _rope(q[...], cq[...], saq[...], sbq[...], half).astype(BF16)
    kk = jnp.concatenate([_rope(k0[...], ck0[...], sak0[...], sbk0[...], half),
                          _rope(k1[...], ck1[...], sak1[...], sbk1[...], half),
                          _rope(k2[...], ck2[...], sak2[...], sbk2[...], half)], axis=0).astype(BF16)
    vv = jnp.concatenate([v0[...], v1[...], v2[...]], axis=0).astype(BF16)
    kcb, vcb = kc[...].astype(BF16), vc[...].astype(BF16)
    qpos = j * SW_TQ + lax.broadcasted_iota(jnp.int32, (SW_TQ, 3 * SW_TQ), 0)
    kpos = (j - 1) * SW_TQ + lax.broadcasted_iota(jnp.int32, (SW_TQ, 3 * SW_TQ), 1)
    valid = (jnp.abs(qpos - kpos) <= SW_WINDOW) & (kpos >= 0) & (kpos < t)
    for h in range(SW_HEADS):
        g = h // SW_GROUP
        hs, gs = slice(h * hd, (h + 1) * hd), slice(g * hd, (g + 1) * hd)
        s_loc = jnp.where(valid, _dot_nt(qr[:, hs], kk[:, gs]) * scale, NEG)
        s_cx = _dot_nt(qr[:, hs], kcb[:, gs]) * scale
        o[:, hs] = _softmax_pv([(s_loc, vv[:, gs]), (s_cx, vcb[:, gs])], sink=sink_ref[h])


def _sw_attention(q, k, v, kc, vc, sink, tabs_q, tabs_k, nb, t):
    nj = t // SW_TQ
    lc = kc.shape[1]
    kvw = SW_KV_HEADS * HEAD_DIM
    prev = lambda j: jnp.maximum(j - 1, 0)
    nxt = lambda j: jnp.minimum(j + 1, nj - 1)
    cur = lambda j: j
    qblk = pl.BlockSpec((SW_TQ, BRANCH_W), lambda b, j: (b * nj + j, 0))
    kblk = lambda f: pl.BlockSpec((SW_TQ, kvw), lambda b, j: (b * nj + f(j), 0))
    ctx = pl.BlockSpec((None, lc, kvw), lambda b, j: (b, 0, 0))
    tq = pl.BlockSpec((SW_TQ, BRANCH_W), lambda b, j: (j, 0))
    tk = lambda f: pl.BlockSpec((SW_TQ, kvw), lambda b, j: (f(j), 0))
    in_specs = [qblk, kblk(prev), kblk(cur), kblk(nxt), kblk(prev), kblk(cur), kblk(nxt), ctx, ctx, tq, tq, tq]
    args = [q, k, k, k, v, v, v, kc, vc, *tabs_q]
    for f in (prev, cur, nxt):
        in_specs += [tk(f)] * 3
        args += list(tabs_k)
    in_specs.append(pl.BlockSpec(memory_space=pltpu.SMEM))
    args.append(sink)
    return pl.pallas_call(
        functools.partial(_sw_kernel, t=t), grid=(nb, nj), in_specs=in_specs, out_specs=qblk,
        out_shape=jax.ShapeDtypeStruct((nb * t, BRANCH_W), F32),
        compiler_params=_cp(("parallel", "arbitrary"), 48), name="sw_attention",
    )(*args)


MLA_TQ = 512
MLA_TK = 512


def _mla_kernel(qm, kn, kr, vt, knc, krc, vtc, cq, saq, sbq, ck, sak, sbk, o, qt_ref, m_ref, l_ref, acc_ref, *,
                n_lat):
    kj = pl.program_id(2)
    half = MLA_ROPE // 4

    @pl.when(kj == 0)
    def _():
        qt_ref[...] = _rope(qm[...], cq[...], saq[...], sbq[...], half).T.astype(BF16)
        m_ref[...] = jnp.full(m_ref.shape, NEG, F32)
        l_ref[...] = jnp.zeros(l_ref.shape, F32)
        acc_ref[...] = jnp.zeros(acc_ref.shape, F32)

    def step(kcat, vt_b):
        for h in range(MLA_HEADS):
            ks = slice(h * LANE, (h + 1) * LANE)
            vs = slice(h * MLA_V, (h + 1) * MLA_V)
            s = _dot(kcat[:, ks], qt_ref[ks, :]) * MLA_SCALE
            m_old = m_ref[h]
            m_new = jnp.maximum(m_old, jnp.max(s, axis=0, keepdims=True))
            alpha = jnp.exp(m_old - m_new)
            p = jnp.exp(s - m_new)
            l_ref[h] = alpha * l_ref[h] + jnp.sum(p, axis=0, keepdims=True)
            acc_ref[vs, :] = alpha * acc_ref[vs, :] + _dot(vt_b[vs, :], p.astype(BF16))
            m_ref[h] = m_new

    @pl.when(kj < n_lat)
    def _():
        krr = _rope(kr[...], ck[...], sak[...], sbk[...], half)
        step((kn[...] + jnp.concatenate([krr] * MLA_HEADS, axis=1)).astype(BF16), vt[...].astype(BF16))

    @pl.when(kj == n_lat)
    def _():
        step((knc[...] + jnp.concatenate([krc[...]] * MLA_HEADS, axis=1)).astype(BF16), vtc[...].astype(BF16))
        for h in range(MLA_HEADS):
            vs = slice(h * MLA_V, (h + 1) * MLA_V)
            acc_ref[vs, :] = acc_ref[vs, :] / l_ref[h]
        o[...] = acc_ref[...].T


def _mla_attention(qm, kn, kr, vt, knc, krc, vtc, tabs_q, tabs_k, nb, t):
    nq, n_lat = t // MLA_TQ, t // MLA_TK
    lc = krc.shape[1]
    assert lc == MLA_TK
    kb = lambda b, i, kj: b * n_lat + jnp.minimum(kj, n_lat - 1)
    kblk = lambda wd: pl.BlockSpec((MLA_TK, wd), lambda b, i, kj: (kb(b, i, kj), 0))
    tkb = pl.BlockSpec((MLA_TK, LANE), lambda b, i, kj: (jnp.minimum(kj, n_lat - 1), 0))
    tqb = pl.BlockSpec((MLA_TQ, MLA_QK_W), lambda b, i, kj: (i, 0))
    return pl.pallas_call(
        functools.partial(_mla_kernel, n_lat=n_lat), grid=(nb, nq, n_lat + 1),
        in_specs=[pl.BlockSpec((MLA_TQ, MLA_QK_W), lambda b, i, kj: (b * nq + i, 0)),
                  kblk(MLA_QK_W), kblk(LANE),
                  pl.BlockSpec((BRANCH_W, MLA_TK), lambda b, i, kj: (0, kb(b, i, kj))),
                  pl.BlockSpec((lc, MLA_QK_W), lambda b, i, kj: (b, 0)),
                  pl.BlockSpec((None, lc, LANE), lambda b, i, kj: (b, 0, 0)),
                  pl.BlockSpec((BRANCH_W, lc), lambda b, i, kj: (0, b)),
                  tqb, tqb, tqb, tkb, tkb, tkb],
        out_specs=pl.BlockSpec((MLA_TQ, BRANCH_W), lambda b, i, kj: (b * nq + i, 0)),
        out_shape=jax.ShapeDtypeStruct((nb * t, BRANCH_W), F32),
        scratch_shapes=[pltpu.VMEM((MLA_QK_W, MLA_TQ), BF16), pltpu.VMEM((MLA_HEADS, 1, MLA_TQ), F32),
                        pltpu.VMEM((MLA_HEADS, 1, MLA_TQ), F32), pltpu.VMEM((BRANCH_W, MLA_TQ), F32)],
        compiler_params=_cp(("parallel", "parallel", "arbitrary"), 48), name="mla_attention",
    )(qm, kn, kr, vt, knc, krc, vtc, *tabs_q, *tabs_k)


def _log_sigmoid(x):
    return jnp.minimum(x, 0.0) - jnp.log(1.0 + jnp.exp(-jnp.abs(x)))


def _split3(x):
    hi = x.astype(BF16)
    r1 = x - hi.astype(F32)
    mid = r1.astype(BF16)
    lo = (r1 - mid.astype(F32)).astype(BF16)
    return jnp.concatenate([hi, mid, lo], axis=1)


def _mlstm_dir(d, q_ref, k_ref, v_ref, if_ref, o_ref, c_aug, m_prev, negsel_ref):
    ln = ML_CHUNK
    scale = HEAD_DIM ** -0.5
    t_in = if_ref[...]
    logf = _log_sigmoid(t_in)
    ri = lax.broadcasted_iota(jnp.int32, (ln, ln), 0)
    ci = lax.broadcasted_iota(jnp.int32, (ln, ln), 1)
    mask = (ci <= ri) if d == 0 else (ci >= ri)
    tri = jnp.where(mask, 1.0, 0.0).astype(BF16)
    fs = _split3(logf)
    bcol = _dot(tri, fs[:, :LANE]) + _dot(tri, fs[:, LANE:2 * LANE]) + _dot(tri, fs[:, 2 * LANE:])
    b = pltpu.roll(bcol, LANE - 2 * ML_HEADS, 1)
    a = t_in - b
    row = lax.broadcasted_iota(jnp.int32, (ln, LANE), 0)
    cm, sh = a, 1
    while sh < ln:
        if d == 0:
            cm = jnp.maximum(cm, jnp.where(row >= sh, pltpu.roll(cm, sh, 0), NEG))
        else:
            cm = jnp.maximum(cm, jnp.where(row < ln - sh, pltpu.roll(cm, ln - sh, 0), NEG))
        sh *= 2
    big_m = jnp.maximum(m_prev, cm)
    xs = _split3(big_m - m_prev)
    ys = _split3(b + big_m)
    at = (a - m_prev).T
    last = ln - 1 if d == 0 else 0
    m_last = big_m[last:last + 1, :]
    m_new = b[last:last + 1, :] + m_last
    decay = jnp.exp(m_prev - m_last)
    wt = jnp.exp(a - m_last).T
    q, k, v = q_ref[...], k_ref[...], v_ref[...]
    new_c = []
    for h in range(ML_HEADS):
        c = d * ML_HEADS + h
        hsl = slice(h * LANE, (h + 1) * LANE)
        neg_x = _dot(xs, negsel_ref[c])
        neg_y = _dot(ys, negsel_ref[c])
        w_intra = jnp.exp(jnp.where(mask, at[c:c + 1, :] + neg_x, NEG))
        kh = k[:, hsl] * scale
        qb, kb, vb = q[:, hsl].astype(BF16), kh.astype(BF16), v[:, hsl].astype(BF16)
        s = _dot_nt(qb, kb) * w_intra
        c_prev = c_aug[c]
        nd = _dot(s.astype(BF16), vb) + jnp.exp(neg_x) * _dot(qb, c_prev.astype(BF16))
        den = pltpu.roll(nd, HEAD_DIM, 1)
        hout = nd / jnp.maximum(jnp.abs(den), jnp.exp(neg_y))
        o_ref[:, h * HEAD_DIM:(h + 1) * HEAD_DIM] = hout[:, :HEAD_DIM]
        new_c.append(decay[:, c:c + 1] * c_prev + _dot((kh.T * wt[c:c + 1, :]).astype(BF16), vb))
    return new_c, m_new


def _mlstm_kernel(qf, kf, vf, iff, qb, kb, vb, ifb, c0, m0, negsel, hf, hb, c_out, m_out, c_s, m_s, *, nc):
    c = pl.program_id(1)
    nst = 2 * ML_HEADS

    @pl.when(c == 0)
    def _():
        c_s[...] = c0[...]
        m_s[...] = m0[...]

    c_aug = [c_s[i] for i in range(nst)]
    m_prev = m_s[...]
    cf, mf = _mlstm_dir(0, qf, kf, vf, iff, hf, c_aug, m_prev, negsel)
    cb, mb = _mlstm_dir(1, qb, kb, vb, ifb, hb, c_aug, m_prev, negsel)
    for i, cn in enumerate(cf + cb):
        c_s[i] = cn
    lane = lax.broadcasted_iota(jnp.int32, (1, LANE), 1)
    m_s[...] = jnp.where(lane < ML_HEADS, mf, mb)

    @pl.when(c == nc - 1)
    def _():
        c_out[...] = c_s[...]
        m_out[...] = m_s[...]


def _mlstm(q, k, v, gif, c0, n0, m0, nb, t):
    nc = t // ML_CHUNK
    nst = 2 * ML_HEADS
    hd = HEAD_DIM
    qkw = ML_HEADS * LANE
    c_aug0 = jnp.concatenate([c0.reshape(nb, nst, hd, hd),
                              jnp.broadcast_to(n0.reshape(nb, nst, hd, 1), (nb, nst, hd, hd))], axis=-1)
    c_aug0 = jnp.pad(c_aug0, ((0, 0), (0, 0), (0, LANE - hd), (0, 0)))
    m_row0 = jnp.pad(m0.reshape(nb, 1, nst), ((0, 0), (0, 0), (0, LANE - nst)))
    sel = np.zeros((nst, 3 * LANE, LANE), np.float32)
    for ch in range(nst):
        sel[ch, [ch, LANE + ch, 2 * LANE + ch], :] = -1.0
    fw = lambda wd: pl.BlockSpec((ML_CHUNK, wd), lambda b, c: (b * nc + c, 0))
    bw = lambda wd: pl.BlockSpec((ML_CHUNK, wd), lambda b, c: (b * nc + nc - 1 - c, 0))
    st = lambda shp: pl.BlockSpec((None,) + shp, lambda b, c: (b,) + (0,) * len(shp))
    shapes = ((nst, LANE, LANE), (1, LANE))
    hf, hb, c_fin, m_fin = pl.pallas_call(
        functools.partial(_mlstm_kernel, nc=nc), grid=(nb, nc),
        in_specs=[fw(qkw), fw(qkw), fw(qkw), fw(LANE), bw(qkw), bw(qkw), bw(qkw), bw(LANE)]
                 + [st(s) for s in shapes] + [pl.BlockSpec(sel.shape, lambda b, c: (0, 0, 0))],
        out_specs=[fw(BRANCH_W), bw(BRANCH_W)] + [st(s) for s in shapes],
        out_shape=[jax.ShapeDtypeStruct((nb * t, BRANCH_W), F32)] * 2
                  + [jax.ShapeDtypeStruct((nb,) + s, F32) for s in shapes],
        scratch_shapes=[pltpu.VMEM(s, F32) for s in shapes],
        compiler_params=_cp(("parallel", "arbitrary")), name="mlstm_scan",
    )(q, k, v, gif, q, k, v, gif, c_aug0, m_row0, jnp.asarray(sel, BF16))
    c_new = c_fin[:, :, :hd, :hd].reshape(nb, 2, ML_HEADS, hd, hd)
    n_new = c_fin[:, :, :hd, hd].reshape(nb, 2, ML_HEADS, hd)
    m_new = m_fin[:, 0, :nst].reshape(nb, 2, ML_HEADS)
    return hf, hb, c_new, n_new, m_new


def _prep_weights(w_in, b_in, mla_q_norm, w_uq, w_uk, w_uv, w_branch, w_out, peer_wq, peer_keys, peer_u, peer_v):
    offs = np.cumsum((0,) + IN_SPLITS)
    seg = lambda a, i: a[..., offs[i]:offs[i + 1]]
    padr = lambda a, wd, lead=0: jnp.pad(a, [(0, 0)] * (a.ndim - 1) + [(lead, wd - a.shape[-1] - lead)])

    def headpad(a, fill):
        a4 = a.reshape(a.shape[:-1] + (ML_HEADS, HEAD_DIM))
        a4 = jnp.pad(a4, [(0, 0)] * (a4.ndim - 1) + [(0, LANE - HEAD_DIM)], constant_values=fill)
        return a4.reshape(a.shape[:-1] + (ML_HEADS * LANE,))

    def layout(a, is_bias):
        ml_if = padr(jnp.concatenate([seg(a, 7), seg(a, 8)], axis=-1), LANE)
        ml = [headpad(seg(a, 3), 0.0), headpad(seg(a, 4), 0.0), headpad(seg(a, 5), 1.0 if is_bias else 0.0)]
        parts = [seg(a, i) for i in range(3)] + ml + [seg(a, 6), ml_if, seg(a, 9), seg(a, 10), seg(a, 11),
                                                       padr(seg(a, 12), 256), seg(a, 13),
                                                       padr(seg(a, 14), LANE, KROPE_SLICE.start)]
        return jnp.concatenate(parts, axis=-1)

    nl = w_in.shape[0]
    w_uq_p = padr(w_uq.reshape(nl, MLA_Q_RANK, MLA_HEADS, MLA_NOPE + MLA_ROPE), LANE).reshape(nl, MLA_Q_RANK, -1)
    w_uq_p = jnp.pad(w_uq_p, ((0, 0), (0, 256 - MLA_Q_RANK), (0, 0)))
    w_uk_p = padr(w_uk.reshape(nl, MLA_KV_RANK, MLA_HEADS, MLA_NOPE), LANE).reshape(nl, MLA_KV_RANK, -1)
    return dict(
        w_ab=layout(w_in, False).astype(BF16), b_ab=layout(b_in, True),
        w_c=seg(w_in, 15).astype(BF16), b_c=seg(b_in, 15),
        q_norm=padr(mla_q_norm, 256), w_uq=w_uq_p.astype(BF16),
        w_ukv=jnp.concatenate([w_uk_p, w_uv], axis=-1).astype(BF16),
        w_branch=w_branch.astype(BF16), w_out=w_out.astype(BF16), peer_wq=peer_wq.astype(BF16),
        peer_keys=peer_keys.astype(BF16), peer_u=peer_u.astype(BF16),
        peer_vt=jnp.swapaxes(peer_v.reshape(nl, -1, PEER_EC, peer_v.shape[-1]), 2, 3).astype(BF16),
    )


def _project(x, wl, l, norm_g, sc, sh, rows_per_mod, bf16_names=()):
    names = [name for name, _ in SEGS_AB]
    outs = _linear(x, wl["w_ab"][l], [wd for _, wd in SEGS_AB], bias=wl["b_ab"][l], norm_g=norm_g,
                   mod=(sc, sh), rows_per_mod=rows_per_mod, bf16_out=tuple(names.index(nm) for nm in bf16_names),
                   tm=256, vmem_mb=48, name="in_proj_ab")
    p = {name: o for (name, _), o in zip(SEGS_AB, outs)}
    (p["gate"],) = _linear(x, wl["w_c"][l], [N_BRANCH * D_MODEL], bias=wl["b_c"][l], norm_g=norm_g,
                           mod=(sc, sh), rows_per_mod=rows_per_mod, bf16_out=(0,), tm=256, vmem_mb=48,
                           name="in_proj_gate")
    (qm,) = _linear(p["cq"], wl["w_uq"][l], [MLA_QK_W], norm_g=wl["q_norm"][l], k_valid=MLA_Q_RANK, tm=512,
                    name="mla_q_up")
    return p, qm


def kernel(x_prompt, x_sample, cache_na_k, cache_na_v, state_mlstm_C, state_mlstm_n, state_mlstm_m,
           cache_swa_k, cache_swa_v, cache_mla_ckv, cache_mla_krope, c, c_ctx, w_mod, b_mod, norm1_g, norm2_g,
           w_in, b_in, na_rpb, sw_sink, mla_q_norm, w_uq, mla_kv_norm, w_uk, w_uv, w_branch, w_out, peer_wq,
           peer_keys, peer_u, peer_v, final_norm_g):
    nbp, tp, d = x_prompt.shape
    nbs, ts, _ = x_sample.shape
    nl = w_in.shape[0]
    lc = cache_na_k.shape[2]
    nst = 2 * ML_HEADS
    wl = _prep_weights(w_in, b_in, mla_q_norm, w_uq, w_uk, w_uv, w_branch, w_out, peer_wq, peer_keys,
                       peer_u, peer_v)

    n_cond = 1 + nbs
    cond = jnp.zeros((16, d), F32).at[0].set(c_ctx).at[1:n_cond].set(c)
    mods = _modulation(cond, w_mod, b_mod).reshape(nl, 16, 6, 1, d)

    tabs_swq = _rope_tables(ts, SW_HEADS, HEAD_DIM)
    tabs_swk = _rope_tables(ts, SW_KV_HEADS, HEAD_DIM)
    tabs_mlaq = _rope_tables(ts, MLA_HEADS, MLA_ROPE, lead=MLA_NOPE, width=LANE)
    tabs_mlak = _rope_tables(ts, 1, MLA_ROPE, lead=MLA_NOPE, width=LANE)
    na_bias = _na_bias_tables(na_rpb, ts // GRID_W)
    krope_c = jnp.pad(cache_mla_krope, ((0, 0), (0, 0), (0, 0), (KROPE_SLICE.start, LANE - KROPE_SLICE.stop)))

    xp = x_prompt.reshape(nbp * tp, d)
    xs = x_sample.reshape(nbs * ts, d)
    zeros_c = jnp.zeros((nbp, 2, ML_HEADS, HEAD_DIM, HEAD_DIM), F32)
    zeros_n = jnp.zeros((nbp, 2, ML_HEADS, HEAD_DIM), F32)
    zeros_m = jnp.zeros((nbp, 2, ML_HEADS), F32)
    per_layer = tuple([] for _ in range(9))

    for l in range(nl):
        mod_p = [mods[l, 0:1, i] for i in range(6)]
        mod_s = [mods[l, 1:n_cond, i] for i in range(6)]
        sink = sw_sink[l]

        p, qm = _project(xp, wl, l, norm1_g[l], mod_p[1], mod_p[0], nbp * tp)
        ckv_n, kn, vm = _linear(p["ckv"], wl["w_ukv"][l], [MLA_QK_W, 256], norm_g=mla_kv_norm[l],
                                emit_normed=True, tm=512, name="mla_kv_up")
        o_na, o_sw, o_mla = _ctx_attention(p, qm, kn, vm, sink, nbp, tp)
        hf, hb, c_new, n_new, m_new = _mlstm(p["ml_q"], p["ml_k"], p["ml_v"], p["ml_if"], zeros_c, zeros_n,
                                             zeros_m, nbp, tp)
        xp = _merge(o_na, hf, hb, p["ml_o"], o_sw, o_mla, p["gate"], wl["w_branch"][l], wl["w_out"][l], xp,
                    mod_p[2], nbp * tp)
        xp = _peer(xp, norm2_g[l], mod_p[4], mod_p[3], mod_p[5], wl["peer_wq"][l], wl["peer_keys"][l],
                   wl["peer_u"][l], wl["peer_vt"][l], nbp * tp)
        ctx_t = (p["na_k"].reshape(nbp, tp, NA_HEADS, HEAD_DIM), p["na_v"].reshape(nbp, tp, NA_HEADS, HEAD_DIM),
                 c_new, n_new, m_new,
                 p["sw_k"].reshape(nbp, tp, SW_KV_HEADS, HEAD_DIM), p["sw_v"].reshape(nbp, tp, SW_KV_HEADS, HEAD_DIM),
                 ckv_n.reshape(nbp, tp, MLA_KV_RANK), p["krope"][:, KROPE_SLICE].reshape(nbp, tp, MLA_ROPE))
        for i, a in enumerate(ctx_t):
            per_layer[i].append(a)

        p, qm = _project(xs, wl, l, norm1_g[l], mod_s[1], mod_s[0], ts,
                         bf16_names=("na_q", "na_k", "na_v", "ml_q", "ml_v", "sw_v"))
        _, kn, vt = _linear(p["ckv"], wl["w_ukv"][l], [MLA_QK_W, 256], norm_g=mla_kv_norm[l], emit_normed=True,
                            transpose_last=True, tm=512, name="mla_kv_up_t")
        knc, vtc = _linear(cache_mla_ckv[:, l].reshape(nbs * lc, MLA_KV_RANK), wl["w_ukv"][l], [MLA_QK_W, 256],
                           transpose_last=True, tm=512, name="mla_kv_up_cache")
        o_na = _na_attention(p["na_q"], p["na_k"], p["na_v"], cache_na_k[:, l].reshape(nbs, lc, BRANCH_W),
                             cache_na_v[:, l].reshape(nbs, lc, BRANCH_W), na_bias[l], nbs, ts)
        o_sw = _sw_attention(p["sw_q"], p["sw_k"], p["sw_v"],
                             cache_swa_k[:, l].reshape(nbs, lc, SW_KV_HEADS * HEAD_DIM),
                             cache_swa_v[:, l].reshape(nbs, lc, SW_KV_HEADS * HEAD_DIM), sink, tabs_swq, tabs_swk,
                             nbs, ts)
        o_mla = _mla_attention(qm, kn, p["krope"], vt, knc, krope_c[:, l], vtc, tabs_mlaq, tabs_mlak, nbs, ts)
        hf, hb, _, _, _ = _mlstm(p["ml_q"], p["ml_k"], p["ml_v"], p["ml_if"], state_mlstm_C[:, l],
                                 state_mlstm_n[:, l], state_mlstm_m[:, l], nbs, ts)
        xs = _merge(o_na, hf, hb, p["ml_o"], o_sw, o_mla, p["gate"], wl["w_branch"][l], wl["w_out"][l], xs,
                    mod_s[2], ts)
        xs = _peer(xs, norm2_g[l], mod_s[4], mod_s[3], mod_s[5], wl["peer_wq"][l], wl["peer_keys"][l],
                   wl["peer_u"][l], wl["peer_vt"][l], ts)

    y_prompt = _final_norm(xp, final_norm_g).reshape(nbp, tp, d)
    y_sample = _final_norm(xs, final_norm_g).reshape(nbs, ts, d)
    return (y_prompt, y_sample) + tuple(jnp.stack(s, axis=1) for s in per_layer)
```

```python
import functools

import numpy as np
import jax
import jax.numpy as jnp
from jax import lax
from jax.experimental import pallas as pl
from jax.experimental.pallas import tpu as pltpu

F32 = jnp.float32
BF16 = jnp.bfloat16

D_MODEL = 1024
GRID_W = 64
HEAD_DIM = 64
N_BRANCH = 4
BRANCH_W = 256
NA_HEADS = 4
NA_KR_MAX = 8
NA_KC = 16
ML_HEADS = 4
ML_CHUNK = 128
SW_HEADS = 4
SW_KV_HEADS = 2
SW_GROUP = SW_HEADS // SW_KV_HEADS
SW_WINDOW = 128
MLA_HEADS = 4
MLA_Q_RANK = 192
MLA_KV_RANK = 128
MLA_NOPE = 64
MLA_ROPE = 32
MLA_V = 64
MLA_SCALE = (MLA_NOPE + MLA_ROPE) ** -0.5
PEER_HEADS = 8
PEER_KEY_DIM = 64
PEER_N_KEYS = 128
PEER_TOPK = 16
PEER_EC = 512
ROPE_BASE = 10000.0
RMS_EPS = 1e-6
NEG = -1e30
IN_SPLITS = (256, 256, 256, 256, 256, 256, 256, 8, 8, 256, 128, 128, 192, 128, 32, 4096)

LANE = 128
MLA_QK_W = MLA_HEADS * LANE

SEGS_AB = (("na_q", 256), ("na_k", 256), ("na_v", 256), ("ml_q", 512), ("ml_k", 512), ("ml_v", 512),
           ("ml_o", 256), ("ml_if", 128), ("sw_q", 256), ("sw_k", 128), ("sw_v", 128), ("cq", 256),
           ("ckv", 128), ("krope", 128))
KROPE_SLICE = slice(MLA_NOPE, MLA_NOPE + MLA_ROPE)


def _cp(sem, vmem_mb=None):
    kw = dict(dimension_semantics=sem)
    if vmem_mb is not None:
        kw["vmem_limit_bytes"] = vmem_mb * 1024 * 1024
    return pltpu.CompilerParams(**kw)


def _dot(a, b):
    return jnp.dot(a, b, preferred_element_type=F32)


def _dot_nt(a, b):
    return lax.dot_general(a, b, (((1,), (1,)), ((), ())), preferred_element_type=F32)


def _rms(x, inv_n):
    return x * lax.rsqrt(jnp.sum(x * x, axis=-1, keepdims=True) * inv_n + RMS_EPS)


def _sigmoid(x):
    return 1.0 / (1.0 + jnp.exp(-x))


def _mod_kernel(c_ref, w_ref, b_ref, o_ref):
    c = c_ref[...]
    s = c * _sigmoid(c)
    o_ref[...] = _dot(s.astype(BF16), w_ref[...].astype(BF16)) + b_ref[...]


def _modulation(cond, w_mod, b_mod):
    nl, d, n = w_mod.shape
    r = cond.shape[0]
    tn = 1024
    return pl.pallas_call(
        _mod_kernel,
        grid=(nl, n // tn),
        in_specs=[pl.BlockSpec((r, d), lambda l, j: (0, 0)),
                  pl.BlockSpec((None, d, tn), lambda l, j: (l, 0, j)),
                  pl.BlockSpec((None, 1, tn), lambda l, j: (l, 0, j))],
        out_specs=pl.BlockSpec((None, r, tn), lambda l, j: (l, 0, j)),
        out_shape=jax.ShapeDtypeStruct((nl, r, n), F32),
        compiler_params=_cp(("parallel", "parallel")),
        name="adaln_modulation",
    )(cond, w_mod, b_mod.reshape(nl, 1, n))


def _linear_kernel(*refs, segs, has_norm, has_mod, has_bias, emit_normed, transpose_last, inv_k):
    it = iter(refs)
    x_ref = next(it)
    g_ref = next(it) if has_norm else None
    sc_ref = next(it) if has_mod else None
    sh_ref = next(it) if has_mod else None
    w_ref = next(it)
    b_ref = next(it) if has_bias else None
    outs = list(it)
    x = x_ref[...]
    if has_norm:
        x = _rms(x, inv_k) * g_ref[...]
    if has_mod:
        x = x * (1.0 + sc_ref[0]) + sh_ref[0]
    if emit_normed:
        outs[0][...] = x
        outs = outs[1:]
    xb = x.astype(BF16)
    for idx, ((s, wd), o) in enumerate(zip(segs, outs)):
        y = _dot(xb, w_ref[:, s:s + wd])
        if has_bias:
            y = y + b_ref[:, s:s + wd]
        o[...] = y.T if (transpose_last and idx == len(segs) - 1) else y


def _linear(x, w, widths, *, bias=None, norm_g=None, k_valid=None, mod=None, rows_per_mod=None,
            emit_normed=False, transpose_last=False, tm=256, vmem_mb=None, name="linear"):
    n, k = x.shape
    tm = min(tm, n)
    segs, s = [], 0
    for wd in widths:
        segs.append((s, wd))
        s += wd
    assert s == w.shape[1] and n % tm == 0
    args = [x]
    in_specs = [pl.BlockSpec((tm, k), lambda i: (i, 0))]
    if norm_g is not None:
        args.append(norm_g.reshape(1, k))
        in_specs.append(pl.BlockSpec((1, k), lambda i: (0, 0)))
    if mod is not None:
        assert rows_per_mod % tm == 0
        for m in mod:
            args.append(m)
            in_specs.append(pl.BlockSpec((1, 1, k), lambda i: ((i * tm) // rows_per_mod, 0, 0)))
    args.append(w)
    in_specs.append(pl.BlockSpec(w.shape, lambda i: (0, 0)))
    if bias is not None:
        args.append(bias.reshape(1, -1))
        in_specs.append(pl.BlockSpec((1, w.shape[1]), lambda i: (0, 0)))
    out_shapes, out_specs = [], []
    if emit_normed:
        out_shapes.append(jax.ShapeDtypeStruct((n, k), F32))
        out_specs.append(pl.BlockSpec((tm, k), lambda i: (i, 0)))
    for idx, wd in enumerate(widths):
        if transpose_last and idx == len(widths) - 1:
            out_shapes.append(jax.ShapeDtypeStruct((wd, n), F32))
            out_specs.append(pl.BlockSpec((wd, tm), lambda i: (0, i)))
        else:
            out_shapes.append(jax.ShapeDtypeStruct((n, wd), F32))
            out_specs.append(pl.BlockSpec((tm, wd), lambda i: (i, 0)))
    kern = functools.partial(_linear_kernel, segs=tuple(segs), has_norm=norm_g is not None,
                             has_mod=mod is not None, has_bias=bias is not None, emit_normed=emit_normed,
                             transpose_last=transpose_last, inv_k=1.0 / (k_valid or k))
    return pl.pallas_call(kern, grid=(n // tm,), in_specs=in_specs, out_specs=out_specs,
                          out_shape=out_shapes, compiler_params=_cp(("parallel",), vmem_mb), name=name)(*args)


def _merge_kernel(ona, hf, hb, mlo, osw, omla, gate, wb, wo, x, g1, bd, o):
    h = hf[...] + hb[...]
    hsq = h * h
    hi = hsq.astype(BF16)
    lo = (hsq - hi.astype(F32)).astype(BF16)
    ms = (_dot(hi, bd[...]) + _dot(lo, bd[...])) * (1.0 / HEAD_DIM)
    oml = h * lax.rsqrt(ms + RMS_EPS) * _sigmoid(mlo[...])
    branches = (ona[...], oml, osw[...], omla[...])
    acc = None
    for n in range(N_BRANCH):
        y = _sigmoid(gate[:, n * D_MODEL:(n + 1) * D_MODEL]) * _dot(branches[n].astype(BF16), wb[n])
        acc = y if acc is None else acc + y
    o[...] = x[...] + g1[0] * _dot(acc.astype(BF16), wo[...])


def _merge(ona, hf, hb, mlo, osw, omla, gate, wb, wo, x, g1, rows_per_mod, tm=512):
    n = x.shape[0]
    tm = min(tm, n)
    bd = jnp.asarray(np.kron(np.eye(BRANCH_W // HEAD_DIM), np.ones((HEAD_DIM, HEAD_DIM))), BF16)
    row = lambda wd: pl.BlockSpec((tm, wd), lambda i: (i, 0))
    return pl.pallas_call(
        _merge_kernel,
        grid=(n // tm,),
        in_specs=[row(BRANCH_W)] * 6 + [row(N_BRANCH * D_MODEL),
                  pl.BlockSpec(wb.shape, lambda i: (0, 0, 0)),
                  pl.BlockSpec(wo.shape, lambda i: (0, 0)),
                  row(D_MODEL),
                  pl.BlockSpec((1, 1, D_MODEL), lambda i: ((i * tm) // rows_per_mod, 0, 0)),
                  pl.BlockSpec(bd.shape, lambda i: (0, 0))],
        out_specs=row(D_MODEL),
        out_shape=jax.ShapeDtypeStruct((n, D_MODEL), F32),
        compiler_params=_cp(("parallel",), 48),
        name="branch_merge",
    )(ona, hf, hb, mlo, osw, omla, gate, wb, wo, x, g1, bd)


def _top_rows(s, k):
    cur, rows = s, []
    for _ in range(k):
        mx = jnp.max(cur, axis=0, keepdims=True)
        rows.append(mx)
        cur = jnp.where(cur == mx, NEG, cur)
    return rows


def _peer_select(s1, s2):
    rows8 = lax.broadcasted_iota(jnp.int32, (8, 1), 0)
    a1, a2 = _top_rows(s1, PEER_TOPK), _top_rows(s2, PEER_TOPK)
    a1_16 = jnp.concatenate(a1, axis=0)
    a1_8 = a1_16[:8]
    cands = [a1_16 + a2[0], a1_8 + a2[1]]
    for k2 in range(2, 8):
        cands.append(jnp.where(rows8 < PEER_TOPK // (k2 + 1), a1_8 + a2[k2], NEG))
    cands.append(a1[0] + jnp.concatenate(a2[8:], axis=0))
    cand = jnp.concatenate(cands, axis=0)
    top = _top_rows(cand, PEER_TOPK + 1)
    tau = 0.5 * (top[PEER_TOPK - 1] + top[PEER_TOPK])
    z = jnp.sum(jnp.where(cand >= tau, jnp.exp(cand - top[0]), 0.0), axis=0, keepdims=True)
    return tau - s1, jnp.exp(s1 - a1[0]), jnp.exp(s2 - a2[0]) * (0.5 / z)


def _peer_kernel(x_ref, ng_ref, sc_ref, sh_ref, g2_ref, wq_ref, keys_ref, u_ref, vt_ref, o_ref,
                 h2t_ref, acc_ref, t1_ref, e1_ref, s2_ref, e2_ref, ht0_ref, ht1_ref, z0_ref, z1_ref, *,
                 ec, n_chunks, tm):
    s = pl.program_id(1)
    n_i1 = ec // PEER_N_KEYS
    n_tc = tm // LANE

    @pl.when(s == 0)
    def _():
        x = x_ref[...]
        h2 = _rms(x, 1.0 / D_MODEL) * ng_ref[...] * (1.0 + sc_ref[0]) + sh_ref[0]
        h2t_ref[...] = h2.T.astype(BF16)
        q = _dot(h2.astype(BF16), wq_ref[...])
        qt = q.T.astype(BF16)
        for h in range(PEER_HEADS):
            r0 = h * 2 * PEER_KEY_DIM
            s1 = _dot(keys_ref[h, 0], qt[r0:r0 + PEER_KEY_DIM, :])
            s2 = _dot(keys_ref[h, 1], qt[r0 + PEER_KEY_DIM:r0 + 2 * PEER_KEY_DIM, :])
            for tc in range(n_tc):
                cols = slice(tc * LANE, (tc + 1) * LANE)
                t1, e1, e2 = _peer_select(s1[:, cols], s2[:, cols])
                t1_ref[h, :, cols] = t1
                e1_ref[h, :, cols] = e1
                s2_ref[h, tc] = s2[:, cols]
                e2_ref[h, tc] = e2
        for r in (ht0_ref, ht1_ref, z0_ref, z1_ref, acc_ref):
            r[...] = jnp.zeros_like(r)

    cb = jnp.clip(s - 1, 0, n_chunks - 1)

    def stages(ht_w, ht_r, z_w, z_r):
        orows = D_MODEL // n_i1
        half = n_tc // 2

        def up_piece(j, nh):
            rows = slice(j * PEER_N_KEYS, (j + 1) * PEER_N_KEYS)
            hj = _dot(u_ref[rows, :], h2t_ref[:, nh * half * LANE:(nh + 1) * half * LANE])
            for k in range(half):
                ht_w[nh * half + k, rows, :] = hj[:, k * LANE:(k + 1) * LANE]

        def down_piece(j, nh):
            osl = slice(j * orows, (j + 1) * orows)
            zh = jnp.concatenate([z_r[nh * half + k] for k in range(half)], axis=1)
            acc_ref[osl, nh * half * LANE:(nh + 1) * half * LANE] += _dot(vt_ref[osl, :], zh)

        pieces = [(fn, j, nh) for j in range(n_i1) for nh in (0, 1) for fn in (up_piece, down_piece)]
        th_rows = [[t1_ref[h, pl.ds(cb * n_i1 + j, 1), :] for h in range(PEER_HEADS)] for j in range(n_i1)]
        e1_rows = [[e1_ref[h, pl.ds(cb * n_i1 + j, 1), :] for h in range(PEER_HEADS)] for j in range(n_i1)]
        n_rh = 4
        rh_rows = PEER_N_KEYS // n_rh
        n_units = n_tc * n_rh
        for unit in range(n_units):
            for fn, j, nh in pieces[unit * len(pieces) // n_units:(unit + 1) * len(pieces) // n_units]:
                fn(j, nh)
            tc, rh = divmod(unit, n_rh)
            cols = slice(tc * LANE, (tc + 1) * LANE)
            r2 = slice(rh * rh_rows, (rh + 1) * rh_rows)
            ws = [None] * n_i1
            for h in range(PEER_HEADS):
                s2t, e2t = s2_ref[h, tc, r2, :], e2_ref[h, tc, r2, :]
                for j in range(n_i1):
                    term = jnp.where(s2t >= th_rows[j][h][:, cols], e2t, 0.0) * e1_rows[j][h][:, cols]
                    ws[j] = term if ws[j] is None else ws[j] + term
            for j in range(n_i1):
                rows = slice(j * PEER_N_KEYS + rh * rh_rows, j * PEER_N_KEYS + (rh + 1) * rh_rows)
                xh = ht_r[tc, rows, :]
                t = jnp.tanh(xh * (0.7978845608028654 + 0.035677408136300125 * (xh * xh)))
                z_w[tc, rows, :] = (ws[j] * (xh + xh * t)).astype(BF16)

    @pl.when(s % 2 == 0)
    def _():
        stages(ht0_ref, ht1_ref, z1_ref, z0_ref)

    @pl.when(s % 2 == 1)
    def _():
        stages(ht1_ref, ht0_ref, z0_ref, z1_ref)

    @pl.when(s == n_chunks + 1)
    def _():
        o_ref[...] = x_ref[...] + g2_ref[0] * acc_ref[...].T


def _peer(x, ng, sc, sh, g2, wq, keys, u, vt, rows_per_mod, tm=512):
    n = x.shape[0]
    tm = min(tm, n)
    n_chunks, _, ec = vt.shape
    last = n_chunks - 1
    modspec = pl.BlockSpec((1, 1, D_MODEL), lambda i, s: ((i * tm) // rows_per_mod, 0, 0))
    kern = functools.partial(_peer_kernel, ec=ec, n_chunks=n_chunks, tm=tm)
    head_buf = pltpu.VMEM((PEER_HEADS, PEER_N_KEYS, tm), F32)
    tile_buf = pltpu.VMEM((PEER_HEADS, tm // LANE, PEER_N_KEYS, LANE), F32)
    return pl.pallas_call(
        kern,
        grid=(n // tm, n_chunks + 2),
        in_specs=[pl.BlockSpec((tm, D_MODEL), lambda i, s: (i, 0)),
                  pl.BlockSpec((1, D_MODEL), lambda i, s: (0, 0)),
                  modspec, modspec, modspec,
                  pl.BlockSpec(wq.shape, lambda i, s: (0, 0)),
                  pl.BlockSpec(keys.shape, lambda i, s: (0, 0, 0, 0)),
                  pl.BlockSpec((ec, D_MODEL), lambda i, s: (jnp.minimum(s, last), 0)),
                  pl.BlockSpec((None, D_MODEL, ec), lambda i, s: (jnp.clip(s - 2, 0, last), 0, 0))],
        out_specs=pl.BlockSpec((tm, D_MODEL), lambda i, s: (i, 0)),
        out_shape=jax.ShapeDtypeStruct((n, D_MODEL), F32),
        scratch_shapes=[pltpu.VMEM((D_MODEL, tm), BF16), pltpu.VMEM((D_MODEL, tm), F32),
                        head_buf, head_buf, tile_buf, tile_buf,
                        pltpu.VMEM((tm // LANE, ec, LANE), F32), pltpu.VMEM((tm // LANE, ec, LANE), F32),
                        pltpu.VMEM((tm // LANE, ec, LANE), BF16), pltpu.VMEM((tm // LANE, ec, LANE), BF16)],
        compiler_params=_cp(("parallel", "arbitrary"), 56),
        name="peer_ffn",
    )(x, ng.reshape(1, D_MODEL), sc, sh, g2, wq, keys, u, vt)


def _final_norm_kernel(x_ref, g_ref, o_ref):
    o_ref[...] = _rms(x_ref[...], 1.0 / D_MODEL) * g_ref[...]


def _final_norm(x, g, tm=512):
    n = x.shape[0]
    tm = min(tm, n)
    return pl.pallas_call(
        _final_norm_kernel, grid=(n // tm,),
        in_specs=[pl.BlockSpec((tm, D_MODEL), lambda i: (i, 0)), pl.BlockSpec((1, D_MODEL), lambda i: (0, 0))],
        out_specs=pl.BlockSpec((tm, D_MODEL), lambda i: (i, 0)),
        out_shape=jax.ShapeDtypeStruct((n, D_MODEL), F32),
        compiler_params=_cp(("parallel",)), name="final_norm")(x, g.reshape(1, D_MODEL))


def _rope_tables(t, n_heads, dh, lead=0, width=None):
    width = width or dh
    pos = jnp.arange(t)
    sec, half = dh // 2, dh // 4
    freqs = ROPE_BASE ** (-jnp.arange(half, dtype=F32) / half)
    first = jnp.asarray((np.arange(dh) % sec) < half)
    ang_r = (pos // GRID_W).astype(F32)[:, None] * freqs[None, :]
    ang_c = (pos % GRID_W).astype(F32)[:, None] * freqs[None, :]
    ang = jnp.concatenate([ang_r, ang_r, ang_c, ang_c], axis=1)
    cos, sin = jnp.cos(ang), jnp.sin(ang)
    sa = jnp.where(first[None, :], -sin, 0.0)
    sb = jnp.where(first[None, :], 0.0, sin)
    padw = ((0, 0), (lead, width - dh - lead))
    place = lambda a, fill: jnp.tile(jnp.pad(a, padw, constant_values=fill), (1, n_heads))
    return place(cos, 1.0), place(sa, 0.0), place(sb, 0.0)


def _rope(x, cos, sa, sb, half):
    w = x.shape[-1]
    return x * cos + pltpu.roll(x, w - half, 1) * sa + pltpu.roll(x, half, 1) * sb


def _softmax_pv(blocks, sink=None):
    m = None
    for s, _ in blocks:
        bm = jnp.max(s, axis=1, keepdims=True)
        m = bm if m is None else jnp.maximum(m, bm)
    if sink is not None:
        m = jnp.maximum(m, sink)
    l, acc = None, None
    for s, v in blocks:
        p = jnp.exp(s - m)
        bl = jnp.sum(p, axis=1, keepdims=True)
        pv = _dot(p.astype(BF16), v)
        l = bl if l is None else l + bl
        acc = pv if acc is None else acc + pv
    if sink is not None:
        l = l + jnp.exp(sink - m)
    return acc / l


def _ctx_attn_kernel(naq, nak, nav, swq, swk, swv, qm, kn, kr, vm, sink_ref, o_na, o_sw, o_mla):
    hd = HEAD_DIM
    scale = hd ** -0.5
    q, k, v = naq[...].astype(BF16), nak[...].astype(BF16), nav[...].astype(BF16)
    for h in range(NA_HEADS):
        hs = slice(h * hd, (h + 1) * hd)
        o_na[:, hs] = _softmax_pv([(_dot_nt(q[:, hs], k[:, hs]) * scale, v[:, hs])])
    q, k, v = swq[...].astype(BF16), swk[...].astype(BF16), swv[...].astype(BF16)
    for h in range(SW_HEADS):
        g = h // SW_GROUP
        hs, gs = slice(h * hd, (h + 1) * hd), slice(g * hd, (g + 1) * hd)
        o_sw[:, hs] = _softmax_pv([(_dot_nt(q[:, hs], k[:, gs]) * scale, v[:, gs])], sink=sink_ref[h])
    q, v = qm[...].astype(BF16), vm[...].astype(BF16)
    krv = kr[...]
    for h in range(MLA_HEADS):
        ks = slice(h * LANE, (h + 1) * LANE)
        kcat = (kn[:, ks] + krv).astype(BF16)
        s = _dot_nt(q[:, ks], kcat) * MLA_SCALE
        o_mla[:, h * MLA_V:(h + 1) * MLA_V] = _softmax_pv([(s, v[:, h * MLA_V:(h + 1) * MLA_V])])


def _ctx_attention(p, qm, kn, vm, sink, nb, t):
    row = lambda wd: pl.BlockSpec((t, wd), lambda b: (b, 0))
    outs = pl.pallas_call(
        _ctx_attn_kernel, grid=(nb,),
        in_specs=[row(256), row(256), row(256), row(256), row(128), row(128), row(MLA_QK_W), row(MLA_QK_W),
                  row(LANE), row(256), pl.BlockSpec(memory_space=pltpu.SMEM)],
        out_specs=[row(256)] * 3,
        out_shape=[jax.ShapeDtypeStruct((nb * t, BRANCH_W), F32)] * 3,
        compiler_params=_cp(("parallel",)), name="ctx_attention",
    )(p["na_q"], p["na_k"], p["na_v"], p["sw_q"], p["sw_k"], p["sw_v"], qm, kn, p["krope"], vm, sink)
    return outs


NA_TQ = 256


def _na_bias_tables(rpb, rows_n):
    nr, nc = 2 * NA_KR_MAX - 1, 2 * NA_KC - 1
    p_sel = np.zeros((3, 4, 12, nr), np.float32)
    ok_r = np.zeros((3, 4, 12), bool)
    for ti, j in enumerate((0, 1, rows_n // 4 - 1)):
        for a in range(4):
            r = 4 * j + a
            r0 = min(max(r - NA_KR_MAX // 2, 0), rows_n - NA_KR_MAX)
            for e in range(12):
                krow = 4 * (j - 1) + e
                if r0 <= krow < r0 + NA_KR_MAX:
                    ok_r[ti, a, e] = True
                    p_sel[ti, a, e, krow - r + NA_KR_MAX - 1] = 1.0
    c = np.arange(GRID_W)[:, None]
    w = np.arange(GRID_W)[None, :]
    cs = np.clip(c - NA_KC // 2, 0, GRID_W - NA_KC)
    ok_c = (w >= cs) & (w < cs + NA_KC)
    q_sel = np.eye(nc, dtype=np.float32)[np.clip(w - c + NA_KC - 1, 0, nc - 1)]
    hp = lax.Precision.HIGHEST
    m1 = jnp.einsum("taei,lhij->lthaej", jnp.asarray(p_sel), rpb, precision=hp)
    b = jnp.einsum("lthaej,cwj->lthacew", m1, jnp.asarray(q_sel), precision=hp)
    ok = ok_r[:, :, None, :, None] & ok_c[None, None, :, None, :]
    b = jnp.where(jnp.asarray(ok)[None, :, None], b, NEG)
    return b.reshape(rpb.shape[0], 3, NA_HEADS, 4 * GRID_W, 12 * GRID_W)


def _na_kernel(q, k0, k1, k2, v0, v1, v2, kc, vc, bias, o):
    hd = HEAD_DIM
    scale = hd ** -0.5
    qb = q[...].astype(BF16)
    kk = jnp.concatenate([k0[...], k1[...], k2[...]], axis=0).astype(BF16)
    vv = jnp.concatenate([v0[...], v1[...], v2[...]], axis=0).astype(BF16)
    kcb, vcb = kc[...].astype(BF16), vc[...].astype(BF16)
    for h in range(NA_HEADS):
        hs = slice(h * hd, (h + 1) * hd)
        s_nb = _dot_nt(qb[:, hs], kk[:, hs]) * scale + bias[h]
        s_cx = _dot_nt(qb[:, hs], kcb[:, hs]) * scale
        o[:, hs] = _softmax_pv([(s_nb, vv[:, hs]), (s_cx, vcb[:, hs])])


def _na_attention(q, k, v, kc, vc, bias, nb, t):
    nj = t // NA_TQ
    lc = kc.shape[1]
    blk = lambda f: pl.BlockSpec((NA_TQ, BRANCH_W), lambda b, j: (b * nj + f(j), 0))
    prev = lambda j: jnp.maximum(j - 1, 0)
    nxt = lambda j: jnp.minimum(j + 1, nj - 1)
    cur = lambda j: j
    ctx = pl.BlockSpec((None, lc, BRANCH_W), lambda b, j: (b, 0, 0))
    tsel = lambda b, j: (jnp.where(j == 0, 0, jnp.where(j == nj - 1, 2, 1)), 0, 0, 0)
    return pl.pallas_call(
        _na_kernel, grid=(nb, nj),
        in_specs=[blk(cur), blk(prev), blk(cur), blk(nxt), blk(prev), blk(cur), blk(nxt), ctx, ctx,
                  pl.BlockSpec((None,) + bias.shape[1:], tsel)],
        out_specs=blk(cur),
        out_shape=jax.ShapeDtypeStruct((nb * t, BRANCH_W), F32),
        compiler_params=_cp(("parallel", "arbitrary"), 48), name="na_attention",
    )(q, k, k, k, v, v, v, kc, vc, bias)


SW_TQ = 256


def _sw_kernel(q, k0, k1, k2, v0, v1, v2, kc, vc, cq, saq, sbq, ck0, sak0, sbk0, ck1, sak1, sbk1,
               ck2, sak2, sbk2, sink_ref, o, *, t):
    hd = HEAD_DIM
    scale = hd ** -0.5
    half = hd // 4
    j = pl.program_id(1)
    qr = _rope(q[...], cq[...], saq[...], sbq[...], half).astype(BF16)
    kk = jnp.concatenate([_rope(k0[...], ck0[...], sak0[...], sbk0[...], half),
                          _rope(k1[...], ck1[...], sak1[...], sbk1[...], half),
                          _rope(k2[...], ck2[...], sak2[...], sbk2[...], half)], axis=0).astype(BF16)
    vv = jnp.concatenate([v0[...], v1[...], v2[...]], axis=0).astype(BF16)
    kcb, vcb = kc[...].astype(BF16), vc[...].astype(BF16)
    qpos = j * SW_TQ + lax.broadcasted_iota(jnp.int32, (SW_TQ, 3 * SW_TQ), 0)
    kpos = (j - 1) * SW_TQ + lax.broadcasted_iota(jnp.int32, (SW_TQ, 3 * SW_TQ), 1)
    valid = (jnp.abs(qpos - kpos) <= SW_WINDOW) & (kpos >= 0) & (kpos < t)
    for h in range(SW_HEADS):
        g = h // SW_GROUP
        hs, gs = slice(h * hd, (h + 1) * hd), slice(g * hd, (g + 1) * hd)
        s_loc = jnp.where(valid, _dot_nt(qr[:, hs], kk[:, gs]) * scale, NEG)
        s_cx = _dot_nt(qr[:, hs], kcb[:, gs]) * scale
        o[:, hs] = _softmax_pv([(s_loc, vv[:, gs]), (s_cx, vcb[:, gs])], sink=sink_ref[h])


def _sw_attention(q, k, v, kc, vc, sink, tabs_q, tabs_k, nb, t):
    nj = t // SW_TQ
    lc = kc.shape[1]
    kvw = SW_KV_HEADS * HEAD_DIM
    prev = lambda j: jnp.maximum(j - 1, 0)
    nxt = lambda j: jnp.minimum(j + 1, nj - 1)
    cur = lambda j: j
    qblk = pl.BlockSpec((SW_TQ, BRANCH_W), lambda b, j: (b * nj + j, 0))
    kblk = lambda f: pl.BlockSpec((SW_TQ, kvw), lambda b, j: (b * nj + f(j), 0))
    ctx = pl.BlockSpec((None, lc, kvw), lambda b, j: (b, 0, 0))
    tq = pl.BlockSpec((SW_TQ, BRANCH_W), lambda b, j: (j, 0))
    tk = lambda f: pl.BlockSpec((SW_TQ, kvw), lambda b, j: (f(j), 0))
    in_specs = [qblk, kblk(prev), kblk(cur), kblk(nxt), kblk(prev), kblk(cur), kblk(nxt), ctx, ctx, tq, tq, tq]
    args = [q, k, k, k, v, v, v, kc, vc, *tabs_q]
    for f in (prev, cur, nxt):
        in_specs += [tk(f)] * 3
        args += list(tabs_k)
    in_specs.append(pl.BlockSpec(memory_space=pltpu.SMEM))
    args.append(sink)
    return pl.pallas_call(
        functools.partial(_sw_kernel, t=t), grid=(nb, nj), in_specs=in_specs, out_specs=qblk,
        out_shape=jax.ShapeDtypeStruct((nb * t, BRANCH_W), F32),
        compiler_params=_cp(("parallel", "arbitrary"), 48), name="sw_attention",
    )(*args)


MLA_TQ = 512
MLA_TK = 512


def _mla_kernel(qm, kn, kr, vt, knc, krc, vtc, cq, saq, sbq, ck, sak, sbk, o, qt_ref, m_ref, l_ref, acc_ref, *,
                n_lat):
    kj = pl.program_id(2)
    half = MLA_ROPE // 4

    @pl.when(kj == 0)
    def _():
        qt_ref[...] = _rope(qm[...], cq[...], saq[...], sbq[...], half).T.astype(BF16)
        m_ref[...] = jnp.full(m_ref.shape, NEG, F32)
        l_ref[...] = jnp.zeros(l_ref.shape, F32)
        acc_ref[...] = jnp.zeros(acc_ref.shape, F32)

    def step(kcat, vt_b):
        for h in range(MLA_HEADS):
            ks = slice(h * LANE, (h + 1) * LANE)
            vs = slice(h * MLA_V, (h + 1) * MLA_V)
            s = _dot(kcat[:, ks], qt_ref[ks, :]) * MLA_SCALE
            m_old = m_ref[h]
            m_new = jnp.maximum(m_old, jnp.max(s, axis=0, keepdims=True))
            alpha = jnp.exp(m_old - m_new)
            p = jnp.exp(s - m_new)
            l_ref[h] = alpha * l_ref[h] + jnp.sum(p, axis=0, keepdims=True)
            acc_ref[vs, :] = alpha * acc_ref[vs, :] + _dot(vt_b[vs, :], p.astype(BF16))
            m_ref[h] = m_new

    @pl.when(kj < n_lat)
    def _():
        krr = _rope(kr[...], ck[...], sak[...], sbk[...], half)
        step((kn[...] + jnp.concatenate([krr] * MLA_HEADS, axis=1)).astype(BF16), vt[...].astype(BF16))

    @pl.when(kj == n_lat)
    def _():
        step((knc[...] + jnp.concatenate([krc[...]] * MLA_HEADS, axis=1)).astype(BF16), vtc[...].astype(BF16))
        for h in range(MLA_HEADS):
            vs = slice(h * MLA_V, (h + 1) * MLA_V)
            acc_ref[vs, :] = acc_ref[vs, :] / l_ref[h]
        o[...] = acc_ref[...].T


def _mla_attention(qm, kn, kr, vt, knc, krc, vtc, tabs_q, tabs_k, nb, t):
    nq, n_lat = t // MLA_TQ, t // MLA_TK
    lc = krc.shape[1]
    assert lc == MLA_TK
    kb = lambda b, i, kj: b * n_lat + jnp.minimum(kj, n_lat - 1)
    kblk = lambda wd: pl.BlockSpec((MLA_TK, wd), lambda b, i, kj: (kb(b, i, kj), 0))
    tkb = pl.BlockSpec((MLA_TK, LANE), lambda b, i, kj: (jnp.minimum(kj, n_lat - 1), 0))
    tqb = pl.BlockSpec((MLA_TQ, MLA_QK_W), lambda b, i, kj: (i, 0))
    return pl.pallas_call(
        functools.partial(_mla_kernel, n_lat=n_lat), grid=(nb, nq, n_lat + 1),
        in_specs=[pl.BlockSpec((MLA_TQ, MLA_QK_W), lambda b, i, kj: (b * nq + i, 0)),
                  kblk(MLA_QK_W), kblk(LANE),
                  pl.BlockSpec((BRANCH_W, MLA_TK), lambda b, i, kj: (0, kb(b, i, kj))),
                  pl.BlockSpec((lc, MLA_QK_W), lambda b, i, kj: (b, 0)),
                  pl.BlockSpec((None, lc, LANE), lambda b, i, kj: (b, 0, 0)),
                  pl.BlockSpec((BRANCH_W, lc), lambda b, i, kj: (0, b)),
                  tqb, tqb, tqb, tkb, tkb, tkb],
        out_specs=pl.BlockSpec((MLA_TQ, BRANCH_W), lambda b, i, kj: (b * nq + i, 0)),
        out_shape=jax.ShapeDtypeStruct((nb * t, BRANCH_W), F32),
        scratch_shapes=[pltpu.VMEM((MLA_QK_W, MLA_TQ), BF16), pltpu.VMEM((MLA_HEADS, 1, MLA_TQ), F32),
                        pltpu.VMEM((MLA_HEADS, 1, MLA_TQ), F32), pltpu.VMEM((BRANCH_W, MLA_TQ), F32)],
        compiler_params=_cp(("parallel", "parallel", "arbitrary"), 48), name="mla_attention",
    )(qm, kn, kr, vt, knc, krc, vtc, *tabs_q, *tabs_k)


def _log_sigmoid(x):
    return jnp.minimum(x, 0.0) - jnp.log(1.0 + jnp.exp(-jnp.abs(x)))


def _split3(x):
    hi = x.astype(BF16)
    r1 = x - hi.astype(F32)
    mid = r1.astype(BF16)
    lo = (r1 - mid.astype(F32)).astype(BF16)
    return jnp.concatenate([hi, mid, lo], axis=1)


def _mlstm_dir(d, q_ref, k_ref, v_ref, if_ref, o_ref, c_aug, m_prev, negsel_ref):
    ln = ML_CHUNK
    scale = HEAD_DIM ** -0.5
    t_in = if_ref[...]
    logf = _log_sigmoid(t_in)
    ri = lax.broadcasted_iota(jnp.int32, (ln, ln), 0)
    ci = lax.broadcasted_iota(jnp.int32, (ln, ln), 1)
    mask = (ci <= ri) if d == 0 else (ci >= ri)
    tri = jnp.where(mask, 1.0, 0.0).astype(BF16)
    fs = _split3(logf)
    bcol = _dot(tri, fs[:, :LANE]) + _dot(tri, fs[:, LANE:2 * LANE]) + _dot(tri, fs[:, 2 * LANE:])
    b = pltpu.roll(bcol, LANE - 2 * ML_HEADS, 1)
    a = t_in - b
    row = lax.broadcasted_iota(jnp.int32, (ln, LANE), 0)
    cm, sh = a, 1
    while sh < ln:
        if d == 0:
            cm = jnp.maximum(cm, jnp.where(row >= sh, pltpu.roll(cm, sh, 0), NEG))
        else:
            cm = jnp.maximum(cm, jnp.where(row < ln - sh, pltpu.roll(cm, ln - sh, 0), NEG))
        sh *= 2
    big_m = jnp.maximum(m_prev, cm)
    xs = _split3(big_m - m_prev)
    ys = _split3(b + big_m)
    at = (a - m_prev).T
    last = ln - 1 if d == 0 else 0
    m_last = big_m[last:last + 1, :]
    m_new = b[last:last + 1, :] + m_last
    decay = jnp.exp(m_prev - m_last)
    wt = jnp.exp(a - m_last).T
    q, k, v = q_ref[...], k_ref[...], v_ref[...]
    new_c = []
    for h in range(ML_HEADS):
        c = d * ML_HEADS + h
        hsl = slice(h * LANE, (h + 1) * LANE)
        neg_x = _dot(xs, negsel_ref[c])
        neg_y = _dot(ys, negsel_ref[c])
        w_intra = jnp.exp(jnp.where(mask, at[c:c + 1, :] + neg_x, NEG))
        kh = k[:, hsl] * scale
        qb, kb, vb = q[:, hsl].astype(BF16), kh.astype(BF16), v[:, hsl].astype(BF16)
        s = _dot_nt(qb, kb) * w_intra
        c_prev = c_aug[c]
        nd = _dot(s.astype(BF16), vb) + jnp.exp(neg_x) * _dot(qb, c_prev.astype(BF16))
        den = pltpu.roll(nd, HEAD_DIM, 1)
        hout = nd / jnp.maximum(jnp.abs(den), jnp.exp(neg_y))
        o_ref[:, h * HEAD_DIM:(h + 1) * HEAD_DIM] = hout[:, :HEAD_DIM]
        new_c.append(decay[:, c:c + 1] * c_prev + _dot((kh.T * wt[c:c + 1, :]).astype(BF16), vb))
    return new_c, m_new


def _mlstm_kernel(qf, kf, vf, iff, qb, kb, vb, ifb, c0, m0, negsel, hf, hb, c_out, m_out, c_s, m_s, *, nc):
    c = pl.program_id(1)
    nst = 2 * ML_HEADS

    @pl.when(c == 0)
    def _():
        c_s[...] = c0[...]
        m_s[...] = m0[...]

    c_aug = [c_s[i] for i in range(nst)]
    m_prev = m_s[...]
    cf, mf = _mlstm_dir(0, qf, kf, vf, iff, hf, c_aug, m_prev, negsel)
    cb, mb = _mlstm_dir(1, qb, kb, vb, ifb, hb, c_aug, m_prev, negsel)
    for i, cn in enumerate(cf + cb):
        c_s[i] = cn
    lane = lax.broadcasted_iota(jnp.int32, (1, LANE), 1)
    m_s[...] = jnp.where(lane < ML_HEADS, mf, mb)

    @pl.when(c == nc - 1)
    def _():
        c_out[...] = c_s[...]
        m_out[...] = m_s[...]


def _mlstm(q, k, v, gif, c0, n0, m0, nb, t):
    nc = t // ML_CHUNK
    nst = 2 * ML_HEADS
    hd = HEAD_DIM
    qkw = ML_HEADS * LANE
    c_aug0 = jnp.concatenate([c0.reshape(nb, nst, hd, hd),
                              jnp.broadcast_to(n0.reshape(nb, nst, hd, 1), (nb, nst, hd, hd))], axis=-1)
    c_aug0 = jnp.pad(c_aug0, ((0, 0), (0, 0), (0, LANE - hd), (0, 0)))
    m_row0 = jnp.pad(m0.reshape(nb, 1, nst), ((0, 0), (0, 0), (0, LANE - nst)))
    sel = np.zeros((nst, 3 * LANE, LANE), np.float32)
    for ch in range(nst):
        sel[ch, [ch, LANE + ch, 2 * LANE + ch], :] = -1.0
    fw = lambda wd: pl.BlockSpec((ML_CHUNK, wd), lambda b, c: (b * nc + c, 0))
    bw = lambda wd: pl.BlockSpec((ML_CHUNK, wd), lambda b, c: (b * nc + nc - 1 - c, 0))
    st = lambda shp: pl.BlockSpec((None,) + shp, lambda b, c: (b,) + (0,) * len(shp))
    shapes = ((nst, LANE, LANE), (1, LANE))
    hf, hb, c_fin, m_fin = pl.pallas_call(
        functools.partial(_mlstm_kernel, nc=nc), grid=(nb, nc),
        in_specs=[fw(qkw), fw(qkw), fw(qkw), fw(LANE), bw(qkw), bw(qkw), bw(qkw), bw(LANE)]
                 + [st(s) for s in shapes] + [pl.BlockSpec(sel.shape, lambda b, c: (0, 0, 0))],
        out_specs=[fw(BRANCH_W), bw(BRANCH_W)] + [st(s) for s in shapes],
        out_shape=[jax.ShapeDtypeStruct((nb * t, BRANCH_W), F32)] * 2
                  + [jax.ShapeDtypeStruct((nb,) + s, F32) for s in shapes],
        scratch_shapes=[pltpu.VMEM(s, F32) for s in shapes],
        compiler_params=_cp(("parallel", "arbitrary")), name="mlstm_scan",
    )(q, k, v, gif, q, k, v, gif, c_aug0, m_row0, jnp.asarray(sel, BF16))
    c_new = c_fin[:, :, :hd, :hd].reshape(nb, 2, ML_HEADS, hd, hd)
    n_new = c_fin[:, :, :hd, hd].reshape(nb, 2, ML_HEADS, hd)
    m_new = m_fin[:, 0, :nst].reshape(nb, 2, ML_HEADS)
    return hf, hb, c_new, n_new, m_new


def _prep_weights(w_in, b_in, mla_q_norm, w_uq, w_uk, w_uv, w_branch, w_out, peer_wq, peer_keys, peer_u, peer_v):
    offs = np.cumsum((0,) + IN_SPLITS)
    seg = lambda a, i: a[..., offs[i]:offs[i + 1]]
    padr = lambda a, wd, lead=0: jnp.pad(a, [(0, 0)] * (a.ndim - 1) + [(lead, wd - a.shape[-1] - lead)])

    def headpad(a, fill):
        a4 = a.reshape(a.shape[:-1] + (ML_HEADS, HEAD_DIM))
        a4 = jnp.pad(a4, [(0, 0)] * (a4.ndim - 1) + [(0, LANE - HEAD_DIM)], constant_values=fill)
        return a4.reshape(a.shape[:-1] + (ML_HEADS * LANE,))

    def layout(a, is_bias):
        ml_if = padr(jnp.concatenate([seg(a, 7), seg(a, 8)], axis=-1), LANE)
        ml = [headpad(seg(a, 3), 0.0), headpad(seg(a, 4), 0.0), headpad(seg(a, 5), 1.0 if is_bias else 0.0)]
        parts = [seg(a, i) for i in range(3)] + ml + [seg(a, 6), ml_if, seg(a, 9), seg(a, 10), seg(a, 11),
                                                       padr(seg(a, 12), 256), seg(a, 13),
                                                       padr(seg(a, 14), LANE, KROPE_SLICE.start)]
        return jnp.concatenate(parts, axis=-1)

    nl = w_in.shape[0]
    w_uq_p = padr(w_uq.reshape(nl, MLA_Q_RANK, MLA_HEADS, MLA_NOPE + MLA_ROPE), LANE).reshape(nl, MLA_Q_RANK, -1)
    w_uq_p = jnp.pad(w_uq_p, ((0, 0), (0, 256 - MLA_Q_RANK), (0, 0)))
    w_uk_p = padr(w_uk.reshape(nl, MLA_KV_RANK, MLA_HEADS, MLA_NOPE), LANE).reshape(nl, MLA_KV_RANK, -1)
    return dict(
        w_ab=layout(w_in, False).astype(BF16), b_ab=layout(b_in, True),
        w_c=seg(w_in, 15).astype(BF16), b_c=seg(b_in, 15),
        q_norm=padr(mla_q_norm, 256), w_uq=w_uq_p.astype(BF16),
        w_ukv=jnp.concatenate([w_uk_p, w_uv], axis=-1).astype(BF16),
        w_branch=w_branch.astype(BF16), w_out=w_out.astype(BF16), peer_wq=peer_wq.astype(BF16),
        peer_keys=peer_keys.astype(BF16), peer_u=peer_u.astype(BF16),
        peer_vt=jnp.swapaxes(peer_v.reshape(nl, -1, PEER_EC, peer_v.shape[-1]), 2, 3).astype(BF16),
    )


def _project(x, wl, l, norm_g, sc, sh, rows_per_mod):
    outs = _linear(x, wl["w_ab"][l], [wd for _, wd in SEGS_AB], bias=wl["b_ab"][l], norm_g=norm_g,
                   mod=(sc, sh), rows_per_mod=rows_per_mod, tm=512, vmem_mb=52, name="in_proj_ab")
    p = {name: o for (name, _), o in zip(SEGS_AB, outs)}
    (p["gate"],) = _linear(x, wl["w_c"][l], [N_BRANCH * D_MODEL], bias=wl["b_c"][l], norm_g=norm_g,
                           mod=(sc, sh), rows_per_mod=rows_per_mod, tm=512, vmem_mb=52, name="in_proj_gate")
    (qm,) = _linear(p["cq"], wl["w_uq"][l], [MLA_QK_W], norm_g=wl["q_norm"][l], k_valid=MLA_Q_RANK, tm=512,
                    name="mla_q_up")
    return p, qm


def kernel(x_prompt, x_sample, cache_na_k, cache_na_v, state_mlstm_C, state_mlstm_n, state_mlstm_m,
           cache_swa_k, cache_swa_v, cache_mla_ckv, cache_mla_krope, c, c_ctx, w_mod, b_mod, norm1_g, norm2_g,
           w_in, b_in, na_rpb, sw_sink, mla_q_norm, w_uq, mla_kv_norm, w_uk, w_uv, w_branch, w_out, peer_wq,
           peer_keys, peer_u, peer_v, final_norm_g):
    nbp, tp, d = x_prompt.shape
    nbs, ts, _ = x_sample.shape
    nl = w_in.shape[0]
    lc = cache_na_k.shape[2]
    nst = 2 * ML_HEADS
    wl = _prep_weights(w_in, b_in, mla_q_norm, w_uq, w_uk, w_uv, w_branch, w_out, peer_wq, peer_keys,
                       peer_u, peer_v)

    n_cond = 1 + nbs
    cond = jnp.zeros((16, d), F32).at[0].set(c_ctx).at[1:n_cond].set(c)
    mods = _modulation(cond, w_mod, b_mod).reshape(nl, 16, 6, 1, d)

    tabs_swq = _rope_tables(ts, SW_HEADS, HEAD_DIM)
    tabs_swk = _rope_tables(ts, SW_KV_HEADS, HEAD_DIM)
    tabs_mlaq = _rope_tables(ts, MLA_HEADS, MLA_ROPE, lead=MLA_NOPE, width=LANE)
    tabs_mlak = _rope_tables(ts, 1, MLA_ROPE, lead=MLA_NOPE, width=LANE)
    na_bias = _na_bias_tables(na_rpb, ts // GRID_W)
    krope_c = jnp.pad(cache_mla_krope, ((0, 0), (0, 0), (0, 0), (KROPE_SLICE.start, LANE - KROPE_SLICE.stop)))

    xp = x_prompt.reshape(nbp * tp, d)
    xs = x_sample.reshape(nbs * ts, d)
    zeros_c = jnp.zeros((nbp, 2, ML_HEADS, HEAD_DIM, HEAD_DIM), F32)
    zeros_n = jnp.zeros((nbp, 2, ML_HEADS, HEAD_DIM), F32)
    zeros_m = jnp.zeros((nbp, 2, ML_HEADS), F32)
    per_layer = tuple([] for _ in range(9))

    for l in range(nl):
        mod_p = [mods[l, 0:1, i] for i in range(6)]
        mod_s = [mods[l, 1:n_cond, i] for i in range(6)]
        sink = sw_sink[l]

        p, qm = _project(xp, wl, l, norm1_g[l], mod_p[1], mod_p[0], nbp * tp)
        ckv_n, kn, vm = _linear(p["ckv"], wl["w_ukv"][l], [MLA_QK_W, 256], norm_g=mla_kv_norm[l],
                                emit_normed=True, tm=512, name="mla_kv_up")
        o_na, o_sw, o_mla = _ctx_attention(p, qm, kn, vm, sink, nbp, tp)
        hf, hb, c_new, n_new, m_new = _mlstm(p["ml_q"], p["ml_k"], p["ml_v"], p["ml_if"], zeros_c, zeros_n,
                                             zeros_m, nbp, tp)
        xp = _merge(o_na, hf, hb, p["ml_o"], o_sw, o_mla, p["gate"], wl["w_branch"][l], wl["w_out"][l], xp,
                    mod_p[2], nbp * tp)
        xp = _peer(xp, norm2_g[l], mod_p[4], mod_p[3], mod_p[5], wl["peer_wq"][l], wl["peer_keys"][l],
                   wl["peer_u"][l], wl["peer_vt"][l], nbp * tp)
        ctx_t = (p["na_k"].reshape(nbp, tp, NA_HEADS, HEAD_DIM), p["na_v"].reshape(nbp, tp, NA_HEADS, HEAD_DIM),
                 c_new, n_new, m_new,
                 p["sw_k"].reshape(nbp, tp, SW_KV_HEADS, HEAD_DIM), p["sw_v"].reshape(nbp, tp, SW_KV_HEADS, HEAD_DIM),
                 ckv_n.reshape(nbp, tp, MLA_KV_RANK), p["krope"][:, KROPE_SLICE].reshape(nbp, tp, MLA_ROPE))
        for i, a in enumerate(ctx_t):
            per_layer[i].append(a)

        p, qm = _project(xs, wl, l, norm1_g[l], mod_s[1], mod_s[0], ts)
        _, kn, vt = _linear(p["ckv"], wl["w_ukv"][l], [MLA_QK_W, 256], norm_g=mla_kv_norm[l], emit_normed=True,
                            transpose_last=True, tm=512, name="mla_kv_up_t")
        knc, vtc = _linear(cache_mla_ckv[:, l].reshape(nbs * lc, MLA_KV_RANK), wl["w_ukv"][l], [MLA_QK_W, 256],
                           transpose_last=True, tm=512, name="mla_kv_up_cache")
        o_na = _na_attention(p["na_q"], p["na_k"], p["na_v"], cache_na_k[:, l].reshape(nbs, lc, BRANCH_W),
                             cache_na_v[:, l].reshape(nbs, lc, BRANCH_W), na_bias[l], nbs, ts)
        o_sw = _sw_attention(p["sw_q"], p["sw_k"], p["sw_v"],
                             cache_swa_k[:, l].reshape(nbs, lc, SW_KV_HEADS * HEAD_DIM),
                             cache_swa_v[:, l].reshape(nbs, lc, SW_KV_HEADS * HEAD_DIM), sink, tabs_swq, tabs_swk,
                             nbs, ts)
        o_mla = _mla_attention(qm, kn, p["krope"], vt, knc, krope_c[:, l], vtc, tabs_mlaq, tabs_mlak, nbs, ts)
        hf, hb, _, _, _ = _mlstm(p["ml_q"], p["ml_k"], p["ml_v"], p["ml_if"], state_mlstm_C[:, l],
                                 state_mlstm_n[:, l], state_mlstm_m[:, l], nbs, ts)
        xs = _merge(o_na, hf, hb, p["ml_o"], o_sw, o_mla, p["gate"], wl["w_branch"][l], wl["w_out"][l], xs,
                    mod_s[2], ts)
        xs = _peer(xs, norm2_g[l], mod_s[4], mod_s[3], mod_s[5], wl["peer_wq"][l], wl["peer_keys"][l],
                   wl["peer_u"][l], wl["peer_vt"][l], ts)

    y_prompt = _final_norm(xp, final_norm_g).reshape(nbp, tp, d)
    y_sample = _final_norm(xs, final_norm_g).reshape(nbs, ts, d)
    return (y_prompt, y_sample) + tuple(jnp.stack(s, axis=1) for s in per_layer)
```

```python
import functools

import numpy as np
import jax
import jax.numpy as jnp
from jax import lax
from jax.experimental import pallas as pl
from jax.experimental.pallas import tpu as pltpu

F32 = jnp.float32
BF16 = jnp.bfloat16

D_MODEL = 1024
GRID_W = 64
HEAD_DIM = 64
N_BRANCH = 4
BRANCH_W = 256
NA_HEADS = 4
NA_KR_MAX = 8
NA_KC = 16
ML_HEADS = 4
ML_CHUNK = 128
SW_HEADS = 4
SW_KV_HEADS = 2
SW_GROUP = SW_HEADS // SW_KV_HEADS
SW_WINDOW = 128
MLA_HEADS = 4
MLA_Q_RANK = 192
MLA_KV_RANK = 128
MLA_NOPE = 64
MLA_ROPE = 32
MLA_V = 64
MLA_SCALE = (MLA_NOPE + MLA_ROPE) ** -0.5
PEER_HEADS = 8
PEER_KEY_DIM = 64
PEER_N_KEYS = 128
PEER_TOPK = 16
PEER_EC = 512
ROPE_BASE = 10000.0
RMS_EPS = 1e-6
NEG = -1e30
IN_SPLITS = (256, 256, 256, 256, 256, 256, 256, 8, 8, 256, 128, 128, 192, 128, 32, 4096)

LANE = 128
MLA_QK_W = MLA_HEADS * LANE

SEGS_AB = (("na_q", 256), ("na_k", 256), ("na_v", 256), ("ml_q", 512), ("ml_k", 512), ("ml_v", 512),
           ("ml_o", 256), ("ml_if", 128), ("sw_q", 256), ("sw_k", 128), ("sw_v", 128), ("cq", 256),
           ("ckv", 128), ("krope", 128))
KROPE_SLICE = slice(MLA_NOPE, MLA_NOPE + MLA_ROPE)


def _cp(sem, vmem_mb=None):
    kw = dict(dimension_semantics=sem)
    if vmem_mb is not None:
        kw["vmem_limit_bytes"] = vmem_mb * 1024 * 1024
    return pltpu.CompilerParams(**kw)


def _dot(a, b):
    return jnp.dot(a, b, preferred_element_type=F32)


def _dot_nt(a, b):
    return lax.dot_general(a, b, (((1,), (1,)), ((), ())), preferred_element_type=F32)


def _rms(x, inv_n):
    return x * lax.rsqrt(jnp.sum(x * x, axis=-1, keepdims=True) * inv_n + RMS_EPS)


def _sigmoid(x):
    return 1.0 / (1.0 + jnp.exp(-x))


def _mod_kernel(c_ref, w_ref, b_ref, o_ref):
    c = c_ref[...]
    s = c * _sigmoid(c)
    o_ref[...] = _dot(s.astype(BF16), w_ref[...].astype(BF16)) + b_ref[...]


def _modulation(cond, w_mod, b_mod):
    nl, d, n = w_mod.shape
    r = cond.shape[0]
    tn = 1024
    return pl.pallas_call(
        _mod_kernel,
        grid=(nl, n // tn),
        in_specs=[pl.BlockSpec((r, d), lambda l, j: (0, 0)),
                  pl.BlockSpec((None, d, tn), lambda l, j: (l, 0, j)),
                  pl.BlockSpec((None, 1, tn), lambda l, j: (l, 0, j))],
        out_specs=pl.BlockSpec((None, r, tn), lambda l, j: (l, 0, j)),
        out_shape=jax.ShapeDtypeStruct((nl, r, n), F32),
        compiler_params=_cp(("parallel", "parallel")),
        name="adaln_modulation",
    )(cond, w_mod, b_mod.reshape(nl, 1, n))


def _linear_kernel(*refs, segs, has_norm, has_mod, has_bias, emit_normed, transpose_last, inv_k):
    it = iter(refs)
    x_ref = next(it)
    g_ref = next(it) if has_norm else None
    sc_ref = next(it) if has_mod else None
    sh_ref = next(it) if has_mod else None
    w_ref = next(it)
    b_ref = next(it) if has_bias else None
    outs = list(it)
    x = x_ref[...]
    if has_norm:
        x = _rms(x, inv_k) * g_ref[...]
    if has_mod:
        x = x * (1.0 + sc_ref[0]) + sh_ref[0]
    if emit_normed:
        outs[0][...] = x
        outs = outs[1:]
    xb = x.astype(BF16)
    for idx, ((s, wd), o) in enumerate(zip(segs, outs)):
        y = _dot(xb, w_ref[:, s:s + wd])
        if has_bias:
            y = y + b_ref[:, s:s + wd]
        o[...] = y.T if (transpose_last and idx == len(segs) - 1) else y


def _linear(x, w, widths, *, bias=None, norm_g=None, k_valid=None, mod=None, rows_per_mod=None,
            emit_normed=False, transpose_last=False, tm=256, vmem_mb=None, name="linear"):
    n, k = x.shape
    tm = min(tm, n)
    segs, s = [], 0
    for wd in widths:
        segs.append((s, wd))
        s += wd
    assert s == w.shape[1] and n % tm == 0
    args = [x]
    in_specs = [pl.BlockSpec((tm, k), lambda i: (i, 0))]
    if norm_g is not None:
        args.append(norm_g.reshape(1, k))
        in_specs.append(pl.BlockSpec((1, k), lambda i: (0, 0)))
    if mod is not None:
        assert rows_per_mod % tm == 0
        for m in mod:
            args.append(m)
            in_specs.append(pl.BlockSpec((1, 1, k), lambda i: ((i * tm) // rows_per_mod, 0, 0)))
    args.append(w)
    in_specs.append(pl.BlockSpec(w.shape, lambda i: (0, 0)))
    if bias is not None:
        args.append(bias.reshape(1, -1))
        in_specs.append(pl.BlockSpec((1, w.shape[1]), lambda i: (0, 0)))
    out_shapes, out_specs = [], []
    if emit_normed:
        out_shapes.append(jax.ShapeDtypeStruct((n, k), F32))
        out_specs.append(pl.BlockSpec((tm, k), lambda i: (i, 0)))
    for idx, wd in enumerate(widths):
        if transpose_last and idx == len(widths) - 1:
            out_shapes.append(jax.ShapeDtypeStruct((wd, n), F32))
            out_specs.append(pl.BlockSpec((wd, tm), lambda i: (0, i)))
        else:
            out_shapes.append(jax.ShapeDtypeStruct((n, wd), F32))
            out_specs.append(pl.BlockSpec((tm, wd), lambda i: (i, 0)))
    kern = functools.partial(_linear_kernel, segs=tuple(segs), has_norm=norm_g is not None,
                             has_mod=mod is not None, has_bias=bias is not None, emit_normed=emit_normed,
                             transpose_last=transpose_last, inv_k=1.0 / (k_valid or k))
    return pl.pallas_call(kern, grid=(n // tm,), in_specs=in_specs, out_specs=out_specs,
                          out_shape=out_shapes, compiler_params=_cp(("parallel",), vmem_mb), name=name)(*args)


def _merge_kernel(ona, hf, hb, mlo, osw, omla, gate, wb, wo, x, g1, bd, o):
    h = hf[...] + hb[...]
    hsq = h * h
    hi = hsq.astype(BF16)
    lo = (hsq - hi.astype(F32)).astype(BF16)
    ms = (_dot(hi, bd[...]) + _dot(lo, bd[...])) * (1.0 / HEAD_DIM)
    oml = h * lax.rsqrt(ms + RMS_EPS) * _sigmoid(mlo[...])
    branches = (ona[...], oml, osw[...], omla[...])
    acc = None
    for n in range(N_BRANCH):
        y = _sigmoid(gate[:, n * D_MODEL:(n + 1) * D_MODEL]) * _dot(branches[n].astype(BF16), wb[n])
        acc = y if acc is None else acc + y
    o[...] = x[...] + g1[0] * _dot(acc.astype(BF16), wo[...])


def _merge(ona, hf, hb, mlo, osw, omla, gate, wb, wo, x, g1, rows_per_mod, tm=512):
    n = x.shape[0]
    tm = min(tm, n)
    bd = jnp.asarray(np.kron(np.eye(BRANCH_W // HEAD_DIM), np.ones((HEAD_DIM, HEAD_DIM))), BF16)
    row = lambda wd: pl.BlockSpec((tm, wd), lambda i: (i, 0))
    return pl.pallas_call(
        _merge_kernel,
        grid=(n // tm,),
        in_specs=[row(BRANCH_W)] * 6 + [row(N_BRANCH * D_MODEL),
                  pl.BlockSpec(wb.shape, lambda i: (0, 0, 0)),
                  pl.BlockSpec(wo.shape, lambda i: (0, 0)),
                  row(D_MODEL),
                  pl.BlockSpec((1, 1, D_MODEL), lambda i: ((i * tm) // rows_per_mod, 0, 0)),
                  pl.BlockSpec(bd.shape, lambda i: (0, 0))],
        out_specs=row(D_MODEL),
        out_shape=jax.ShapeDtypeStruct((n, D_MODEL), F32),
        compiler_params=_cp(("parallel",), 48),
        name="branch_merge",
    )(ona, hf, hb, mlo, osw, omla, gate, wb, wo, x, g1, bd)


def _top_rows(s, k):
    cur, rows = s, []
    for _ in range(k):
        mx = jnp.max(cur, axis=0, keepdims=True)
        rows.append(mx)
        cur = jnp.where(cur == mx, NEG, cur)
    return rows


def _peer_select(s1, s2):
    rows8 = lax.broadcasted_iota(jnp.int32, (8, 1), 0)
    a1, a2 = _top_rows(s1, PEER_TOPK), _top_rows(s2, PEER_TOPK)
    a1_16 = jnp.concatenate(a1, axis=0)
    a1_8 = a1_16[:8]
    cands = [a1_16 + a2[0], a1_8 + a2[1]]
    for k2 in range(2, 8):
        cands.append(jnp.where(rows8 < PEER_TOPK // (k2 + 1), a1_8 + a2[k2], NEG))
    cands.append(a1[0] + jnp.concatenate(a2[8:], axis=0))
    cand = jnp.concatenate(cands, axis=0)
    top = _top_rows(cand, PEER_TOPK + 1)
    tau = 0.5 * (top[PEER_TOPK - 1] + top[PEER_TOPK])
    z = jnp.sum(jnp.where(cand >= tau, jnp.exp(cand - top[0]), 0.0), axis=0, keepdims=True)
    return tau - s1, jnp.exp(s1 - a1[0]), jnp.exp(s2 - a2[0]) * (0.5 / z)


def _peer_kernel(x_ref, ng_ref, sc_ref, sh_ref, g2_ref, wq_ref, keys_ref, u_ref, vt_ref, o_ref,
                 h2t_ref, acc_ref, t1_ref, e1_ref, s2_ref, e2_ref, ht0_ref, ht1_ref, z0_ref, z1_ref, *,
                 ec, n_chunks, tm):
    s = pl.program_id(1)
    n_i1 = ec // PEER_N_KEYS
    n_tc = tm // LANE

    @pl.when(s == 0)
    def _():
        x = x_ref[...]
        h2 = _rms(x, 1.0 / D_MODEL) * ng_ref[...] * (1.0 + sc_ref[0]) + sh_ref[0]
        h2t_ref[...] = h2.T.astype(BF16)
        q = _dot(h2.astype(BF16), wq_ref[...])
        qt = q.T.astype(BF16)
        for h in range(PEER_HEADS):
            r0 = h * 2 * PEER_KEY_DIM
            s1 = _dot(keys_ref[h, 0], qt[r0:r0 + PEER_KEY_DIM, :])
            s2 = _dot(keys_ref[h, 1], qt[r0 + PEER_KEY_DIM:r0 + 2 * PEER_KEY_DIM, :])
            for tc in range(n_tc):
                cols = slice(tc * LANE, (tc + 1) * LANE)
                t1, e1, e2 = _peer_select(s1[:, cols], s2[:, cols])
                t1_ref[h, :, cols] = t1
                e1_ref[h, :, cols] = e1
                s2_ref[h, tc] = s2[:, cols]
                e2_ref[h, tc] = e2
        for r in (ht0_ref, ht1_ref, z0_ref, z1_ref, acc_ref):
            r[...] = jnp.zeros_like(r)

    cb = jnp.clip(s - 1, 0, n_chunks - 1)

    def stages(ht_w, ht_r, z_w, z_r):
        orows = D_MODEL // n_i1
        half = n_tc // 2

        def up_piece(j, nh):
            rows = slice(j * PEER_N_KEYS, (j + 1) * PEER_N_KEYS)
            hj = _dot(u_ref[rows, :], h2t_ref[:, nh * half * LANE:(nh + 1) * half * LANE])
            for k in range(half):
                ht_w[nh * half + k, rows, :] = hj[:, k * LANE:(k + 1) * LANE]

        def down_piece(j, nh):
            osl = slice(j * orows, (j + 1) * orows)
            zh = jnp.concatenate([z_r[nh * half + k] for k in range(half)], axis=1)
            acc_ref[osl, nh * half * LANE:(nh + 1) * half * LANE] += _dot(vt_ref[osl, :], zh)

        pieces = [(fn, j, nh) for j in range(n_i1) for nh in (0, 1) for fn in (up_piece, down_piece)]
        th_rows = [[t1_ref[h, pl.ds(cb * n_i1 + j, 1), :] for h in range(PEER_HEADS)] for j in range(n_i1)]
        e1_rows = [[e1_ref[h, pl.ds(cb * n_i1 + j, 1), :] for h in range(PEER_HEADS)] for j in range(n_i1)]
        n_rh = 4
        rh_rows = PEER_N_KEYS // n_rh
        n_units = n_tc * n_rh
        for unit in range(n_units):
            for fn, j, nh in pieces[unit * len(pieces) // n_units:(unit + 1) * len(pieces) // n_units]:
                fn(j, nh)
            tc, rh = divmod(unit, n_rh)
            cols = slice(tc * LANE, (tc + 1) * LANE)
            r2 = slice(rh * rh_rows, (rh + 1) * rh_rows)
            ws = [None] * n_i1
            for h in range(PEER_HEADS):
                s2t, e2t = s2_ref[h, tc, r2, :], e2_ref[h, tc, r2, :]
                for j in range(n_i1):
                    term = jnp.where(s2t >= th_rows[j][h][:, cols], e2t, 0.0) * e1_rows[j][h][:, cols]
                    ws[j] = term if ws[j] is None else ws[j] + term
            for j in range(n_i1):
                rows = slice(j * PEER_N_KEYS + rh * rh_rows, j * PEER_N_KEYS + (rh + 1) * rh_rows)
                xh = ht_r[tc, rows, :]
                t = jnp.tanh(xh * (0.7978845608028654 + 0.035677408136300125 * (xh * xh)))
                z_w[tc, rows, :] = (ws[j] * (xh + xh * t)).astype(BF16)

    @pl.when(s % 2 == 0)
    def _():
        stages(ht0_ref, ht1_ref, z1_ref, z0_ref)

    @pl.when(s % 2 == 1)
    def _():
        stages(ht1_ref, ht0_ref, z0_ref, z1_ref)

    @pl.when(s == n_chunks + 1)
    def _():
        o_ref[...] = x_ref[...] + g2_ref[0] * acc_ref[...].T


def _peer(x, ng, sc, sh, g2, wq, keys, u, vt, rows_per_mod, tm=512):
    n = x.shape[0]
    tm = min(tm, n)
    n_chunks, _, ec = vt.shape
    last = n_chunks - 1
    modspec = pl.BlockSpec((1, 1, D_MODEL), lambda i, s: ((i * tm) // rows_per_mod, 0, 0))
    kern = functools.partial(_peer_kernel, ec=ec, n_chunks=n_chunks, tm=tm)
    head_buf = pltpu.VMEM((PEER_HEADS, PEER_N_KEYS, tm), F32)
    tile_buf = pltpu.VMEM((PEER_HEADS, tm // LANE, PEER_N_KEYS, LANE), F32)
    return pl.pallas_call(
        kern,
        grid=(n // tm, n_chunks + 2),
        in_specs=[pl.BlockSpec((tm, D_MODEL), lambda i, s: (i, 0)),
                  pl.BlockSpec((1, D_MODEL), lambda i, s: (0, 0)),
                  modspec, modspec, modspec,
                  pl.BlockSpec(wq.shape, lambda i, s: (0, 0)),
                  pl.BlockSpec(keys.shape, lambda i, s: (0, 0, 0, 0)),
                  pl.BlockSpec((ec, D_MODEL), lambda i, s: (jnp.minimum(s, last), 0)),
                  pl.BlockSpec((None, D_MODEL, ec), lambda i, s: (jnp.clip(s - 2, 0, last), 0, 0))],
        out_specs=pl.BlockSpec((tm, D_MODEL), lambda i, s: (i, 0)),
        out_shape=jax.ShapeDtypeStruct((n, D_MODEL), F32),
        scratch_shapes=[pltpu.VMEM((D_MODEL, tm), BF16), pltpu.VMEM((D_MODEL, tm), F32),
                        head_buf, head_buf, tile_buf, tile_buf,
                        pltpu.VMEM((tm // LANE, ec, LANE), F32), pltpu.VMEM((tm // LANE, ec, LANE), F32),
                        pltpu.VMEM((tm // LANE, ec, LANE), BF16), pltpu.VMEM((tm // LANE, ec, LANE), BF16)],
        compiler_params=_cp(("parallel", "arbitrary"), 56),
        name="peer_ffn",
    )(x, ng.reshape(1, D_MODEL), sc, sh, g2, wq, keys, u, vt)


def _final_norm_kernel(x_ref, g_ref, o_ref):
    o_ref[...] = _rms(x_ref[...], 1.0 / D_MODEL) * g_ref[...]


def _final_norm(x, g, tm=512):
    n = x.shape[0]
    tm = min(tm, n)
    return pl.pallas_call(
        _final_norm_kernel, grid=(n // tm,),
        in_specs=[pl.BlockSpec((tm, D_MODEL), lambda i: (i, 0)), pl.BlockSpec((1, D_MODEL), lambda i: (0, 0))],
        out_specs=pl.BlockSpec((tm, D_MODEL), lambda i: (i, 0)),
        out_shape=jax.ShapeDtypeStruct((n, D_MODEL), F32),
        compiler_params=_cp(("parallel",)), name="final_norm")(x, g.reshape(1, D_MODEL))


def _rope_tables(t, n_heads, dh, lead=0, width=None):
    width = width or dh
    pos = jnp.arange(t)
    sec, half = dh // 2, dh // 4
    freqs = ROPE_BASE ** (-jnp.arange(half, dtype=F32) / half)
    first = jnp.asarray((np.arange(dh) % sec) < half)
    ang_r = (pos // GRID_W).astype(F32)[:, None] * freqs[None, :]
    ang_c = (pos % GRID_W).astype(F32)[:, None] * freqs[None, :]
    ang = jnp.concatenate([ang_r, ang_r, ang_c, ang_c], axis=1)
    cos, sin = jnp.cos(ang), jnp.sin(ang)
    sa = jnp.where(first[None, :], -sin, 0.0)
    sb = jnp.where(first[None, :], 0.0, sin)
    padw = ((0, 0), (lead, width - dh - lead))
    place = lambda a, fill: jnp.tile(jnp.pad(a, padw, constant_values=fill), (1, n_heads))
    return place(cos, 1.0), place(sa, 0.0), place(sb, 0.0)


def _rope(x, cos, sa, sb, half):
    w = x.shape[-1]
    return x * cos + pltpu.roll(x, w - half, 1) * sa + pltpu.roll(x, half, 1) * sb


def _softmax_pv(blocks, sink=None):
    m = None
    for s, _ in blocks:
        bm = jnp.max(s, axis=1, keepdims=True)
        m = bm if m is None else jnp.maximum(m, bm)
    if sink is not None:
        m = jnp.maximum(m, sink)
    l, acc = None, None
    for s, v in blocks:
        p = jnp.exp(s - m)
        bl = jnp.sum(p, axis=1, keepdims=True)
        pv = _dot(p.astype(BF16), v)
        l = bl if l is None else l + bl
        acc = pv if acc is None else acc + pv
    if sink is not None:
        l = l + jnp.exp(sink - m)
    return acc / l


def _ctx_attn_kernel(naq, nak, nav, swq, swk, swv, qm, kn, kr, vm, sink_ref, o_na, o_sw, o_mla):
    hd = HEAD_DIM
    scale = hd ** -0.5
    q, k, v = naq[...].astype(BF16), nak[...].astype(BF16), nav[...].astype(BF16)
    for h in range(NA_HEADS):
        hs = slice(h * hd, (h + 1) * hd)
        o_na[:, hs] = _softmax_pv([(_dot_nt(q[:, hs], k[:, hs]) * scale, v[:, hs])])
    q, k, v = swq[...].astype(BF16), swk[...].astype(BF16), swv[...].astype(BF16)
    for h in range(SW_HEADS):
        g = h // SW_GROUP
        hs, gs = slice(h * hd, (h + 1) * hd), slice(g * hd, (g + 1) * hd)
        o_sw[:, hs] = _softmax_pv([(_dot_nt(q[:, hs], k[:, gs]) * scale, v[:, gs])], sink=sink_ref[h])
    q, v = qm[...].astype(BF16), vm[...].astype(BF16)
    krv = kr[...]
    for h in range(MLA_HEADS):
        ks = slice(h * LANE, (h + 1) * LANE)
        kcat = (kn[:, ks] + krv).astype(BF16)
        s = _dot_nt(q[:, ks], kcat) * MLA_SCALE
        o_mla[:, h * MLA_V:(h + 1) * MLA_V] = _softmax_pv([(s, v[:, h * MLA_V:(h + 1) * MLA_V])])


def _ctx_attention(p, qm, kn, vm, sink, nb, t):
    row = lambda wd: pl.BlockSpec((t, wd), lambda b: (b, 0))
    outs = pl.pallas_call(
        _ctx_attn_kernel, grid=(nb,),
        in_specs=[row(256), row(256), row(256), row(256), row(128), row(128), row(MLA_QK_W), row(MLA_QK_W),
                  row(LANE), row(256), pl.BlockSpec(memory_space=pltpu.SMEM)],
        out_specs=[row(256)] * 3,
        out_shape=[jax.ShapeDtypeStruct((nb * t, BRANCH_W), F32)] * 3,
        compiler_params=_cp(("parallel",)), name="ctx_attention",
    )(p["na_q"], p["na_k"], p["na_v"], p["sw_q"], p["sw_k"], p["sw_v"], qm, kn, p["krope"], vm, sink)
    return outs


NA_TQ = 256


def _na_bias_tables(rpb, rows_n):
    nr, nc = 2 * NA_KR_MAX - 1, 2 * NA_KC - 1
    p_sel = np.zeros((3, 4, 12, nr), np.float32)
    ok_r = np.zeros((3, 4, 12), bool)
    for ti, j in enumerate((0, 1, rows_n // 4 - 1)):
        for a in range(4):
            r = 4 * j + a
            r0 = min(max(r - NA_KR_MAX // 2, 0), rows_n - NA_KR_MAX)
            for e in range(12):
                krow = 4 * (j - 1) + e
                if r0 <= krow < r0 + NA_KR_MAX:
                    ok_r[ti, a, e] = True
                    p_sel[ti, a, e, krow - r + NA_KR_MAX - 1] = 1.0
    c = np.arange(GRID_W)[:, None]
    w = np.arange(GRID_W)[None, :]
    cs = np.clip(c - NA_KC // 2, 0, GRID_W - NA_KC)
    ok_c = (w >= cs) & (w < cs + NA_KC)
    q_sel = np.eye(nc, dtype=np.float32)[np.clip(w - c + NA_KC - 1, 0, nc - 1)]
    hp = lax.Precision.HIGHEST
    m1 = jnp.einsum("taei,lhij->lthaej", jnp.asarray(p_sel), rpb, precision=hp)
    b = jnp.einsum("lthaej,cwj->lthacew", m1, jnp.asarray(q_sel), precision=hp)
    ok = ok_r[:, :, None, :, None] & ok_c[None, None, :, None, :]
    b = jnp.where(jnp.asarray(ok)[None, :, None], b, NEG)
    return b.reshape(rpb.shape[0], 3, NA_HEADS, 4 * GRID_W, 12 * GRID_W)


def _na_kernel(q, k0, k1, k2, v0, v1, v2, kc, vc, bias, o):
    hd = HEAD_DIM
    scale = hd ** -0.5
    qb = q[...].astype(BF16)
    kk = jnp.concatenate([k0[...], k1[...], k2[...]], axis=0).astype(BF16)
    vv = jnp.concatenate([v0[...], v1[...], v2[...]], axis=0).astype(BF16)
    kcb, vcb = kc[...].astype(BF16), vc[...].astype(BF16)
    for h in range(NA_HEADS):
        hs = slice(h * hd, (h + 1) * hd)
        s_nb = _dot_nt(qb[:, hs], kk[:, hs]) * scale + bias[h]
        s_cx = _dot_nt(qb[:, hs], kcb[:, hs]) * scale
        o[:, hs] = _softmax_pv([(s_nb, vv[:, hs]), (s_cx, vcb[:, hs])])


def _na_attention(q, k, v, kc, vc, bias, nb, t):
    nj = t // NA_TQ
    lc = kc.shape[1]
    blk = lambda f: pl.BlockSpec((NA_TQ, BRANCH_W), lambda b, j: (b * nj + f(j), 0))
    prev = lambda j: jnp.maximum(j - 1, 0)
    nxt = lambda j: jnp.minimum(j + 1, nj - 1)
    cur = lambda j: j
    ctx = pl.BlockSpec((None, lc, BRANCH_W), lambda b, j: (b, 0, 0))
    tsel = lambda b, j: (jnp.where(j == 0, 0, jnp.where(j == nj - 1, 2, 1)), 0, 0, 0)
    return pl.pallas_call(
        _na_kernel, grid=(nb, nj),
        in_specs=[blk(cur), blk(prev), blk(cur), blk(nxt), blk(prev), blk(cur), blk(nxt), ctx, ctx,
                  pl.BlockSpec((None,) + bias.shape[1:], tsel)],
        out_specs=blk(cur),
        out_shape=jax.ShapeDtypeStruct((nb * t, BRANCH_W), F32),
        compiler_params=_cp(("parallel", "arbitrary"), 48), name="na_attention",
    )(q, k, k, k, v, v, v, kc, vc, bias)


SW_TQ = 256


def _sw_kernel(q, k0, k1, k2, v0, v1, v2, kc, vc, cq, saq, sbq, ck0, sak0, sbk0, ck1, sak1, sbk1,
               ck2, sak2, sbk2, sink_ref, o, *, t):
    hd = HEAD_DIM
    scale = hd ** -0.5
    half = hd // 4
    j = pl.program_id(1)
    qr = _rope(q[...], cq[...], saq[...], sbq[...], half).astype(BF16)
    kk = jnp.concatenate([_rope(k0[...], ck0[...], sak0[...], sbk0[...], half),
                          _rope(k1[...], ck1[...], sak1[...], sbk1[...], half),
                          _rope(k2[...], ck2[...], sak2[...], sbk2[...], half)], axis=0).astype(BF16)
    vv = jnp.concatenate([v0[...], v1[...], v2[...]], axis=0).astype(BF16)
    kcb, vcb = kc[...].astype(BF16), vc[...].astype(BF16)
    qpos = j * SW_TQ + lax.broadcasted_iota(jnp.int32, (SW_TQ, 3 * SW_TQ), 0)
    kpos = (j - 1) * SW_TQ + lax.broadcasted_iota(jnp.int32, (SW_TQ, 3 * SW_TQ), 1)
    valid = (jnp.abs(qpos - kpos) <= SW_WINDOW) & (kpos >= 0) & (kpos < t)
    for h in range(SW_HEADS):
        g = h // SW_GROUP
        hs, gs = slice(h * hd, (h + 1) * hd), slice(g * hd, (g + 1) * hd)
        s_loc = jnp.where(valid, _dot_nt(qr[:, hs], kk[:, gs]) * scale, NEG)
        s_cx = _dot_nt(qr[:, hs], kcb[:, gs]) * scale
        o[:, hs] = _softmax_pv([(s_loc, vv[:, gs]), (s_cx, vcb[:, gs])], sink=sink_ref[h])


def _sw_attention(q, k, v, kc, vc, sink, tabs_q, tabs_k, nb, t):
    nj = t // SW_TQ
    lc = kc.shape[1]
    kvw = SW_KV_HEADS * HEAD_DIM
    prev = lambda j: jnp.maximum(j - 1, 0)
    nxt = lambda j: jnp.minimum(j + 1, nj - 1)
    cur = lambda j: j
    qblk = pl.BlockSpec((SW_TQ, BRANCH_W), lambda b, j: (b * nj + j, 0))
    kblk = lambda f: pl.BlockSpec((SW_TQ, kvw), lambda b, j: (b * nj + f(j), 0))
    ctx = pl.BlockSpec((None, lc, kvw), lambda b, j: (b, 0, 0))
    tq = pl.BlockSpec((SW_TQ, BRANCH_W), lambda b, j: (j, 0))
    tk = lambda f: pl.BlockSpec((SW_TQ, kvw), lambda b, j: (f(j), 0))
    in_specs = [qblk, kblk(prev), kblk(cur), kblk(nxt), kblk(prev), kblk(cur), kblk(nxt), ctx, ctx, tq, tq, tq]
    args = [q, k, k, k, v, v, v, kc, vc, *tabs_q]
    for f in (prev, cur, nxt):
        in_specs += [tk(f)] * 3
        args += list(tabs_k)
    in_specs.append(pl.BlockSpec(memory_space=pltpu.SMEM))
    args.append(sink)
    return pl.pallas_call(
        functools.partial(_sw_kernel, t=t), grid=(nb, nj), in_specs=in_specs, out_specs=qblk,
        out_shape=jax.ShapeDtypeStruct((nb * t, BRANCH_W), F32),
        compiler_params=_cp(("parallel", "arbitrary"), 48), name="sw_attention",
    )(*args)


MLA_TQ = 512
MLA_TK = 512


def _mla_kernel(qm, kn, kr, vt, knc, krc, vtc, cq, saq, sbq, ck, sak, sbk, o, qt_ref, m_ref, l_ref, acc_ref, *,
                n_lat):
    kj = pl.program_id(2)
    half = MLA_ROPE // 4

    @pl.when(kj == 0)
    def _():
        qt_ref[...] = _rope(qm[...], cq[...], saq[...], sbq[...], half).T.astype(BF16)
        m_ref[...] = jnp.full(m_ref.shape, NEG, F32)
        l_ref[...] = jnp.zeros(l_ref.shape, F32)
        acc_ref[...] = jnp.zeros(acc_ref.shape, F32)

    def step(kcat, vt_b):
        for h in range(MLA_HEADS):
            ks = slice(h * LANE, (h + 1) * LANE)
            vs = slice(h * MLA_V, (h + 1) * MLA_V)
            s = _dot(kcat[:, ks], qt_ref[ks, :]) * MLA_SCALE
            m_old = m_ref[h]
            m_new = jnp.maximum(m_old, jnp.max(s, axis=0, keepdims=True))
            alpha = jnp.exp(m_old - m_new)
            p = jnp.exp(s - m_new)
            l_ref[h] = alpha * l_ref[h] + jnp.sum(p, axis=0, keepdims=True)
            acc_ref[vs, :] = alpha * acc_ref[vs, :] + _dot(vt_b[vs, :], p.astype(BF16))
            m_ref[h] = m_new

    @pl.when(kj < n_lat)
    def _():
        krr = _rope(kr[...], ck[...], sak[...], sbk[...], half)
        step((kn[...] + jnp.concatenate([krr] * MLA_HEADS, axis=1)).astype(BF16), vt[...].astype(BF16))

    @pl.when(kj == n_lat)
    def _():
        step((knc[...] + jnp.concatenate([krc[...]] * MLA_HEADS, axis=1)).astype(BF16), vtc[...].astype(BF16))
        for h in range(MLA_HEADS):
            vs = slice(h * MLA_V, (h + 1) * MLA_V)
            acc_ref[vs, :] = acc_ref[vs, :] / l_ref[h]
        o[...] = acc_ref[...].T


def _mla_attention(qm, kn, kr, vt, knc, krc, vtc, tabs_q, tabs_k, nb, t):
    nq, n_lat = t // MLA_TQ, t // MLA_TK
    lc = krc.shape[1]
    assert lc == MLA_TK
    kb = lambda b, i, kj: b * n_lat + jnp.minimum(kj, n_lat - 1)
    kblk = lambda wd: pl.BlockSpec((MLA_TK, wd), lambda b, i, kj: (kb(b, i, kj), 0))
    tkb = pl.BlockSpec((MLA_TK, LANE), lambda b, i, kj: (jnp.minimum(kj, n_lat - 1), 0))
    tqb = pl.BlockSpec((MLA_TQ, MLA_QK_W), lambda b, i, kj: (i, 0))
    return pl.pallas_call(
        functools.partial(_mla_kernel, n_lat=n_lat), grid=(nb, nq, n_lat + 1),
        in_specs=[pl.BlockSpec((MLA_TQ, MLA_QK_W), lambda b, i, kj: (b * nq + i, 0)),
                  kblk(MLA_QK_W), kblk(LANE),
                  pl.BlockSpec((BRANCH_W, MLA_TK), lambda b, i, kj: (0, kb(b, i, kj))),
                  pl.BlockSpec((lc, MLA_QK_W), lambda b, i, kj: (b, 0)),
                  pl.BlockSpec((None, lc, LANE), lambda b, i, kj: (b, 0, 0)),
                  pl.BlockSpec((BRANCH_W, lc), lambda b, i, kj: (0, b)),
                  tqb, tqb, tqb, tkb, tkb, tkb],
        out_specs=pl.BlockSpec((MLA_TQ, BRANCH_W), lambda b, i, kj: (b * nq + i, 0)),
        out_shape=jax.ShapeDtypeStruct((nb * t, BRANCH_W), F32),
        scratch_shapes=[pltpu.VMEM((MLA_QK_W, MLA_TQ), BF16), pltpu.VMEM((MLA_HEADS, 1, MLA_TQ), F32),
                        pltpu.VMEM((MLA_HEADS, 1, MLA_TQ), F32), pltpu.VMEM((BRANCH_W, MLA_TQ), F32)],
        compiler_params=_cp(("parallel", "parallel", "arbitrary"), 48), name="mla_attention",
    )(qm, kn, kr, vt, knc, krc, vtc, *tabs_q, *tabs_k)


def _log_sigmoid(x):
    return jnp.minimum(x, 0.0) - jnp.log(1.0 + jnp.exp(-jnp.abs(x)))


def _split3(x):
    hi = x.astype(BF16)
    r1 = x - hi.astype(F32)
    mid = r1.astype(BF16)
    lo = (r1 - mid.astype(F32)).astype(BF16)
    return jnp.concatenate([hi, mid, lo], axis=1)


def _mlstm_dir(d, q_ref, k_ref, v_ref, if_ref, o_ref, c_aug, m_prev, negsel_ref):
    ln = ML_CHUNK
    scale = HEAD_DIM ** -0.5
    t_in = if_ref[...]
    logf = _log_sigmoid(t_in)
    ri = lax.broadcasted_iota(jnp.int32, (ln, ln), 0)
    ci = lax.broadcasted_iota(jnp.int32, (ln, ln), 1)
    mask = (ci <= ri) if d == 0 else (ci >= ri)
    tri = jnp.where(mask, 1.0, 0.0).astype(BF16)
    fs = _split3(logf)
    bcol = _dot(tri, fs[:, :LANE]) + _dot(tri, fs[:, LANE:2 * LANE]) + _dot(tri, fs[:, 2 * LANE:])
    b = pltpu.roll(bcol, LANE - 2 * ML_HEADS, 1)
    a = t_in - b
    row = lax.broadcasted_iota(jnp.int32, (ln, LANE), 0)
    cm, sh = a, 1
    while sh < ln:
        if d == 0:
            cm = jnp.maximum(cm, jnp.where(row >= sh, pltpu.roll(cm, sh, 0), NEG))
        else:
            cm = jnp.maximum(cm, jnp.where(row < ln - sh, pltpu.roll(cm, ln - sh, 0), NEG))
        sh *= 2
    big_m = jnp.maximum(m_prev, cm)
    xs = _split3(big_m - m_prev)
    ys = _split3(b + big_m)
    at = (a - m_prev).T
    last = ln - 1 if d == 0 else 0
    m_last = big_m[last:last + 1, :]
    m_new = b[last:last + 1, :] + m_last
    decay = jnp.exp(m_prev - m_last)
    wt = jnp.exp(a - m_last).T
    q, k, v = q_ref[...], k_ref[...], v_ref[...]
    new_c = []
    for h in range(ML_HEADS):
        c = d * ML_HEADS + h
        hsl = slice(h * LANE, (h + 1) * LANE)
        neg_x = _dot(xs, negsel_ref[c])
        neg_y = _dot(ys, negsel_ref[c])
        w_intra = jnp.exp(jnp.where(mask, at[c:c + 1, :] + neg_x, NEG))
        kh = k[:, hsl] * scale
        qb, kb, vb = q[:, hsl].astype(BF16), kh.astype(BF16), v[:, hsl].astype(BF16)
        s = _dot_nt(qb, kb) * w_intra
        c_prev = c_aug[c]
        nd = _dot(s.astype(BF16), vb) + jnp.exp(neg_x) * _dot(qb, c_prev.astype(BF16))
        den = pltpu.roll(nd, HEAD_DIM, 1)
        hout = nd / jnp.maximum(jnp.abs(den), jnp.exp(neg_y))
        o_ref[:, h * HEAD_DIM:(h + 1) * HEAD_DIM] = hout[:, :HEAD_DIM]
        new_c.append(decay[:, c:c + 1] * c_prev + _dot((kh.T * wt[c:c + 1, :]).astype(BF16), vb))
    return new_c, m_new


def _mlstm_kernel(qf, kf, vf, iff, qb, kb, vb, ifb, c0, m0, negsel, hf, hb, c_out, m_out, c_s, m_s, *, nc):
    c = pl.program_id(1)
    nst = 2 * ML_HEADS

    @pl.when(c == 0)
    def _():
        c_s[...] = c0[...]
        m_s[...] = m0[...]

    c_aug = [c_s[i] for i in range(nst)]
    m_prev = m_s[...]
    cf, mf = _mlstm_dir(0, qf, kf, vf, iff, hf, c_aug, m_prev, negsel)
    cb, mb = _mlstm_dir(1, qb, kb, vb, ifb, hb, c_aug, m_prev, negsel)
    for i, cn in enumerate(cf + cb):
        c_s[i] = cn
    lane = lax.broadcasted_iota(jnp.int32, (1, LANE), 1)
    m_s[...] = jnp.where(lane < ML_HEADS, mf, mb)

    @pl.when(c == nc - 1)
    def _():
        c_out[...] = c_s[...]
        m_out[...] = m_s[...]


def _mlstm(q, k, v, gif, c0, n0, m0, nb, t):
    nc = t // ML_CHUNK
    nst = 2 * ML_HEADS
    hd = HEAD_DIM
    qkw = ML_HEADS * LANE
    c_aug0 = jnp.concatenate([c0.reshape(nb, nst, hd, hd),
                              jnp.broadcast_to(n0.reshape(nb, nst, hd, 1), (nb, nst, hd, hd))], axis=-1)
    c_aug0 = jnp.pad(c_aug0, ((0, 0), (0, 0), (0, LANE - hd), (0, 0)))
    m_row0 = jnp.pad(m0.reshape(nb, 1, nst), ((0, 0), (0, 0), (0, LANE - nst)))
    sel = np.zeros((nst, 3 * LANE, LANE), np.float32)
    for ch in range(nst):
        sel[ch, [ch, LANE + ch, 2 * LANE + ch], :] = -1.0
    fw = lambda wd: pl.BlockSpec((ML_CHUNK, wd), lambda b, c: (b * nc + c, 0))
    bw = lambda wd: pl.BlockSpec((ML_CHUNK, wd), lambda b, c: (b * nc + nc - 1 - c, 0))
    st = lambda shp: pl.BlockSpec((None,) + shp, lambda b, c: (b,) + (0,) * len(shp))
    shapes = ((nst, LANE, LANE), (1, LANE))
    hf, hb, c_fin, m_fin = pl.pallas_call(
        functools.partial(_mlstm_kernel, nc=nc), grid=(nb, nc),
        in_specs=[fw(qkw), fw(qkw), fw(qkw), fw(LANE), bw(qkw), bw(qkw), bw(qkw), bw(LANE)]
                 + [st(s) for s in shapes] + [pl.BlockSpec(sel.shape, lambda b, c: (0, 0, 0))],
        out_specs=[fw(BRANCH_W), bw(BRANCH_W)] + [st(s) for s in shapes],
        out_shape=[jax.ShapeDtypeStruct((nb * t, BRANCH_W), F32)] * 2
                  + [jax.ShapeDtypeStruct((nb,) + s, F32) for s in shapes],
        scratch_shapes=[pltpu.VMEM(s, F32) for s in shapes],
        compiler_params=_cp(("parallel", "arbitrary")), name="mlstm_scan",
    )(q, k, v, gif, q, k, v, gif, c_aug0, m_row0, jnp.asarray(sel, BF16))
    c_new = c_fin[:, :, :hd, :hd].reshape(nb, 2, ML_HEADS, hd, hd)
    n_new = c_fin[:, :, :hd, hd].reshape(nb, 2, ML_HEADS, hd)
    m_new = m_fin[:, 0, :nst].reshape(nb, 2, ML_HEADS)
    return hf, hb, c_new, n_new, m_new


def _prep_weights(w_in, b_in, mla_q_norm, w_uq, w_uk, w_uv, w_branch, w_out, peer_wq, peer_keys, peer_u, peer_v):
    offs = np.cumsum((0,) + IN_SPLITS)
    seg = lambda a, i: a[..., offs[i]:offs[i + 1]]
    padr = lambda a, wd, lead=0: jnp.pad(a, [(0, 0)] * (a.ndim - 1) + [(lead, wd - a.shape[-1] - lead)])

    def headpad(a, fill):
        a4 = a.reshape(a.shape[:-1] + (ML_HEADS, HEAD_DIM))
        a4 = jnp.pad(a4, [(0, 0)] * (a4.ndim - 1) + [(0, LANE - HEAD_DIM)], constant_values=fill)
        return a4.reshape(a.shape[:-1] + (ML_HEADS * LANE,))

    def layout(a, is_bias):
        ml_if = padr(jnp.concatenate([seg(a, 7), seg(a, 8)], axis=-1), LANE)
        ml = [headpad(seg(a, 3), 0.0), headpad(seg(a, 4), 0.0), headpad(seg(a, 5), 1.0 if is_bias else 0.0)]
        parts = [seg(a, i) for i in range(3)] + ml + [seg(a, 6), ml_if, seg(a, 9), seg(a, 10), seg(a, 11),
                                                       padr(seg(a, 12), 256), seg(a, 13),
                                                       padr(seg(a, 14), LANE, KROPE_SLICE.start)]
        return jnp.concatenate(parts, axis=-1)

    nl = w_in.shape[0]
    w_uq_p = padr(w_uq.reshape(nl, MLA_Q_RANK, MLA_HEADS, MLA_NOPE + MLA_ROPE), LANE).reshape(nl, MLA_Q_RANK, -1)
    w_uq_p = jnp.pad(w_uq_p, ((0, 0), (0, 256 - MLA_Q_RANK), (0, 0)))
    w_uk_p = padr(w_uk.reshape(nl, MLA_KV_RANK, MLA_HEADS, MLA_NOPE), LANE).reshape(nl, MLA_KV_RANK, -1)
    return dict(
        w_ab=layout(w_in, False).astype(BF16), b_ab=layout(b_in, True),
        w_c=seg(w_in, 15).astype(BF16), b_c=seg(b_in, 15),
        q_norm=padr(mla_q_norm, 256), w_uq=w_uq_p.astype(BF16),
        w_ukv=jnp.concatenate([w_uk_p, w_uv], axis=-1).astype(BF16),
        w_branch=w_branch.astype(BF16), w_out=w_out.astype(BF16), peer_wq=peer_wq.astype(BF16),
        peer_keys=peer_keys.astype(BF16), peer_u=peer_u.astype(BF16),
        peer_vt=jnp.swapaxes(peer_v.reshape(nl, -1, PEER_EC, peer_v.shape[-1]), 2, 3).astype(BF16),
    )


def _project(x, wl, l, norm_g, sc, sh, rows_per_mod):
    outs = _linear(x, wl["w_ab"][l], [wd for _, wd in SEGS_AB], bias=wl["b_ab"][l], norm_g=norm_g,
                   mod=(sc, sh), rows_per_mod=rows_per_mod, tm=512, vmem_mb=52, name="in_proj_ab")
    p = {name: o for (name, _), o in zip(SEGS_AB, outs)}
    (p["gate"],) = _linear(x, wl["w_c"][l], [N_BRANCH * D_MODEL], bias=wl["b_c"][l], norm_g=norm_g,
                           mod=(sc, sh), rows_per_mod=rows_per_mod, tm=512, vmem_mb=52, name="in_proj_gate")
    (qm,) = _linear(p["cq"], wl["w_uq"][l], [MLA_QK_W], norm_g=wl["q_norm"][l], k_valid=MLA_Q_RANK, tm=2048,
                    name="mla_q_up")
    return p, qm


def kernel(x_prompt, x_sample, cache_na_k, cache_na_v, state_mlstm_C, state_mlstm_n, state_mlstm_m,
           cache_swa_k, cache_swa_v, cache_mla_ckv, cache_mla_krope, c, c_ctx, w_mod, b_mod, norm1_g, norm2_g,
           w_in, b_in, na_rpb, sw_sink, mla_q_norm, w_uq, mla_kv_norm, w_uk, w_uv, w_branch, w_out, peer_wq,
           peer_keys, peer_u, peer_v, final_norm_g):
    nbp, tp, d = x_prompt.shape
    nbs, ts, _ = x_sample.shape
    nl = w_in.shape[0]
    lc = cache_na_k.shape[2]
    nst = 2 * ML_HEADS
    wl = _prep_weights(w_in, b_in, mla_q_norm, w_uq, w_uk, w_uv, w_branch, w_out, peer_wq, peer_keys,
                       peer_u, peer_v)

    n_cond = 1 + nbs
    cond = jnp.zeros((16, d), F32).at[0].set(c_ctx).at[1:n_cond].set(c)
    mods = _modulation(cond, w_mod, b_mod).reshape(nl, 16, 6, 1, d)

    tabs_swq = _rope_tables(ts, SW_HEADS, HEAD_DIM)
    tabs_swk = _rope_tables(ts, SW_KV_HEADS, HEAD_DIM)
    tabs_mlaq = _rope_tables(ts, MLA_HEADS, MLA_ROPE, lead=MLA_NOPE, width=LANE)
    tabs_mlak = _rope_tables(ts, 1, MLA_ROPE, lead=MLA_NOPE, width=LANE)
    na_bias = _na_bias_tables(na_rpb, ts // GRID_W)
    krope_c = jnp.pad(cache_mla_krope, ((0, 0), (0, 0), (0, 0), (KROPE_SLICE.start, LANE - KROPE_SLICE.stop)))

    xp = x_prompt.reshape(nbp * tp, d)
    xs = x_sample.reshape(nbs * ts, d)
    zeros_c = jnp.zeros((nbp, 2, ML_HEADS, HEAD_DIM, HEAD_DIM), F32)
    zeros_n = jnp.zeros((nbp, 2, ML_HEADS, HEAD_DIM), F32)
    zeros_m = jnp.zeros((nbp, 2, ML_HEADS), F32)
    per_layer = tuple([] for _ in range(9))

    for l in range(nl):
        mod_p = [mods[l, 0:1, i] for i in range(6)]
        mod_s = [mods[l, 1:n_cond, i] for i in range(6)]
        sink = sw_sink[l]

        p, qm = _project(xp, wl, l, norm1_g[l], mod_p[1], mod_p[0], nbp * tp)
        ckv_n, kn, vm = _linear(p["ckv"], wl["w_ukv"][l], [MLA_QK_W, 256], norm_g=mla_kv_norm[l],
                                emit_normed=True, tm=2048, name="mla_kv_up")
        o_na, o_sw, o_mla = _ctx_attention(p, qm, kn, vm, sink, nbp, tp)
        hf, hb, c_new, n_new, m_new = _mlstm(p["ml_q"], p["ml_k"], p["ml_v"], p["ml_if"], zeros_c, zeros_n,
                                             zeros_m, nbp, tp)
        xp = _merge(o_na, hf, hb, p["ml_o"], o_sw, o_mla, p["gate"], wl["w_branch"][l], wl["w_out"][l], xp,
                    mod_p[2], nbp * tp)
        xp = _peer(xp, norm2_g[l], mod_p[4], mod_p[3], mod_p[5], wl["peer_wq"][l], wl["peer_keys"][l],
                   wl["peer_u"][l], wl["peer_vt"][l], nbp * tp)
        ctx_t = (p["na_k"].reshape(nbp, tp, NA_HEADS, HEAD_DIM), p["na_v"].reshape(nbp, tp, NA_HEADS, HEAD_DIM),
                 c_new, n_new, m_new,
                 p["sw_k"].reshape(nbp, tp, SW_KV_HEADS, HEAD_DIM), p["sw_v"].reshape(nbp, tp, SW_KV_HEADS, HEAD_DIM),
                 ckv_n.reshape(nbp, tp, MLA_KV_RANK), p["krope"][:, KROPE_SLICE].reshape(nbp, tp, MLA_ROPE))
        for i, a in enumerate(ctx_t):
            per_layer[i].append(a)

        p, qm = _project(xs, wl, l, norm1_g[l], mod_s[1], mod_s[0], ts)
        _, kn, vt = _linear(p["ckv"], wl["w_ukv"][l], [MLA_QK_W, 256], norm_g=mla_kv_norm[l], emit_normed=True,
                            transpose_last=True, tm=2048, name="mla_kv_up_t")
        knc, vtc = _linear(cache_mla_ckv[:, l].reshape(nbs * lc, MLA_KV_RANK), wl["w_ukv"][l], [MLA_QK_W, 256],
                           transpose_last=True, tm=2048, name="mla_kv_up_cache")
        o_na = _na_attention(p["na_q"], p["na_k"], p["na_v"], cache_na_k[:, l].reshape(nbs, lc, BRANCH_W),
                             cache_na_v[:, l].reshape(nbs, lc, BRANCH_W), na_bias[l], nbs, ts)
        o_sw = _sw_attention(p["sw_q"], p["sw_k"], p["sw_v"],
                             cache_swa_k[:, l].reshape(nbs, lc, SW_KV_HEADS * HEAD_DIM),
                             cache_swa_v[:, l].reshape(nbs, lc, SW_KV_HEADS * HEAD_DIM), sink, tabs_swq, tabs_swk,
                             nbs, ts)
        o_mla = _mla_attention(qm, kn, p["krope"], vt, knc, krope_c[:, l], vtc, tabs_mlaq, tabs_mlak, nbs, ts)
        hf, hb, _, _, _ = _mlstm(p["ml_q"], p["ml_k"], p["ml_v"], p["ml_if"], state_mlstm_C[:, l],
                                 state_mlstm_n[:, l], state_mlstm_m[:, l], nbs, ts)
        xs = _merge(o_na, hf, hb, p["ml_o"], o_sw, o_mla, p["gate"], wl["w_branch"][l], wl["w_out"][l], xs,
                    mod_s[2], ts)
        xs = _peer(xs, norm2_g[l], mod_s[4], mod_s[3], mod_s[5], wl["peer_wq"][l], wl["peer_keys"][l],
                   wl["peer_u"][l], wl["peer_vt"][l], ts)

    y_prompt = _final_norm(xp, final_norm_g).reshape(nbp, tp, d)
    y_sample = _final_norm(xs, final_norm_g).reshape(nbs, ts, d)
    return (y_prompt, y_sample) + tuple(jnp.stack(s, axis=1) for s in per_layer)
```
